```python
import math
import jax, jax.numpy as jnp
from jax import lax
import numpy as np

D_MODEL = 1024
BATCH = 4
SEQ = 4096
DEPTH = 2
DEC_BATCH = 32
DEC_SEQ = 2048
PAST_LEN = 128

HEAD_DIM = 64
N_RET_HEADS = 4
N_FNET_GROUPS = 4
N_DIL_HEADS = 4
DIL_PAIRS = ((128, 1), (512, 4), (2048, 16))
N_DIL_GROUPS = 3
N_MLA_HEADS = 4
MLA_NOPE = 64
MLA_ROPE = 32
MLA_V = 64
Q_LORA = 256
KV_LORA = 128
D_FF = 2816
RET_CHUNK = 128
Q_BLOCK = 128
ROPE_THETA = 500000.0
RET_THETA = 10000.0
PARTIAL_ROT = HEAD_DIM // 4
EPS = 1e-6
NEG = -1e30

RET_W = N_RET_HEADS * HEAD_DIM
FNET_W = N_FNET_GROUPS * HEAD_DIM
DIL_W = N_DIL_GROUPS * N_DIL_HEADS * HEAD_DIM
D_IN = 4 * RET_W + FNET_W + 3 * DIL_W + Q_LORA + KV_LORA + MLA_ROPE
D_MIX = RET_W + FNET_W + N_DIL_HEADS * HEAD_DIM + N_MLA_HEADS * MLA_V
SPLIT_SIZES = (RET_W, RET_W, RET_W, RET_W, FNET_W, DIL_W, DIL_W, DIL_W, Q_LORA, KV_LORA, MLA_ROPE)

kernel_name = 'hybrid_bidir_parallel_heads_encoder'


def rms_norm(x, g):
    xf = x.astype(jnp.float32)
    y = xf * lax.rsqrt(jnp.mean(xf * xf, axis=-1, keepdims=True) + EPS)
    return (y * g.astype(jnp.float32)).astype(x.dtype)


def rope(x, pos, theta, rot_dim):
    half = rot_dim // 2
    inv = jnp.power(theta, -jnp.arange(half, dtype=jnp.float32) * 2.0 / rot_dim)
    ang = pos.astype(jnp.float32)[:, None] * inv[None, :]
    cos = jnp.cos(ang)[:, None, :]
    sin = jnp.sin(ang)[:, None, :]
    xf = x.astype(jnp.float32)
    x1 = xf[..., :half]
    x2 = xf[..., half:rot_dim]
    out = jnp.concatenate([x1 * cos - x2 * sin, x2 * cos + x1 * sin, xf[..., rot_dim:]], axis=-1)
    return out.astype(x.dtype)


def retention_one_direction(q, k, v, log_gamma, include_diag):
    B, H, S, d = q.shape
    C = RET_CHUNK
    N = S // C
    qc = q.reshape(B, H, N, C, d)
    kc = k.reshape(B, H, N, C, d)
    vc = v.reshape(B, H, N, C, d)
    idx = jnp.arange(C, dtype=jnp.float32)
    diff = idx[:, None] - idx[None, :]
    mask = (diff >= 0) if include_diag else (diff > 0)
    lg = log_gamma[:, None, None]
    decay = jnp.where(mask[None], jnp.exp(jnp.where(mask, diff, 0.0)[None] * lg), 0.0)
    scores = jnp.einsum('bhncd,bhnjd->bhncj', qc, kc) * decay[None, :, None]
    intra = jnp.einsum('bhncj,bhnje->bhnce', scores, vc)
    k_dec = kc * jnp.exp((C - 1 - idx)[None, :] * log_gamma[:, None])[None, :, None, :, None]
    kv = jnp.einsum('bhnjd,bhnje->nbhde', k_dec, vc)
    chunk_decay = jnp.exp(C * log_gamma)[None, :, None, None]

    def step(state, kv_n):
        return state * chunk_decay + kv_n, state

    _, prev = lax.scan(step, jnp.zeros(kv.shape[1:], jnp.float32), kv)
    q_dec = qc * jnp.exp((idx + 1.0)[None, :] * log_gamma[:, None])[None, :, None, :, None]
    cross = jnp.einsum('bhncd,nbhde->bhnce', q_dec, prev)
    return (intra + cross).reshape(B, H, S, d)


def retention_mixer(q, k, v, g, pos, dec_f, dec_b):
    B, S, _ = q.shape
    shp = (B, S, N_RET_HEADS, HEAD_DIM)
    q = rope(q.reshape(shp), pos, RET_THETA, HEAD_DIM)
    k = rope(k.reshape(shp), pos, RET_THETA, HEAD_DIM) * (HEAD_DIM ** -0.5)
    qh = q.transpose(0, 2, 1, 3).astype(jnp.float32)
    kh = k.transpose(0, 2, 1, 3).astype(jnp.float32)
    vh = v.reshape(shp).transpose(0, 2, 1, 3).astype(jnp.float32)
    lf = jax.nn.log_sigmoid(dec_f.astype(jnp.float32))
    lb = jax.nn.log_sigmoid(dec_b.astype(jnp.float32))
    fwd = retention_one_direction(qh, kh, vh, lf, True)
    bwd = retention_one_direction(qh[:, :, ::-1], kh[:, :, ::-1], vh[:, :, ::-1], lb, False)[:, :, ::-1]
    o = fwd + bwd
    o = o * lax.rsqrt(jnp.mean(o * o, axis=-1, keepdims=True) + EPS)
    o = o.transpose(0, 2, 1, 3).reshape(B, S, RET_W).astype(g.dtype)
    return jax.nn.silu(g) * o


def fourier_mixer(u, w_fmix):
    B, S, _ = u.shape
    ug = u.reshape(B, S, N_FNET_GROUPS, HEAD_DIM).astype(jnp.float32)
    f = jnp.fft.fft2(ug, axes=(1, 3), norm='ortho').real.astype(u.dtype)
    return jnp.einsum('bsgc,gce->bsge', f, w_fmix).reshape(B, S, FNET_W)


def dilated_group(q, k, v, dil, radius):
    B, S, H, dh = q.shape
    L = S // dil
    R = radius
    nblk = -(-L // R)
    Lp = nblk * R

    def to_sub(t, left, right):
        t = t.reshape(B, L, dil, H, dh).transpose(0, 2, 1, 3, 4)
        return jnp.pad(t, ((0, 0), (0, 0), (left, right), (0, 0), (0, 0)))

    qs = to_sub(q, 0, Lp - L).reshape(B, dil, nblk, R, H, dh)
    ks = to_sub(k, R, Lp - L + R).reshape(B, dil, nblk + 2, R, H, dh)
    vs = to_sub(v, R, Lp - L + R).reshape(B, dil, nblk + 2, R, H, dh)
    kw = jnp.concatenate([ks[:, :, :-2], ks[:, :, 1:-1], ks[:, :, 2:]], axis=3)
    vw = jnp.concatenate([vs[:, :, :-2], vs[:, :, 1:-1], vs[:, :, 2:]], axis=3)
    qi = jnp.arange(nblk)[:, None] * R + jnp.arange(R)[None, :]
    kj = jnp.arange(nblk)[:, None] * R - R + jnp.arange(3 * R)[None, :]
    valid = (jnp.abs(qi[:, :, None] - kj[:, None, :]) <= R) & (kj[:, None, :] >= 0) & (kj[:, None, :] < L)
    s = jnp.einsum('bgnqhd,bgnkhd->bgnhqk', qs, kw).astype(jnp.float32) * (dh ** -0.5)
    s = jnp.where(valid[None, None, :, None], s, NEG)
    m = jnp.max(s, axis=-1, keepdims=True)
    p = jnp.exp(s - m)
    den = jnp.sum(p, axis=-1)
    o = jnp.einsum('bgnhqk,bgnkhd->bgnqhd', p, vw.astype(jnp.float32))
    o = o / den.transpose(0, 1, 2, 4, 3)[..., None]
    lse = (m[..., 0] + jnp.log(den)).transpose(0, 1, 2, 4, 3)
    o = o.reshape(B, dil, Lp, H, dh)[:, :, :L].transpose(0, 2, 1, 3, 4).reshape(B, S, H, dh)
    lse = lse.reshape(B, dil, Lp, H)[:, :, :L].transpose(0, 2, 1, 3).reshape(B, S, H)
    return o, lse


def dilated_mixer(q, k, v, pos):
    B, S, _ = q.shape
    shp = (B, S, N_DIL_GROUPS, N_DIL_HEADS, HEAD_DIM)
    q = q.reshape(shp)
    k = k.reshape(shp)
    v = v.reshape(shp)
    outs, lses = [], []
    for gi, (win, dil) in enumerate(DIL_PAIRS):
        qg = rope(q[:, :, gi], pos, ROPE_THETA, PARTIAL_ROT)
        kg = rope(k[:, :, gi], pos, ROPE_THETA, PARTIAL_ROT)
        o, lse = dilated_group(qg, kg, v[:, :, gi], dil, win // (2 * dil))
        outs.append(o)
        lses.append(lse)
    wgt = jax.nn.softmax(jnp.stack(lses, 0), axis=0)
    o = jnp.sum(wgt[..., None] * jnp.stack(outs, 0), axis=0)
    return o.reshape(B, S, N_DIL_HEADS * HEAD_DIM).astype(q.dtype)


def mla_mixer(c_q, c_kv, k_rope, pos, q_norm, w_qb, kv_norm, w_kvb):
    B, S, _ = c_q.shape
    H = N_MLA_HEADS
    q = (rms_norm(c_q, q_norm) @ w_qb).reshape(B, S, H, MLA_NOPE + MLA_ROPE)
    q_nope = q[..., :MLA_NOPE]
    q_pe = rope(q[..., MLA_NOPE:], pos, ROPE_THETA, MLA_ROPE)
    kv = (rms_norm(c_kv, kv_norm) @ w_kvb).reshape(B, S, H, MLA_NOPE + MLA_V)
    k_nope = kv[..., :MLA_NOPE]
    v = kv[..., MLA_NOPE:]
    k_pe = rope(k_rope[:, :, None, :], pos, ROPE_THETA, MLA_ROPE)[:, :, 0]
    scale = (MLA_NOPE + MLA_ROPE) ** -0.5
    nq = S // Q_BLOCK
    qn_b = q_nope.reshape(B, nq, Q_BLOCK, H, MLA_NOPE).transpose(1, 0, 2, 3, 4)
    qp_b = q_pe.reshape(B, nq, Q_BLOCK, H, MLA_ROPE).transpose(1, 0, 2, 3, 4)

    def block(args):
        qn, qp = args
        s = (jnp.einsum('bqhd,bkhd->bhqk', qn, k_nope) + jnp.einsum('bqhr,bkr->bhqk', qp, k_pe)).astype(jnp.float32) * scale
        p = jax.nn.softmax(s, axis=-1).astype(v.dtype)
        return jnp.einsum('bhqk,bkhd->bqhd', p, v)

    o = lax.map(block, (qn_b, qp_b))
    return o.transpose(1, 0, 2, 3, 4).reshape(B, S, H * MLA_V)


def dwconv3(u, w, b):
    S = u.shape[1]
    up = jnp.pad(u, ((0, 0), (1, 1), (0, 0)))
    return up[:, :S] * w[0] + up[:, 1:S + 1] * w[1] + up[:, 2:] * w[2] + b


def trunk(x, c, params):
    S = x.shape[1]
    pos = jnp.arange(S)
    cond = jax.nn.silu(c)
    split_pts = [int(i) for i in np.cumsum(SPLIT_SIZES)[:-1]]
    for l in range(DEPTH):
        (w_ada, b_ada, n_pre_mix, w_in, dec_f, dec_b, w_fmix, q_norm, w_qb, kv_norm, w_kvb,
         w_out, n_post_mix, n_pre_ffn, w_up, conv_w, conv_b, w_down, n_post_ffn) = [p[l] for p in params]
        mod = cond @ w_ada + b_ada
        sh1, sc1, g1, sh2, sc2, g2 = [m[:, None, :] for m in jnp.split(mod, 6, axis=-1)]
        h = rms_norm(x, n_pre_mix) * (1.0 + sc1) + sh1
        z = h @ w_in
        rq, rk, rv, rg, fu, dq, dk, dv, cq, ckv, kr = jnp.split(z, split_pts, axis=-1)
        o = jnp.concatenate([
            retention_mixer(rq, rk, rv, rg, pos, dec_f, dec_b),
            fourier_mixer(fu, w_fmix),
            dilated_mixer(dq, dk, dv, pos),
            mla_mixer(cq, ckv, kr, pos, q_norm, w_qb, kv_norm, w_kvb)], axis=-1)
        x = x + g1 * rms_norm(o @ w_out, n_post_mix)
        h = rms_norm(x, n_pre_ffn) * (1.0 + sc2) + sh2
        u = dwconv3(h @ w_up, conv_w, conv_b)
        a, bu = jnp.split(u, 2, axis=-1)
        x = x + g2 * rms_norm((jax.nn.silu(a) * bu) @ w_down, n_post_ffn)
    return x


def setup_inputs(seed: int = 0) -> dict:
    key = jax.random.key(seed)
    ks = jax.random.split(key, 24)
    f32 = jnp.float32

    def nrm(k, shape, scale):
        return jax.random.normal(k, shape, f32) * scale

    L = DEPTH
    hh = jnp.arange(N_RET_HEADS, dtype=f32)
    decay_init = jnp.log(jnp.power(2.0, 5.0 + hh) - 1.0)
    return {
        'x_prompt': nrm(ks[0], (BATCH, SEQ, D_MODEL), 1.0),
        'x_sample': nrm(ks[1], (DEC_BATCH, DEC_SEQ, D_MODEL), 1.0),
        'c_prompt': nrm(ks[2], (BATCH, D_MODEL), 1.0),
        'c_sample': nrm(ks[3], (DEC_BATCH, D_MODEL), 1.0),
        'w_ada': nrm(ks[4], (L, D_MODEL, 6 * D_MODEL), D_MODEL ** -0.5),
        'b_ada': nrm(ks[5], (L, 6 * D_MODEL), 0.01),
        'norm_pre_mix': 1.0 + nrm(ks[6], (L, D_MODEL), 0.05),
        'w_in': nrm(ks[7], (L, D_MODEL, D_IN), D_MODEL ** -0.5),
        'ret_decay_fwd': decay_init[None] + nrm(ks[8], (L, N_RET_HEADS), 0.1),
        'ret_decay_bwd': decay_init[None] + nrm(ks[9], (L, N_RET_HEADS), 0.1),
        'w_fmix': nrm(ks[10], (L, N_FNET_GROUPS, HEAD_DIM, HEAD_DIM), HEAD_DIM ** -0.5),
        'mla_q_norm': 1.0 + nrm(ks[11], (L, Q_LORA), 0.05),
        'mla_w_qb': nrm(ks[12], (L, Q_LORA, N_MLA_HEADS * (MLA_NOPE + MLA_ROPE)), Q_LORA ** -0.5),
        'mla_kv_norm': 1.0 + nrm(ks[13], (L, KV_LORA), 0.05),
        'mla_w_kvb': nrm(ks[14], (L, KV_LORA, N_MLA_HEADS * (MLA_NOPE + MLA_V)), KV_LORA ** -0.5),
        'w_out': nrm(ks[15], (L, D_MIX, D_MODEL), D_MIX ** -0.5),
        'norm_post_mix': 1.0 + nrm(ks[16], (L, D_MODEL), 0.05),
        'norm_pre_ffn': 1.0 + nrm(ks[17], (L, D_MODEL), 0.05),
        'w_up': nrm(ks[18], (L, D_MODEL, 2 * D_FF), D_MODEL ** -0.5),
        'conv_w': nrm(ks[19], (L, 3, 2 * D_FF), 3 ** -0.5),
        'conv_b': nrm(ks[20], (L, 2 * D_FF), 0.01),
        'w_down': nrm(ks[21], (L, D_FF, D_MODEL), D_FF ** -0.5),
        'norm_post_ffn': 1.0 + nrm(ks[22], (L, D_MODEL), 0.05),
    }


def reference(x_prompt, x_sample, c_prompt, c_sample, w_ada, b_ada, norm_pre_mix, w_in,
              ret_decay_fwd, ret_decay_bwd, w_fmix, mla_q_norm, mla_w_qb, mla_kv_norm, mla_w_kvb,
              w_out, norm_post_mix, norm_pre_ffn, w_up, conv_w, conv_b, w_down, norm_post_ffn):
    params = (w_ada, b_ada, norm_pre_mix, w_in, ret_decay_fwd, ret_decay_bwd, w_fmix,
              mla_q_norm, mla_w_qb, mla_kv_norm, mla_w_kvb, w_out, norm_post_mix,
              norm_pre_ffn, w_up, conv_w, conv_b, w_down, norm_post_ffn)
    y_prompt = trunk(x_prompt, c_prompt, params)
    y_sample = trunk(x_sample, c_sample, params)
    return (y_prompt, y_sample)
```

```python
import functools
import math

import numpy as np
import jax
import jax.numpy as jnp
from jax import lax
from jax.experimental import pallas as pl
from jax.experimental.pallas import tpu as pltpu

F32 = jnp.float32
BF16 = jnp.bfloat16

D_MODEL = 1024
HEAD_DIM = 64
N_HEADS = 4
MIX_W = N_HEADS * HEAD_DIM
DIL_PAIRS = ((128, 1), (512, 4), (2048, 16))
N_DIL_GROUPS = 3
DIL_RADIUS = 64
MLA_NOPE = 64
MLA_ROPE = 32
Q_LORA = 256
KV_LORA = 128
D_FF = 2816
ROPE_THETA = 500000.0
RET_THETA = 10000.0
PARTIAL_ROT = HEAD_DIM // 4
EPS = 1e-6
NEG = -1e30

LANES = 128
D_IN_PAD = 4096
MLA_OFF = 3584
ROW_TILE = 512
RET_CHUNK = 256
DIL_TQ = 128
FOUR_TR = 512
MLA_TQ = 256
FFN_CHUNK = 256
HALO = 16
VMEM_LIMIT = 56 * 1024 * 1024


def _params(*sem):
    return pltpu.CompilerParams(dimension_semantics=sem, vmem_limit_bytes=VMEM_LIMIT)


def _rms(x, g):
    return x * lax.rsqrt(jnp.mean(x * x, axis=-1, keepdims=True) + EPS) * g


def _sigmoid(x):
    return 1.0 / (1.0 + jnp.exp(-x))


def _dot(a, b):
    return jnp.dot(a, b, preferred_element_type=F32)


def _dot_nt(a, b):
    return lax.dot_general(a, b, (((1,), (1,)), ((), ())), preferred_element_type=F32)


def _dot_tn(a, b):
    return lax.dot_general(a, b, (((0,), (0,)), ((), ())), preferred_element_type=F32)


def _ada_kernel(c_ref, w_ref, b_ref, o_ref):
    c = c_ref[...]
    cond = (c * _sigmoid(c)).astype(BF16)
    o_ref[0] = _dot(cond, w_ref[0].astype(BF16)) + b_ref[0]


def _ada(c_all, w_ada, b_ada):
    depth, _, n = w_ada.shape
    rows = c_all.shape[0]
    tn = 1536
    return pl.pallas_call(
        _ada_kernel,
        out_shape=jax.ShapeDtypeStruct((depth, rows, n), F32),
        grid=(depth, n // tn),
        in_specs=[
            pl.BlockSpec((rows, D_MODEL), lambda l, j: (0, 0)),
            pl.BlockSpec((1, D_MODEL, tn), lambda l, j: (l, 0, j)),
            pl.BlockSpec((1, 1, tn), lambda l, j: (l, 0, j)),
        ],
        out_specs=pl.BlockSpec((1, rows, tn), lambda l, j: (l, 0, j)),
        compiler_params=_params("arbitrary", "arbitrary"),
        name="ada",
    )(c_all, w_ada, b_ada.reshape(depth, 1, n))


def _inproj_kernel(x_ref, mod_ref, g_ref, w_ref, tab_ref, qn_ref, kvn_ref, wq_ref, wk_ref, wv_ref,
                   pk_ref, ret_ref, fu_ref, dil_ref, mq_ref, mk_ref, mv_ref):
    x = x_ref[...]
    sh = mod_ref[0, 0:1, :]
    sc = mod_ref[0, 1:2, :]
    hb = (_rms(x, g_ref[...]) * (1.0 + sc) + sh).astype(BF16)

    def mm(c0, c1):
        return _dot(hb, w_ref[:, c0:c1])

    lane = lax.broadcasted_iota(jnp.int32, (1, LANES), 1)
    j64 = lane & (HEAD_DIM - 1)

    def make_rope(cos, sin, lo_mask, hi_mask, half):
        c = jnp.where(lo_mask | hi_mask, cos, 1.0)
        sa = jnp.where(lo_mask, -sin, 0.0)
        sb = jnp.where(hi_mask, sin, 0.0)

        def apply(z):
            return z * c + pltpu.roll(z, LANES - half, 1) * sa + pltpu.roll(z, half, 1) * sb
        return apply

    rope_ret = make_rope(tab_ref[:, 0:128], tab_ref[:, 128:256], j64 < 32, j64 >= 32, 32)
    rope_dil = make_rope(tab_ref[:, 256:384], tab_ref[:, 384:512], j64 < 8, (j64 >= 8) & (j64 < 16), 8)
    cos_m = tab_ref[:, 512:640]
    sin_m = tab_ref[:, 640:768]
    rope_kr = make_rope(cos_m, sin_m, lane < 16, (lane >= 16) & (lane < 32), 16)
    rope_mq = make_rope(cos_m, sin_m, (lane >= 64) & (lane < 80), (lane >= 80) & (lane < 96), 16)

    z = mm(0, 512)
    for c in range(4):
        r = rope_ret(z[:, c * LANES:(c + 1) * LANES])
        if c >= 2:
            r = r * (HEAD_DIM ** -0.5)
        ret_ref[:, c * LANES:(c + 1) * LANES] = r.astype(BF16)
    ret_ref[:, 512:1024] = mm(512, 1024).astype(BF16)
    fu_ref[...] = mm(1024, 1280).astype(BF16)
    for g in range(N_DIL_GROUPS):
        base = g * 3 * MIX_W
        z = mm(1280 + base, 1280 + base + 3 * MIX_W)
        for c in range(4):
            r = rope_dil(z[:, c * LANES:(c + 1) * LANES])
            if c < 2:
                r = r * (HEAD_DIM ** -0.5)
            dil_ref[:, base + c * LANES:base + (c + 1) * LANES] = r.astype(BF16)
        dil_ref[:, base + 512:base + 768] = z[:, 512:768].astype(BF16)
    z = mm(MLA_OFF, D_IN_PAD)
    cqn = _rms(z[:, 0:Q_LORA], qn_ref[...]).astype(BF16)
    q = _dot(cqn, wq_ref[...])
    scale = (MLA_NOPE + MLA_ROPE) ** -0.5
    for h in range(N_HEADS):
        r = rope_mq(q[:, h * LANES:(h + 1) * LANES]) * scale
        mq_ref[:, h * LANES:(h + 1) * LANES] = r.astype(BF16)
    ckvn = _rms(z[:, Q_LORA:Q_LORA + KV_LORA], kvn_ref[...]).astype(BF16)
    kr = rope_kr(z[:, 384:512]).astype(BF16)
    mk_ref[...] = (_dot(ckvn, wk_ref[...]) + _dot(kr, pk_ref[...])).astype(BF16)
    mv_ref[...] = _dot(ckvn, wv_ref[...]).astype(BF16)


def _inproj(x2d, mod, g_pre, w_in_p, tab, qn, kvn, wq_p, wk_p, wv_p, pk, seq):
    t = x2d.shape[0]
    tm = ROW_TILE
    tps = seq // tm
    const = lambda i: (0, 0)
    row = lambda i: (i, 0)
    outs = [(t, 1024), (t, MIX_W), (t, 3 * N_DIL_GROUPS * MIX_W), (t, 512), (t, 512), (t, MIX_W)]
    return pl.pallas_call(
        _inproj_kernel,
        out_shape=[jax.ShapeDtypeStruct(s, BF16) for s in outs],
        grid=(t // tm,),
        in_specs=[
            pl.BlockSpec((tm, D_MODEL), row),
            pl.BlockSpec((1, 6, D_MODEL), lambda i: (i // tps, 0, 0)),
            pl.BlockSpec((1, D_MODEL), const),
            pl.BlockSpec((D_MODEL, D_IN_PAD), const),
            pl.BlockSpec((tm, 6 * LANES), lambda i: (i % tps, 0)),
            pl.BlockSpec((1, Q_LORA), const),
            pl.BlockSpec((1, KV_LORA), const),
            pl.BlockSpec((Q_LORA, 512), const),
            pl.BlockSpec((KV_LORA, 512), const),
            pl.BlockSpec((KV_LORA, MIX_W), const),
            pl.BlockSpec((LANES, 512), const),
        ],
        out_specs=[pl.BlockSpec((tm, s[1]), row) for s in outs],
        compiler_params=_params("arbitrary"),
        name="inproj",
    )(x2d, mod, g_pre, w_in_p, tab, qn, kvn, wq_p, wk_p, wv_p, pk)


def _ret_kernel(lg_ref, q_ref, k_ref, v_ref, g_ref, o_ref, acc_ref, st_ref, dmat_ref, vec_ref, rdec_ref):
    c = RET_CHUNK
    seq = q_ref.shape[1]
    n_chunks = seq // c
    lane_head = lax.broadcasted_iota(jnp.int32, (1, MIX_W), 1) // HEAD_DIM
    row_head = lax.broadcasted_iota(jnp.int32, (MIX_W, 1), 0) // HEAD_DIM
    blockdiag = row_head == lane_head

    def per_head(idx, d):
        out = lg_ref[d, 0]
        for h in range(1, N_HEADS):
            out = jnp.where(idx == h, lg_ref[d, h], out)
        return out

    @pl.when(pl.program_id(0) == 0)
    def _tables():
        ri = lax.broadcasted_iota(jnp.int32, (c, c), 0)
        ci = lax.broadcasted_iota(jnp.int32, (c, c), 1)
        diff = (ri - ci).astype(F32)
        for h in range(N_HEADS):
            fwd = jnp.exp(jnp.where(diff >= 0, diff, 0.0) * lg_ref[0, h])
            bwd = jnp.exp(jnp.where(diff < 0, -diff, 0.0) * lg_ref[1, h])
            dmat_ref[h] = jnp.where(diff >= 0, fwd, bwd)
        pos = lax.broadcasted_iota(jnp.int32, (c, MIX_W), 0).astype(F32)
        lf = per_head(lane_head, 0)
        lb = per_head(lane_head, 1)
        vec_ref[0] = jnp.exp((pos + 1.0) * lf)
        vec_ref[1] = jnp.exp((c - 1.0 - pos) * lf)
        vec_ref[2] = jnp.exp((c - pos) * lb)
        vec_ref[3] = jnp.exp(pos * lb)
        rdec_ref[0] = jnp.broadcast_to(jnp.exp(c * per_head(row_head, 0)), (MIX_W, MIX_W))
        rdec_ref[1] = jnp.broadcast_to(jnp.exp(c * per_head(row_head, 1)), (MIX_W, MIX_W))

    ones_bd = jnp.where(blockdiag, 1.0, 0.0).astype(BF16)

    def chunk(ref, n):
        return ref[0, pl.ds(pl.multiple_of(n * c, c), c), :]

    def fwd_body(n, carry):
        qn, kn, vn = chunk(q_ref, n), chunk(k_ref, n), chunk(v_ref, n)
        acc = _dot((qn.astype(F32) * vec_ref[0]).astype(BF16), st_ref[...].astype(BF16))
        for h in range(N_HEADS):
            hm = lane_head == h
            s = _dot_nt(jnp.where(hm, qn, jnp.zeros_like(qn)), kn)
            p = (s * dmat_ref[h]).astype(BF16)
            acc = acc + _dot(p, jnp.where(hm, vn, jnp.zeros_like(vn)))
        acc_ref[pl.ds(pl.multiple_of(n * c, c), c), :] = acc
        kv = _dot_tn((kn.astype(F32) * vec_ref[1]).astype(BF16), vn)
        st_ref[...] = st_ref[...] * rdec_ref[0] + jnp.where(blockdiag, kv, 0.0)
        return carry

    st_ref[...] = jnp.zeros_like(st_ref)
    lax.fori_loop(0, n_chunks, fwd_body, 0)

    def bwd_body(t, carry):
        n = n_chunks - 1 - t
        qn, kn, vn = chunk(q_ref, n), chunk(k_ref, n), chunk(v_ref, n)
        r0 = pl.multiple_of(n * c, c)
        o = acc_ref[pl.ds(r0, c), :] + _dot((qn.astype(F32) * vec_ref[2]).astype(BF16),
                                           st_ref[...].astype(BF16))
        o2 = o * o
        hi = o2.astype(BF16)
        lo = (o2 - hi.astype(F32)).astype(BF16)
        ms = (_dot(hi, ones_bd) + _dot(lo, ones_bd)) * (1.0 / HEAD_DIM)
        gate = chunk(g_ref, n).astype(F32)
        o_ref[0, pl.ds(r0, c), :] = (gate * _sigmoid(gate) * (o * lax.rsqrt(ms + EPS))).astype(BF16)
        kv = _dot_tn((kn.astype(F32) * vec_ref[3]).astype(BF16), vn)
        st_ref[...] = st_ref[...] * rdec_ref[1] + jnp.where(blockdiag, kv, 0.0)
        return carry

    st_ref[...] = jnp.zeros_like(st_ref)
    lax.fori_loop(0, n_chunks, bwd_body, 0)


def _retention(ret3d, lg):
    b, seq, _ = ret3d.shape
    spec = lambda col: pl.BlockSpec((1, seq, MIX_W), lambda i, col=col: (i, 0, col))
    return pl.pallas_call(
        _ret_kernel,
        out_shape=jax.ShapeDtypeStruct((b, seq, MIX_W), BF16),
        grid=(b,),
        in_specs=[pl.BlockSpec(memory_space=pltpu.SMEM), spec(0), spec(1), spec(2), spec(3)],
        out_specs=pl.BlockSpec((1, seq, MIX_W), lambda i: (i, 0, 0)),
        scratch_shapes=[
            pltpu.VMEM((seq, MIX_W), F32),
            pltpu.VMEM((MIX_W, MIX_W), F32),
            pltpu.VMEM((N_HEADS, RET_CHUNK, RET_CHUNK), F32),
            pltpu.VMEM((4, RET_CHUNK, MIX_W), F32),
            pltpu.VMEM((2, MIX_W, MIX_W), F32),
        ],
        compiler_params=_params("arbitrary"),
        name="retention",
    )(lg, ret3d, ret3d, ret3d, ret3d)


def _fourier_kernel(cs_ref, ss_ref, u_ref, cc_ref, sc_ref, wf_ref, o_ref, *, scale):
    u = u_ref[0]
    z1 = _dot(cs_ref[...], u).astype(BF16)
    z2 = _dot(ss_ref[...], u).astype(BF16)
    f = (_dot(z1, cc_ref[...]) - _dot(z2, sc_ref[...])) * scale
    o_ref[0] = _dot(f.astype(BF16), wf_ref[...]).astype(BF16)


def _fourier(fu3d, cs, ss, cc, sc, wf):
    b, seq, _ = fu3d.shape
    tr = FOUR_TR
    const = lambda i, j: (0, 0)
    return pl.pallas_call(
        functools.partial(_fourier_kernel, scale=1.0 / math.sqrt(seq * HEAD_DIM)),
        out_shape=jax.ShapeDtypeStruct((b, seq, MIX_W), BF16),
        grid=(seq // tr, b),
        in_specs=[
            pl.BlockSpec((tr, seq), lambda i, j: (i, 0)),
            pl.BlockSpec((tr, seq), lambda i, j: (i, 0)),
            pl.BlockSpec((1, seq, MIX_W), lambda i, j: (j, 0, 0)),
            pl.BlockSpec((MIX_W, MIX_W), const),
            pl.BlockSpec((MIX_W, MIX_W), const),
            pl.BlockSpec((MIX_W, MIX_W), const),
        ],
        out_specs=pl.BlockSpec((1, tr, MIX_W), lambda i, j: (j, i, 0)),
        compiler_params=_params("arbitrary", "arbitrary"),
        name="fourier",
    )(cs, ss, fu3d, cc, sc, wf)


def _dil_kernel(q_ref, k_ref, v_ref, o_ref, l_ref, *, sub_len, tq, win):
    j = pl.program_id(2)
    if win == sub_len:
        ws = 0
    else:
        ws = pl.multiple_of(jnp.clip(j * tq - DIL_RADIUS, 0, sub_len - win), DIL_RADIUS)
    q = q_ref[0]
    kw = k_ref[0, pl.ds(ws, win), :]
    vw = v_ref[0, pl.ds(ws, win), :]
    lane_head = lax.broadcasted_iota(jnp.int32, (1, MIX_W), 1) // HEAD_DIM
    zero = jnp.zeros_like(q)
    qs = jnp.concatenate([jnp.where(lane_head == h, q, zero) for h in range(N_HEADS)], axis=0)
    s = _dot_nt(qs, kw)
    qi = lax.broadcasted_iota(jnp.int32, (tq, win), 0) + (j * tq - ws)
    kj = lax.broadcasted_iota(jnp.int32, (tq, win), 1)
    valid = jnp.abs(qi - kj) <= DIL_RADIUS
    s = jnp.where(jnp.concatenate([valid] * N_HEADS, axis=0), s, NEG)
    m = jnp.max(s, axis=-1, keepdims=True)
    p = jnp.exp(s - m)
    den = jnp.sum(p, axis=-1, keepdims=True)
    r = _dot((p * (1.0 / den)).astype(BF16), vw)
    lse = m + jnp.log(den)
    o = jnp.zeros((tq, MIX_W), F32)
    lo = jnp.zeros((tq, MIX_W), F32)
    for h in range(N_HEADS):
        hm = lane_head == h
        o = jnp.where(hm, r[h * tq:(h + 1) * tq], o)
        lo = jnp.where(hm, lse[h * tq:(h + 1) * tq], lo)
    o_ref[0] = o.astype(BF16)
    l_ref[0] = lo


def _dilated(dil3d, group, dil):
    b, seq, width = dil3d.shape
    sub_len = seq // dil
    tq = min(DIL_TQ, sub_len)
    win = min(tq + 2 * DIL_RADIUS, sub_len)
    view = dil3d.reshape(b, sub_len, dil * width)
    nblk = width // MIX_W
    col = lambda part: (lambda i, r, j: (i, 0, r * nblk + group * 3 + part))
    o, lse = pl.pallas_call(
        functools.partial(_dil_kernel, sub_len=sub_len, tq=tq, win=win),
        out_shape=[jax.ShapeDtypeStruct((b, sub_len, dil * MIX_W), BF16),
                   jax.ShapeDtypeStruct((b, sub_len, dil * MIX_W), F32)],
        grid=(b, dil, sub_len // tq),
        in_specs=[
            pl.BlockSpec((1, tq, MIX_W), lambda i, r, j: (i, j, r * nblk + group * 3)),
            pl.BlockSpec((1, sub_len, MIX_W), col(1)),
            pl.BlockSpec((1, sub_len, MIX_W), col(2)),
        ],
        out_specs=[pl.BlockSpec((1, tq, MIX_W), lambda i, r, j: (i, j, r)),
                   pl.BlockSpec((1, tq, MIX_W), lambda i, r, j: (i, j, r))],
        compiler_params=_params("arbitrary", "arbitrary", "arbitrary"),
        name=f"dilated{group}",
    )(view, view, view)
    return o.reshape(b * seq, MIX_W), lse.reshape(b * seq, MIX_W)


def _mla_kernel(q_ref, k_ref, v_ref, o_ref):
    lane = lax.broadcasted_iota(jnp.int32, (1, LANES), 1)
    out = jnp.zeros(o_ref.shape[1:], F32)
    for hh in range(2):
        qh = q_ref[0, :, hh * LANES:(hh + 1) * LANES]
        kh = k_ref[0, :, hh * LANES:(hh + 1) * LANES]
        s = _dot_nt(qh, kh)
        m = jnp.max(s, axis=-1, keepdims=True)
        p = jnp.exp(s - m)
        den = jnp.sum(p, axis=-1, keepdims=True)
        o = _dot(p.astype(BF16), v_ref[0]) * (1.0 / den)
        out = jnp.where((lane // HEAD_DIM) == hh, o, out)
    o_ref[0] = out.astype(BF16)


def _mla(mq3d, mk3d, mv3d):
    b, seq, _ = mq3d.shape
    tq = MLA_TQ
    return pl.pallas_call(
        _mla_kernel,
        out_shape=jax.ShapeDtypeStruct((b, seq, MIX_W), BF16),
        grid=(b, 2, seq // tq),
        in_specs=[
            pl.BlockSpec((1, tq, 2 * LANES), lambda i, p, j: (i, j, p)),
            pl.BlockSpec((1, seq, 2 * LANES), lambda i, p, j: (i, 0, p)),
            pl.BlockSpec((1, seq, LANES), lambda i, p, j: (i, 0, p)),
        ],
        out_specs=pl.BlockSpec((1, tq, LANES), lambda i, p, j: (i, j, p)),
        compiler_params=_params("arbitrary", "arbitrary", "arbitrary"),
        name="mla",
    )(mq3d, mk3d, mv3d)


def _outproj_kernel(x_ref, mod_ref, ro_ref, fo_ref, d0_ref, d1_ref, d2_ref, l0_ref, l1_ref, l2_ref,
                    mo_ref, w_ref, gpm_ref, gpf_ref, x1_ref, h2_ref):
    l0, l1, l2 = l0_ref[...], l1_ref[...], l2_ref[...]
    m = jnp.maximum(l0, jnp.maximum(l1, l2))
    e0, e1, e2 = jnp.exp(l0 - m), jnp.exp(l1 - m), jnp.exp(l2 - m)
    od = (e0 * d0_ref[...].astype(F32) + e1 * d1_ref[...].astype(F32) + e2 * d2_ref[...].astype(F32))
    od = (od * (1.0 / (e0 + e1 + e2))).astype(BF16)
    y = (_dot(ro_ref[...], w_ref[0:256, :]) + _dot(fo_ref[...], w_ref[256:512, :])
         + _dot(od, w_ref[512:768, :]) + _dot(mo_ref[...], w_ref[768:1024, :]))
    g1 = mod_ref[0, 2:3, :]
    sh2 = mod_ref[0, 3:4, :]
    sc2 = mod_ref[0, 4:5, :]
    x1 = x_ref[...] + g1 * _rms(y, gpm_ref[...])
    x1_ref[...] = x1
    h2_ref[...] = (_rms(x1, gpf_ref[...]) * (1.0 + sc2) + sh2).astype(BF16)


def _outproj(x2d, mod, ro, fo, d_o, d_l, mo, w_out, g_post_mix, g_pre_ffn, seq):
    t = x2d.shape[0]
    tm = ROW_TILE
    tps = seq // tm
    row = lambda i: (i, 0)
    const = lambda i: (0, 0)
    mix = pl.BlockSpec((tm, MIX_W), row)
    return pl.pallas_call(
        _outproj_kernel,
        out_shape=[jax.ShapeDtypeStruct((t, D_MODEL), F32), jax.ShapeDtypeStruct((t, D_MODEL), BF16)],
        grid=(t // tm,),
        in_specs=[
            pl.BlockSpec((tm, D_MODEL), row),
            pl.BlockSpec((1, 6, D_MODEL), lambda i: (i // tps, 0, 0)),
            mix, mix, mix, mix, mix, mix, mix, mix, mix,
            pl.BlockSpec((D_MODEL, D_MODEL), const),
            pl.BlockSpec((1, D_MODEL), const),
            pl.BlockSpec((1, D_MODEL), const),
        ],
        out_specs=[pl.BlockSpec((tm, D_MODEL), row), pl.BlockSpec((tm, D_MODEL), row)],
        compiler_params=_params("arbitrary"),
        name="outproj",
    )(x2d, mod, ro, fo, d_o[0], d_o[1], d_o[2], d_l[0], d_l[1], d_l[2], mo, w_out, g_post_mix, g_pre_ffn)


def _ffn_kernel(hp_ref, hc_ref, hn_ref, x1_ref, mod_ref, wu_ref, cw_ref, cb_ref, wd_ref, g_ref, o_ref,
                *, tiles_per_seq):
    tm = hc_ref.shape[0]
    t = pl.program_id(0) % tiles_per_seq
    hp = jnp.where(t == 0, jnp.zeros_like(hp_ref[...]), hp_ref[...])
    hn = jnp.where(t == tiles_per_seq - 1, jnp.zeros_like(hn_ref[...]), hn_ref[...])
    he = jnp.concatenate([hp, hc_ref[...], hn], axis=0)

    def conv(c0):
        u = _dot(he, wu_ref[:, c0:c0 + FFN_CHUNK])
        w = cw_ref[:, c0:c0 + FFN_CHUNK]
        return (u[HALO - 1:HALO - 1 + tm] * w[0:1] + u[HALO:HALO + tm] * w[1:2]
                + u[HALO + 1:HALO + 1 + tm] * w[2:3] + cb_ref[:, c0:c0 + FFN_CHUNK])

    acc = jnp.zeros((tm, D_MODEL), F32)
    for c in range(D_FF // FFN_CHUNK):
        a = conv(c * FFN_CHUNK)
        bu = conv(D_FF + c * FFN_CHUNK)
        gate = (a * _sigmoid(a) * bu).astype(BF16)
        acc = acc + _dot(gate, wd_ref[c * FFN_CHUNK:(c + 1) * FFN_CHUNK, :])
    g2 = mod_ref[0, 5:6, :]
    o_ref[...] = x1_ref[...] + g2 * _rms(acc, g_ref[...])


def _ffn(h2, x1, mod, w_up, conv_w, conv_b, w_down, g_post_ffn, seq):
    t = x1.shape[0]
    tm = ROW_TILE
    tps = seq // tm
    hb = tm // HALO
    row = lambda i: (i, 0)
    const = lambda i: (0, 0)
    return pl.pallas_call(
        functools.partial(_ffn_kernel, tiles_per_seq=tps),
        out_shape=jax.ShapeDtypeStruct((t, D_MODEL), F32),
        grid=(t // tm,),
        in_specs=[
            pl.BlockSpec((HALO, D_MODEL), lambda i: (jnp.maximum(i * hb - 1, 0), 0)),
            pl.BlockSpec((tm, D_MODEL), row),
            pl.BlockSpec((HALO, D_MODEL), lambda i: (jnp.minimum((i + 1) * hb, t // HALO - 1), 0)),
            pl.BlockSpec((tm, D_MODEL), row),
            pl.BlockSpec((1, 6, D_MODEL), lambda i: (i // tps, 0, 0)),
            pl.BlockSpec((D_MODEL, 2 * D_FF), const),
            pl.BlockSpec((3, 2 * D_FF), const),
            pl.BlockSpec((1, 2 * D_FF), const),
            pl.BlockSpec((D_FF, D_MODEL), const),
            pl.BlockSpec((1, D_MODEL), const),
        ],
        out_specs=pl.BlockSpec((tm, D_MODEL), row),
        compiler_params=_params("arbitrary"),
        name="ffn",
    )(h2, h2, h2, x1, mod, w_up, conv_w, conv_b, w_down, g_post_ffn)


def _rope_tables(seq):
    pos = jnp.arange(seq, dtype=F32)[:, None]
    lane = np.arange(LANES)
    cols = []
    for theta, rot in ((RET_THETA, HEAD_DIM), (ROPE_THETA, PARTIAL_ROT), (ROPE_THETA, MLA_ROPE)):
        half = rot // 2
        inv = jnp.power(theta, -jnp.arange(half, dtype=F32) * 2.0 / rot)
        ang = pos * inv[lane % half][None, :]
        cols += [jnp.cos(ang), jnp.sin(ang)]
    return jnp.concatenate(cols, axis=1)


def _dft_tables(seq):
    n2 = 64
    n1 = seq // n2
    k = np.arange(seq)[:, None]
    a = 2.0 * np.pi * ((k * np.arange(n1)[None, :] * n2) % seq) / seq
    b = 2.0 * np.pi * ((k * np.arange(n2)[None, :]) % seq) / seq
    ca, sa = jnp.asarray(np.cos(a), F32)[:, :, None], jnp.asarray(np.sin(a), F32)[:, :, None]
    cb, sb = jnp.asarray(np.cos(b), F32)[:, None, :], jnp.asarray(np.sin(b), F32)[:, None, :]
    cs = (ca * cb - sa * sb).reshape(seq, seq).astype(BF16)
    ss = (sa * cb + ca * sb).reshape(seq, seq).astype(BF16)
    return cs, ss


def _block_diag(blocks):
    n = len(blocks)
    rows = [jnp.concatenate([blocks[i] if i == j else jnp.zeros_like(blocks[i]) for j in range(n)], axis=1)
            for i in range(n)]
    return jnp.concatenate(rows, axis=0)


def _perm_w_in(w_in):
    ret = w_in[:, 0:1280]
    dq, dk, dv = w_in[:, 1280:2048], w_in[:, 2048:2816], w_in[:, 2816:3584]
    groups = [jnp.concatenate([m[:, g * MIX_W:(g + 1) * MIX_W] for m in (dq, dk, dv)], axis=1)
              for g in range(N_DIL_GROUPS)]
    pad = jnp.zeros((D_MODEL, D_IN_PAD - w_in.shape[1]), w_in.dtype)
    return jnp.concatenate([ret] + groups + [w_in[:, 3584:], pad], axis=1).astype(BF16)


def _mla_weights(w_qb, w_kvb):
    qh = w_qb.reshape(Q_LORA, N_HEADS, MLA_NOPE + MLA_ROPE)
    wq = jnp.pad(qh, ((0, 0), (0, 0), (0, LANES - MLA_NOPE - MLA_ROPE))).reshape(Q_LORA, N_HEADS * LANES)
    kvh = w_kvb.reshape(KV_LORA, N_HEADS, MLA_NOPE + HEAD_DIM)
    wk = jnp.pad(kvh[:, :, :MLA_NOPE], ((0, 0), (0, 0), (0, LANES - MLA_NOPE))).reshape(KV_LORA, N_HEADS * LANES)
    wv = kvh[:, :, MLA_NOPE:].reshape(KV_LORA, MIX_W)
    place = np.zeros((LANES, N_HEADS * LANES), np.float32)
    for h in range(N_HEADS):
        for r in range(MLA_ROPE):
            place[r, h * LANES + MLA_NOPE + r] = 1.0
    return wq.astype(BF16), wk.astype(BF16), wv.astype(BF16), jnp.asarray(place, BF16)


def _trunk(x, mods, layers, shared):
    b, seq, _ = x.shape
    t = b * seq
    x2d = x.reshape(t, D_MODEL)
    tab = _rope_tables(seq)
    cs, ss = _dft_tables(seq)
    for mod, lw in zip(mods, layers):
        ret, fu, dil, mq, mk, mv = _inproj(x2d, mod, lw["g_pre_mix"], lw["w_in"], tab, lw["q_norm"],
                                           lw["kv_norm"], lw["wq"], lw["wk"], lw["wv"], shared["place"], seq)
        ro = _retention(ret.reshape(b, seq, 1024), lw["lg"]).reshape(t, MIX_W)
        fo = _fourier(fu.reshape(b, seq, MIX_W), cs, ss, shared["cc"], shared["sc"], lw["wf"]).reshape(t, MIX_W)
        dil3d = dil.reshape(b, seq, 3 * N_DIL_GROUPS * MIX_W)
        d_o, d_l = zip(*[_dilated(dil3d, g, d) for g, (_, d) in enumerate(DIL_PAIRS)])
        mo = _mla(mq.reshape(b, seq, 512), mk.reshape(b, seq, 512), mv.reshape(b, seq, MIX_W)).reshape(t, MIX_W)
        x1, h2 = _outproj(x2d, mod, ro, fo, d_o, d_l, mo, lw["w_out"], lw["g_post_mix"], lw["g_pre_ffn"], seq)
        x2d = _ffn(h2, x1, mod, lw["w_up"], lw["conv_w"], lw["conv_b"], lw["w_down"], lw["g_post_ffn"], seq)
    return x2d.reshape(b, seq, D_MODEL)


def kernel(x_prompt, x_sample, c_prompt, c_sample, w_ada, b_ada, norm_pre_mix, w_in, ret_decay_fwd,
           ret_decay_bwd, w_fmix, mla_q_norm, mla_w_qb, mla_kv_norm, mla_w_kvb, w_out, norm_post_mix,
           norm_pre_ffn, w_up, conv_w, conv_b, w_down, norm_post_ffn):
    depth = w_in.shape[0]
    nb_p, nb_s = c_prompt.shape[0], c_sample.shape[0]
    rows = -(-(nb_p + nb_s) // 8) * 8
    c_all = jnp.concatenate([c_prompt, c_sample, jnp.zeros((rows - nb_p - nb_s, D_MODEL), F32)], axis=0)
    mod_all = _ada(c_all, w_ada, b_ada)
    mods_p = [mod_all[l, :nb_p].reshape(nb_p, 6, D_MODEL) for l in range(depth)]
    mods_s = [mod_all[l, nb_p:nb_p + nb_s].reshape(nb_s, 6, D_MODEL) for l in range(depth)]

    c64 = 2.0 * np.pi * np.outer(np.arange(HEAD_DIM), np.arange(HEAD_DIM)) / HEAD_DIM
    shared = {
        "cc": _block_diag([jnp.asarray(np.cos(c64), BF16)] * N_HEADS),
        "sc": _block_diag([jnp.asarray(np.sin(c64), BF16)] * N_HEADS),
    }
    layers = []
    for l in range(depth):
        wq, wk, wv, place = _mla_weights(mla_w_qb[l], mla_w_kvb[l])
        shared["place"] = place
        layers.append({
            "g_pre_mix": norm_pre_mix[l][None, :],
            "w_in": _perm_w_in(w_in[l]),
            "lg": jnp.stack([jax.nn.log_sigmoid(ret_decay_fwd[l]), jax.nn.log_sigmoid(ret_decay_bwd[l])]),
            "wf": _block_diag([w_fmix[l, g] for g in range(N_HEADS)]).astype(BF16),
            "q_norm": mla_q_norm[l][None, :],
            "kv_norm": mla_kv_norm[l][None, :],
            "wq": wq, "wk": wk, "wv": wv,
            "w_out": w_out[l].astype(BF16),
            "g_post_mix": norm_post_mix[l][None, :],
            "g_pre_ffn": norm_pre_ffn[l][None, :],
            "w_up": w_up[l].astype(BF16),
            "conv_w": conv_w[l],
            "conv_b": conv_b[l][None, :],
            "w_down": w_down[l].astype(BF16),
            "g_post_ffn": norm_post_ffn[l][None, :],
        })
    y_prompt = _trunk(x_prompt, mods_p, layers, shared)
    y_sample = _trunk(x_sample, mods_s, layers, shared)
    return (y_prompt, y_sample)
```

```python
import functools
import math

import numpy as np
import jax
import jax.numpy as jnp
from jax import lax
from jax.experimental import pallas as pl
from jax.experimental.pallas import tpu as pltpu

F32 = jnp.float32
BF16 = jnp.bfloat16

D_MODEL = 1024
HEAD_DIM = 64
N_HEADS = 4
MIX_W = N_HEADS * HEAD_DIM
DIL_PAIRS = ((128, 1), (512, 4), (2048, 16))
N_DIL_GROUPS = 3
DIL_RADIUS = 64
MLA_NOPE = 64
MLA_ROPE = 32
Q_LORA = 256
KV_LORA = 128
D_FF = 2816
ROPE_THETA = 500000.0
RET_THETA = 10000.0
PARTIAL_ROT = HEAD_DIM // 4
EPS = 1e-6
NEG = -1e30

LANES = 128
D_IN_PAD = 4096
MLA_OFF = 3584
ROW_TILE = 512
RET_CHUNK = 256
DIL_TQ = 128
DIL_STEP_ROWS = 1024
FOUR_TR = 512
MLA_TQ = 256
FFN_CHUNK = 256
HALO = 16
VMEM_LIMIT = 56 * 1024 * 1024


def _params(*sem):
    return pltpu.CompilerParams(dimension_semantics=sem, vmem_limit_bytes=VMEM_LIMIT)


def _rms(x, g):
    return x * lax.rsqrt(jnp.mean(x * x, axis=-1, keepdims=True) + EPS) * g


def _sigmoid(x):
    return 1.0 / (1.0 + jnp.exp(-x))


def _dot(a, b):
    return jnp.dot(a, b, preferred_element_type=F32)


def _dot_nt(a, b):
    return lax.dot_general(a, b, (((1,), (1,)), ((), ())), preferred_element_type=F32)


def _dot_tn(a, b):
    return lax.dot_general(a, b, (((0,), (0,)), ((), ())), preferred_element_type=F32)


def _ada_kernel(c_ref, w_ref, b_ref, o_ref):
    c = c_ref[...]
    cond = (c * _sigmoid(c)).astype(BF16)
    o_ref[0] = _dot(cond, w_ref[0].astype(BF16)) + b_ref[0]


def _ada(c_all, w_ada, b_ada):
    depth, _, n = w_ada.shape
    rows = c_all.shape[0]
    tn = 1536
    return pl.pallas_call(
        _ada_kernel,
        out_shape=jax.ShapeDtypeStruct((depth, rows, n), F32),
        grid=(depth, n // tn),
        in_specs=[
            pl.BlockSpec((rows, D_MODEL), lambda l, j: (0, 0)),
            pl.BlockSpec((1, D_MODEL, tn), lambda l, j: (l, 0, j)),
            pl.BlockSpec((1, 1, tn), lambda l, j: (l, 0, j)),
        ],
        out_specs=pl.BlockSpec((1, rows, tn), lambda l, j: (l, 0, j)),
        compiler_params=_params("arbitrary", "arbitrary"),
        name="ada",
    )(c_all, w_ada, b_ada.reshape(depth, 1, n))


def _inproj_kernel(x_ref, mod_ref, g_ref, w_ref, tab_ref, qn_ref, kvn_ref, wq_ref, wk_ref, wv_ref,
                   pk_ref, ret_ref, fu_ref, d0_ref, d1_ref, d2_ref, mq_ref, mk_ref, mv_ref, scr_ref, scr2_ref):
    x = x_ref[...]
    sh = mod_ref[0, 0:1, :]
    sc = mod_ref[0, 1:2, :]
    hb = (_rms(x, g_ref[...]) * (1.0 + sc) + sh).astype(BF16)

    def mm(c0, c1):
        return _dot(hb, w_ref[:, c0:c1])

    lane = lax.broadcasted_iota(jnp.int32, (1, LANES), 1)
    j64 = lane & (HEAD_DIM - 1)

    def make_rope(cos, sin, lo_mask, hi_mask, half):
        c = jnp.where(lo_mask | hi_mask, cos, 1.0)
        sa = jnp.where(lo_mask, -sin, 0.0)
        sb = jnp.where(hi_mask, sin, 0.0)

        def apply(z):
            return z * c + pltpu.roll(z, LANES - half, 1) * sa + pltpu.roll(z, half, 1) * sb
        return apply

    rope_ret = make_rope(tab_ref[:, 0:128], tab_ref[:, 128:256], j64 < 32, j64 >= 32, 32)
    rope_dil = make_rope(tab_ref[:, 256:384], tab_ref[:, 384:512], j64 < 8, (j64 >= 8) & (j64 < 16), 8)
    cos_m = tab_ref[:, 512:640]
    sin_m = tab_ref[:, 640:768]
    rope_kr = make_rope(cos_m, sin_m, lane < 16, (lane >= 16) & (lane < 32), 16)
    rope_mq = make_rope(cos_m, sin_m, (lane >= 64) & (lane < 80), (lane >= 80) & (lane < 96), 16)

    z = mm(0, 512)
    for c in range(4):
        r = rope_ret(z[:, c * LANES:(c + 1) * LANES])
        if c >= 2:
            r = r * (HEAD_DIM ** -0.5)
        ret_ref[:, c * LANES:(c + 1) * LANES] = r.astype(BF16)
    ret_ref[:, 512:1024] = mm(512, 1024).astype(BF16)
    fu_ref[...] = mm(1024, 1280).astype(BF16)
    tm = x.shape[0]
    nslab = 3 * MIX_W // LANES
    for g, d_ref in enumerate((d0_ref, d1_ref, d2_ref)):
        dil = DIL_PAIRS[g][1]
        base = g * 3 * MIX_W
        z = mm(1280 + base, 1280 + base + 3 * MIX_W)
        slabs = []
        for c in range(nslab):
            r = z[:, c * LANES:(c + 1) * LANES]
            if c < 4:
                r = rope_dil(r)
            if c < 2:
                r = r * (HEAD_DIM ** -0.5)
            slabs.append(r)
        if dil == 1:
            d_ref[0, 0] = jnp.concatenate(slabs, axis=1).astype(BF16)
            continue
        for c in range(nslab):
            scr_ref[c] = slabs[c]
        n4 = tm // 4

        def rows4(ref, start, count):
            return jnp.concatenate([ref[c, pl.ds(start, count, stride=4), :] for c in range(nslab)], axis=1)

        if dil == 4:
            for r4 in range(4):
                d_ref[0, r4] = rows4(scr_ref, r4, n4).astype(BF16)
        else:
            for r4 in range(4):
                for c in range(nslab):
                    scr2_ref[c, r4 * n4:(r4 + 1) * n4, :] = scr_ref[c, pl.ds(r4, n4, stride=4), :]
            for r4 in range(4):
                for q4 in range(4):
                    d_ref[0, r4 + 4 * q4] = rows4(scr2_ref, r4 * n4 + q4, tm // 16).astype(BF16)
    z = mm(MLA_OFF, D_IN_PAD)
    cqn = _rms(z[:, 0:Q_LORA], qn_ref[...]).astype(BF16)
    q = _dot(cqn, wq_ref[...])
    scale = (MLA_NOPE + MLA_ROPE) ** -0.5
    for h in range(N_HEADS):
        r = rope_mq(q[:, h * LANES:(h + 1) * LANES]) * scale
        mq_ref[:, h * LANES:(h + 1) * LANES] = r.astype(BF16)
    ckvn = _rms(z[:, Q_LORA:Q_LORA + KV_LORA], kvn_ref[...]).astype(BF16)
    kr = rope_kr(z[:, 384:512]).astype(BF16)
    mk_ref[...] = (_dot(ckvn, wk_ref[...]) + _dot(kr, pk_ref[...])).astype(BF16)
    mv_ref[...] = _dot(ckvn, wv_ref[...]).astype(BF16)


def _inproj(x2d, mod, g_pre, w_in_p, tab, qn, kvn, wq_p, wk_p, wv_p, pk, seq):
    t = x2d.shape[0]
    b = t // seq
    tm = ROW_TILE
    tps = seq // tm
    const = lambda i: (0, 0)
    row = lambda i: (i, 0)
    flat = [(t, 1024), (t, MIX_W)]
    flat2 = [(t, 512), (t, 512), (t, MIX_W)]
    dils = [d for _, d in DIL_PAIRS]
    assert dils == [1, 4, 16]
    bf = lambda shp: jax.ShapeDtypeStruct(shp, BF16)
    dil_shapes = [bf((b, d, seq // d, 3 * MIX_W)) for d in dils]
    dil_specs = [pl.BlockSpec((1, d, tm // d, 3 * MIX_W), lambda i: (i // tps, 0, i % tps, 0)) for d in dils]
    return pl.pallas_call(
        _inproj_kernel,
        out_shape=[bf(s) for s in flat] + dil_shapes + [bf(s) for s in flat2],
        grid=(t // tm,),
        in_specs=[
            pl.BlockSpec((tm, D_MODEL), row),
            pl.BlockSpec((1, 6, D_MODEL), lambda i: (i // tps, 0, 0)),
            pl.BlockSpec((1, D_MODEL), const),
            pl.BlockSpec((D_MODEL, D_IN_PAD), const),
            pl.BlockSpec((tm, 6 * LANES), lambda i: (i % tps, 0)),
            pl.BlockSpec((1, Q_LORA), const),
            pl.BlockSpec((1, KV_LORA), const),
            pl.BlockSpec((Q_LORA, 512), const),
            pl.BlockSpec((KV_LORA, 512), const),
            pl.BlockSpec((KV_LORA, MIX_W), const),
            pl.BlockSpec((LANES, 512), const),
        ],
        out_specs=[pl.BlockSpec((tm, s[1]), row) for s in flat] + dil_specs
                  + [pl.BlockSpec((tm, s[1]), row) for s in flat2],
        scratch_shapes=[pltpu.VMEM((3 * MIX_W // LANES, tm, LANES), F32),
                        pltpu.VMEM((3 * MIX_W // LANES, tm, LANES), F32)],
        compiler_params=_params("arbitrary"),
        name="inproj",
    )(x2d, mod, g_pre, w_in_p, tab, qn, kvn, wq_p, wk_p, wv_p, pk)


def _ret_kernel(lg_ref, q_ref, k_ref, v_ref, g_ref, o_ref, acc_ref, st_ref, dmat_ref, vec_ref, rdec_ref):
    c = RET_CHUNK
    seq = q_ref.shape[1]
    n_chunks = seq // c
    lane_head = lax.broadcasted_iota(jnp.int32, (1, MIX_W), 1) // HEAD_DIM
    row_head = lax.broadcasted_iota(jnp.int32, (MIX_W, 1), 0) // HEAD_DIM
    blockdiag = row_head == lane_head

    def per_head(idx, d):
        out = lg_ref[d, 0]
        for h in range(1, N_HEADS):
            out = jnp.where(idx == h, lg_ref[d, h], out)
        return out

    @pl.when(pl.program_id(0) == 0)
    def _tables():
        ri = lax.broadcasted_iota(jnp.int32, (c, c), 0)
        ci = lax.broadcasted_iota(jnp.int32, (c, c), 1)
        diff = (ri - ci).astype(F32)
        for h in range(N_HEADS):
            fwd = jnp.exp(jnp.where(diff >= 0, diff, 0.0) * lg_ref[0, h])
            bwd = jnp.exp(jnp.where(diff < 0, -diff, 0.0) * lg_ref[1, h])
            dmat_ref[h] = jnp.where(diff >= 0, fwd, bwd)
        pos = lax.broadcasted_iota(jnp.int32, (c, MIX_W), 0).astype(F32)
        lf = per_head(lane_head, 0)
        lb = per_head(lane_head, 1)
        vec_ref[0] = jnp.exp((pos + 1.0) * lf)
        vec_ref[1] = jnp.exp((c - 1.0 - pos) * lf)
        vec_ref[2] = jnp.exp((c - pos) * lb)
        vec_ref[3] = jnp.exp(pos * lb)
        rdec_ref[0] = jnp.broadcast_to(jnp.exp(c * per_head(row_head, 0)), (MIX_W, MIX_W))
        rdec_ref[1] = jnp.broadcast_to(jnp.exp(c * per_head(row_head, 1)), (MIX_W, MIX_W))

    ones_bd = jnp.where(blockdiag, 1.0, 0.0).astype(BF16)

    def chunk(ref, n):
        return ref[0, pl.ds(pl.multiple_of(n * c, c), c), :]

    def fwd_body(n, carry):
        qn, kn, vn = chunk(q_ref, n), chunk(k_ref, n), chunk(v_ref, n)
        acc = _dot((qn.astype(F32) * vec_ref[0]).astype(BF16), st_ref[...].astype(BF16))
        for h in range(N_HEADS):
            hm = lane_head == h
            s = _dot_nt(jnp.where(hm, qn, jnp.zeros_like(qn)), kn)
            p = (s * dmat_ref[h]).astype(BF16)
            acc = acc + _dot(p, jnp.where(hm, vn, jnp.zeros_like(vn)))
        acc_ref[pl.ds(pl.multiple_of(n * c, c), c), :] = acc
        kv = _dot_tn((kn.astype(F32) * vec_ref[1]).astype(BF16), vn)
        st_ref[...] = st_ref[...] * rdec_ref[0] + jnp.where(blockdiag, kv, 0.0)
        return carry

    st_ref[...] = jnp.zeros_like(st_ref)
    lax.fori_loop(0, n_chunks, fwd_body, 0)

    def bwd_body(t, carry):
        n = n_chunks - 1 - t
        qn, kn, vn = chunk(q_ref, n), chunk(k_ref, n), chunk(v_ref, n)
        r0 = pl.multiple_of(n * c, c)
        o = acc_ref[pl.ds(r0, c), :] + _dot((qn.astype(F32) * vec_ref[2]).astype(BF16),
                                           st_ref[...].astype(BF16))
        o2 = o * o
        hi = o2.astype(BF16)
        lo = (o2 - hi.astype(F32)).astype(BF16)
        ms = (_dot(hi, ones_bd) + _dot(lo, ones_bd)) * (1.0 / HEAD_DIM)
        gate = chunk(g_ref, n).astype(F32)
        o_ref[0, pl.ds(r0, c), :] = (gate * _sigmoid(gate) * (o * lax.rsqrt(ms + EPS))).astype(BF16)
        kv = _dot_tn((kn.astype(F32) * vec_ref[3]).astype(BF16), vn)
        st_ref[...] = st_ref[...] * rdec_ref[1] + jnp.where(blockdiag, kv, 0.0)
        return carry

    st_ref[...] = jnp.zeros_like(st_ref)
    lax.fori_loop(0, n_chunks, bwd_body, 0)


def _retention(ret3d, lg):
    b, seq, _ = ret3d.shape
    spec = lambda col: pl.BlockSpec((1, seq, MIX_W), lambda i, col=col: (i, 0, col))
    return pl.pallas_call(
        _ret_kernel,
        out_shape=jax.ShapeDtypeStruct((b, seq, MIX_W), BF16),
        grid=(b,),
        in_specs=[pl.BlockSpec(memory_space=pltpu.SMEM), spec(0), spec(1), spec(2), spec(3)],
        out_specs=pl.BlockSpec((1, seq, MIX_W), lambda i: (i, 0, 0)),
        scratch_shapes=[
            pltpu.VMEM((seq, MIX_W), F32),
            pltpu.VMEM((MIX_W, MIX_W), F32),
            pltpu.VMEM((N_HEADS, RET_CHUNK, RET_CHUNK), F32),
            pltpu.VMEM((4, RET_CHUNK, MIX_W), F32),
            pltpu.VMEM((2, MIX_W, MIX_W), F32),
        ],
        compiler_params=_params("arbitrary"),
        name="retention",
    )(lg, ret3d, ret3d, ret3d, ret3d)


def _fourier_kernel(cs_ref, ss_ref, u_ref, cc_ref, sc_ref, wf_ref, o_ref, *, scale):
    u = u_ref[0]
    z1 = _dot(cs_ref[...], u).astype(BF16)
    z2 = _dot(ss_ref[...], u).astype(BF16)
    f = (_dot(z1, cc_ref[...]) - _dot(z2, sc_ref[...])) * scale
    o_ref[0] = _dot(f.astype(BF16), wf_ref[...]).astype(BF16)


def _fourier(fu3d, cs, ss, cc, sc, wf):
    b, seq, _ = fu3d.shape
    tr = FOUR_TR
    const = lambda i, j: (0, 0)
    return pl.pallas_call(
        functools.partial(_fourier_kernel, scale=1.0 / math.sqrt(seq * HEAD_DIM)),
        out_shape=jax.ShapeDtypeStruct((b, seq, MIX_W), BF16),
        grid=(seq // tr, b),
        in_specs=[
            pl.BlockSpec((tr, seq), lambda i, j: (i, 0)),
            pl.BlockSpec((tr, seq), lambda i, j: (i, 0)),
            pl.BlockSpec((1, seq, MIX_W), lambda i, j: (j, 0, 0)),
            pl.BlockSpec((MIX_W, MIX_W), const),
            pl.BlockSpec((MIX_W, MIX_W), const),
            pl.BlockSpec((MIX_W, MIX_W), const),
        ],
        out_specs=pl.BlockSpec((1, tr, MIX_W), lambda i, j: (j, i, 0)),
        compiler_params=_params("arbitrary", "arbitrary"),
        name="fourier",
    )(cs, ss, fu3d, cc, sc, wf)


def _dil_kernel(q_ref, k_ref, v_ref, o_ref, l_ref, *, sub_len, ts, rb, tq, win):
    j = pl.program_id(2)
    lane_head = lax.broadcasted_iota(jnp.int32, (1, MIX_W), 1) // HEAD_DIM
    lse_head = lax.broadcasted_iota(jnp.int32, (1, LANES), 1) // (LANES // N_HEADS)
    kj = lax.broadcasted_iota(jnp.int32, (tq, win), 1)
    qi = lax.broadcasted_iota(jnp.int32, (tq, win), 0)
    for rr in range(rb):
        for blk in range(ts // tq):
            q0 = j * ts + blk * tq
            if win == sub_len:
                ws = 0
            else:
                ws = pl.multiple_of(jnp.clip(q0 - DIL_RADIUS, 0, sub_len - win), DIL_RADIUS)
            q = q_ref[0, rr, blk * tq:(blk + 1) * tq, :]
            kw = k_ref[0, rr, pl.ds(ws, win), :]
            vw = v_ref[0, rr, pl.ds(ws, win), :]
            zero = jnp.zeros_like(q)
            qs = jnp.concatenate([jnp.where(lane_head == h, q, zero) for h in range(N_HEADS)], axis=0)
            s = _dot_nt(qs, kw)
            valid = jnp.abs(qi + (q0 - ws) - kj) <= DIL_RADIUS
            s = jnp.where(jnp.concatenate([valid] * N_HEADS, axis=0), s, NEG)
            m = jnp.max(s, axis=-1, keepdims=True)
            p = jnp.exp(s - m)
            den = jnp.sum(p, axis=-1, keepdims=True)
            r = _dot((p * (1.0 / den)).astype(BF16), vw)
            lse = m + jnp.log(den)
            o = jnp.zeros((tq, MIX_W), F32)
            lo = jnp.zeros((tq, LANES), F32)
            for h in range(N_HEADS):
                o = jnp.where(lane_head == h, r[h * tq:(h + 1) * tq], o)
                lo = jnp.where(lse_head == h, lse[h * tq:(h + 1) * tq], lo)
            o_ref[0, rr, blk * tq:(blk + 1) * tq, :] = o.astype(BF16)
            l_ref[0, rr, blk * tq:(blk + 1) * tq, :] = lo


def _dilated(dg, group):
    b, dil, sub_len, _ = dg.shape
    tq = min(DIL_TQ, sub_len)
    win = min(tq + 2 * DIL_RADIUS, sub_len)
    ts = min(sub_len, DIL_STEP_ROWS)
    rb = min(dil, DIL_STEP_ROWS // ts)
    part = lambda c: (lambda i, r, j: (i, r, 0, c))
    return pl.pallas_call(
        functools.partial(_dil_kernel, sub_len=sub_len, ts=ts, rb=rb, tq=tq, win=win),
        out_shape=[jax.ShapeDtypeStruct((b, dil, sub_len, MIX_W), BF16),
                   jax.ShapeDtypeStruct((b, dil, sub_len, LANES), F32)],
        grid=(b, dil // rb, sub_len // ts),
        in_specs=[
            pl.BlockSpec((1, rb, ts, MIX_W), lambda i, r, j: (i, r, j, 0)),
            pl.BlockSpec((1, rb, sub_len, MIX_W), part(1)),
            pl.BlockSpec((1, rb, sub_len, MIX_W), part(2)),
        ],
        out_specs=[pl.BlockSpec((1, rb, ts, MIX_W), lambda i, r, j: (i, r, j, 0)),
                   pl.BlockSpec((1, rb, ts, LANES), lambda i, r, j: (i, r, j, 0))],
        compiler_params=_params("arbitrary", "arbitrary", "arbitrary"),
        name=f"dilated{group}",
    )(dg, dg, dg)


def _mla_kernel(q_ref, k_ref, v_ref, o_ref):
    lane = lax.broadcasted_iota(jnp.int32, (1, LANES), 1)
    out = jnp.zeros(o_ref.shape[1:], F32)
    for hh in range(2):
        qh = q_ref[0, :, hh * LANES:(hh + 1) * LANES]
        kh = k_ref[0, :, hh * LANES:(hh + 1) * LANES]
        s = _dot_nt(qh, kh)
        m = jnp.max(s, axis=-1, keepdims=True)
        p = jnp.exp(s - m)
        den = jnp.sum(p, axis=-1, keepdims=True)
        o = _dot(p.astype(BF16), v_ref[0]) * (1.0 / den)
        out = jnp.where((lane // HEAD_DIM) == hh, o, out)
    o_ref[0] = out.astype(BF16)


def _mla(mq3d, mk3d, mv3d):
    b, seq, _ = mq3d.shape
    tq = MLA_TQ
    return pl.pallas_call(
        _mla_kernel,
        out_shape=jax.ShapeDtypeStruct((b, seq, MIX_W), BF16),
        grid=(b, 2, seq // tq),
        in_specs=[
            pl.BlockSpec((1, tq, 2 * LANES), lambda i, p, j: (i, j, p)),
            pl.BlockSpec((1, seq, 2 * LANES), lambda i, p, j: (i, 0, p)),
            pl.BlockSpec((1, seq, LANES), lambda i, p, j: (i, 0, p)),
        ],
        out_specs=pl.BlockSpec((1, tq, LANES), lambda i, p, j: (i, j, p)),
        compiler_params=_params("arbitrary", "arbitrary", "arbitrary"),
        name="mla",
    )(mq3d, mk3d, mv3d)


def _outproj_kernel(x_ref, mod_ref, ro_ref, fo_ref, d0_ref, d1_ref, d2_ref, l0_ref, l1_ref, l2_ref,
                    mo_ref, w_ref, gpm_ref, gpf_ref, x1_ref, h2_ref, so1_ref, sl1_ref, so2_ref, sl2_ref):
    tm = x_ref.shape[0]

    def natural_order(o_ref, l_ref, so_ref, sl_ref):
        dil = o_ref.shape[1]
        if dil == 1:
            return o_ref[0, 0].astype(F32), l_ref[0, 0]
        n = tm // dil
        for r in range(dil):
            o = o_ref[0, r].astype(F32)
            for c in range(MIX_W // LANES):
                so_ref[c, pl.ds(r, n, stride=dil), :] = o[:, c * LANES:(c + 1) * LANES]
            sl_ref[pl.ds(r, n, stride=dil), :] = l_ref[0, r]
        return jnp.concatenate([so_ref[c] for c in range(MIX_W // LANES)], axis=1), sl_ref[...]

    o0, l0 = natural_order(d0_ref, l0_ref, None, None)
    o1, l1 = natural_order(d1_ref, l1_ref, so1_ref, sl1_ref)
    o2, l2 = natural_order(d2_ref, l2_ref, so2_ref, sl2_ref)
    m = jnp.maximum(l0, jnp.maximum(l1, l2))
    e0, e1, e2 = jnp.exp(l0 - m), jnp.exp(l1 - m), jnp.exp(l2 - m)
    inv = 1.0 / (e0 + e1 + e2)
    src = lax.broadcasted_iota(jnp.int32, (LANES, MIX_W), 0)
    dst = lax.broadcasted_iota(jnp.int32, (LANES, MIX_W), 1)
    spread = jnp.where(src == (dst // HEAD_DIM) * (LANES // N_HEADS), 1.0, 0.0).astype(BF16)
    od = (_dot((e0 * inv).astype(BF16), spread) * o0 + _dot((e1 * inv).astype(BF16), spread) * o1
          + _dot((e2 * inv).astype(BF16), spread) * o2).astype(BF16)
    y = (_dot(ro_ref[...], w_ref[0:256, :]) + _dot(fo_ref[...], w_ref[256:512, :])
         + _dot(od, w_ref[512:768, :]) + _dot(mo_ref[...], w_ref[768:1024, :]))
    g1 = mod_ref[0, 2:3, :]
    sh2 = mod_ref[0, 3:4, :]
    sc2 = mod_ref[0, 4:5, :]
    x1 = x_ref[...] + g1 * _rms(y, gpm_ref[...])
    x1_ref[...] = x1
    h2_ref[...] = (_rms(x1, gpf_ref[...]) * (1.0 + sc2) + sh2).astype(BF16)


def _outproj(x2d, mod, ro, fo, d_o, d_l, mo, w_out, g_post_mix, g_pre_ffn, seq):
    t = x2d.shape[0]
    tm = ROW_TILE
    tps = seq // tm
    row = lambda i: (i, 0)
    const = lambda i: (0, 0)
    mix = pl.BlockSpec((tm, MIX_W), row)
    res = lambda a: pl.BlockSpec((1, a.shape[1], tm // a.shape[1], a.shape[3]), lambda i: (i // tps, 0, i % tps, 0))
    return pl.pallas_call(
        _outproj_kernel,
        out_shape=[jax.ShapeDtypeStruct((t, D_MODEL), F32), jax.ShapeDtypeStruct((t, D_MODEL), BF16)],
        grid=(t // tm,),
        in_specs=[
            pl.BlockSpec((tm, D_MODEL), row),
            pl.BlockSpec((1, 6, D_MODEL), lambda i: (i // tps, 0, 0)),
            mix, mix, res(d_o[0]), res(d_o[1]), res(d_o[2]), res(d_l[0]), res(d_l[1]), res(d_l[2]), mix,
            pl.BlockSpec((D_MODEL, D_MODEL), const),
            pl.BlockSpec((1, D_MODEL), const),
            pl.BlockSpec((1, D_MODEL), const),
        ],
        out_specs=[pl.BlockSpec((tm, D_MODEL), row), pl.BlockSpec((tm, D_MODEL), row)],
        scratch_shapes=[pltpu.VMEM((MIX_W // LANES, tm, LANES), F32), pltpu.VMEM((tm, LANES), F32),
                        pltpu.VMEM((MIX_W // LANES, tm, LANES), F32), pltpu.VMEM((tm, LANES), F32)],
        compiler_params=_params("arbitrary"),
        name="outproj",
    )(x2d, mod, ro, fo, d_o[0], d_o[1], d_o[2], d_l[0], d_l[1], d_l[2], mo, w_out, g_post_mix, g_pre_ffn)


def _ffn_kernel(hp_ref, hc_ref, hn_ref, x1_ref, mod_ref, wu_ref, cw_ref, cb_ref, wd_ref, g_ref, o_ref,
                gate_ref, *, tiles_per_seq):
    tm = hc_ref.shape[0]
    t = pl.program_id(0) % tiles_per_seq
    hp = jnp.where(t == 0, jnp.zeros_like(hp_ref[...]), hp_ref[...])
    hn = jnp.where(t == tiles_per_seq - 1, jnp.zeros_like(hn_ref[...]), hn_ref[...])
    he = jnp.concatenate([hp, hc_ref[...], hn], axis=0)

    ext = tm + 2 * HALO

    def conv(c0):
        u = _dot(he, wu_ref[:, c0:c0 + FFN_CHUNK])
        w = cw_ref[:, c0:c0 + FFN_CHUNK]
        prev = pltpu.roll(u, 1, 0)[HALO:HALO + tm]
        nxt = pltpu.roll(u, ext - 1, 0)[HALO:HALO + tm]
        return prev * w[0:1] + u[HALO:HALO + tm] * w[1:2] + nxt * w[2:3] + cb_ref[:, c0:c0 + FFN_CHUNK]

    for c in range(D_FF // FFN_CHUNK):
        a = conv(c * FFN_CHUNK)
        bu = conv(D_FF + c * FFN_CHUNK)
        gate_ref[:, c * FFN_CHUNK:(c + 1) * FFN_CHUNK] = (a * _sigmoid(a) * bu).astype(BF16)
    acc = _dot(gate_ref[...], wd_ref[...])
    g2 = mod_ref[0, 5:6, :]
    o_ref[...] = x1_ref[...] + g2 * _rms(acc, g_ref[...])


def _ffn(h2, x1, mod, w_up, conv_w, conv_b, w_down, g_post_ffn, seq):
    t = x1.shape[0]
    tm = ROW_TILE
    tps = seq // tm
    hb = tm // HALO
    row = lambda i: (i, 0)
    const = lambda i: (0, 0)
    return pl.pallas_call(
        functools.partial(_ffn_kernel, tiles_per_seq=tps),
        out_shape=jax.ShapeDtypeStruct((t, D_MODEL), F32),
        grid=(t // tm,),
        in_specs=[
            pl.BlockSpec((HALO, D_MODEL), lambda i: (jnp.maximum(i * hb - 1, 0), 0)),
            pl.BlockSpec((tm, D_MODEL), row),
            pl.BlockSpec((HALO, D_MODEL), lambda i: (jnp.minimum((i + 1) * hb, t // HALO - 1), 0)),
            pl.BlockSpec((tm, D_MODEL), row),
            pl.BlockSpec((1, 6, D_MODEL), lambda i: (i // tps, 0, 0)),
            pl.BlockSpec((D_MODEL, 2 * D_FF), const),
            pl.BlockSpec((3, 2 * D_FF), const),
            pl.BlockSpec((1, 2 * D_FF), const),
            pl.BlockSpec((D_FF, D_MODEL), const),
            pl.BlockSpec((1, D_MODEL), const),
        ],
        out_specs=pl.BlockSpec((tm, D_MODEL), row),
        scratch_shapes=[pltpu.VMEM((tm, D_FF), BF16)],
        compiler_params=_params("arbitrary"),
        name="ffn",
    )(h2, h2, h2, x1, mod, w_up, conv_w, conv_b, w_down, g_post_ffn)


def _rope_tables(seq):
    pos = jnp.arange(seq, dtype=F32)[:, None]
    lane = np.arange(LANES)
    cols = []
    for theta, rot in ((RET_THETA, HEAD_DIM), (ROPE_THETA, PARTIAL_ROT), (ROPE_THETA, MLA_ROPE)):
        half = rot // 2
        inv = jnp.power(theta, -jnp.arange(half, dtype=F32) * 2.0 / rot)
        ang = pos * inv[lane % half][None, :]
        cols += [jnp.cos(ang), jnp.sin(ang)]
    return jnp.concatenate(cols, axis=1)


def _dft_tables(seq):
    n2 = 64
    n1 = seq // n2
    k = np.arange(seq)[:, None]
    a = 2.0 * np.pi * ((k * np.arange(n1)[None, :] * n2) % seq) / seq
    b = 2.0 * np.pi * ((k * np.arange(n2)[None, :]) % seq) / seq
    ca, sa = jnp.asarray(np.cos(a), F32)[:, :, None], jnp.asarray(np.sin(a), F32)[:, :, None]
    cb, sb = jnp.asarray(np.cos(b), F32)[:, None, :], jnp.asarray(np.sin(b), F32)[:, None, :]
    cs = (ca * cb - sa * sb).reshape(seq, seq).astype(BF16)
    ss = (sa * cb + ca * sb).reshape(seq, seq).astype(BF16)
    return cs, ss


def _block_diag(blocks):
    n = len(blocks)
    rows = [jnp.concatenate([blocks[i] if i == j else jnp.zeros_like(blocks[i]) for j in range(n)], axis=1)
            for i in range(n)]
    return jnp.concatenate(rows, axis=0)


def _perm_w_in(w_in):
    ret = w_in[:, 0:1280]
    dq, dk, dv = w_in[:, 1280:2048], w_in[:, 2048:2816], w_in[:, 2816:3584]
    groups = [jnp.concatenate([m[:, g * MIX_W:(g + 1) * MIX_W] for m in (dq, dk, dv)], axis=1)
              for g in range(N_DIL_GROUPS)]
    pad = jnp.zeros((D_MODEL, D_IN_PAD - w_in.shape[1]), w_in.dtype)
    return jnp.concatenate([ret] + groups + [w_in[:, 3584:], pad], axis=1).astype(BF16)


def _mla_weights(w_qb, w_kvb):
    qh = w_qb.reshape(Q_LORA, N_HEADS, MLA_NOPE + MLA_ROPE)
    wq = jnp.pad(qh, ((0, 0), (0, 0), (0, LANES - MLA_NOPE - MLA_ROPE))).reshape(Q_LORA, N_HEADS * LANES)
    kvh = w_kvb.reshape(KV_LORA, N_HEADS, MLA_NOPE + HEAD_DIM)
    wk = jnp.pad(kvh[:, :, :MLA_NOPE], ((0, 0), (0, 0), (0, LANES - MLA_NOPE))).reshape(KV_LORA, N_HEADS * LANES)
    wv = kvh[:, :, MLA_NOPE:].reshape(KV_LORA, MIX_W)
    place = np.zeros((LANES, N_HEADS * LANES), np.float32)
    for h in range(N_HEADS):
        for r in range(MLA_ROPE):
            place[r, h * LANES + MLA_NOPE + r] = 1.0
    return wq.astype(BF16), wk.astype(BF16), wv.astype(BF16), jnp.asarray(place, BF16)


def _trunk(x, mods, layers, shared):
    b, seq, _ = x.shape
    t = b * seq
    x2d = x.reshape(t, D_MODEL)
    tab = _rope_tables(seq)
    cs, ss = _dft_tables(seq)
    for mod, lw in zip(mods, layers):
        ret, fu, dg0, dg1, dg2, mq, mk, mv = _inproj(x2d, mod, lw["g_pre_mix"], lw["w_in"], tab, lw["q_norm"],
                                                     lw["kv_norm"], lw["wq"], lw["wk"], lw["wv"],
                                                     shared["place"], seq)
        ro = _retention(ret.reshape(b, seq, 1024), lw["lg"]).reshape(t, MIX_W)
        fo = _fourier(fu.reshape(b, seq, MIX_W), cs, ss, shared["cc"], shared["sc"], lw["wf"]).reshape(t, MIX_W)
        d_o, d_l = zip(*[_dilated(dg, g) for g, dg in enumerate((dg0, dg1, dg2))])
        mo = _mla(mq.reshape(b, seq, 512), mk.reshape(b, seq, 512), mv.reshape(b, seq, MIX_W)).reshape(t, MIX_W)
        x1, h2 = _outproj(x2d, mod, ro, fo, d_o, d_l, mo, lw["w_out"], lw["g_post_mix"], lw["g_pre_ffn"], seq)
        x2d = _ffn(h2, x1, mod, lw["w_up"], lw["conv_w"], lw["conv_b"], lw["w_down"], lw["g_post_ffn"], seq)
    return x2d.reshape(b, seq, D_MODEL)


def kernel(x_prompt, x_sample, c_prompt, c_sample, w_ada, b_ada, norm_pre_mix, w_in, ret_decay_fwd,
           ret_decay_bwd, w_fmix, mla_q_norm, mla_w_qb, mla_kv_norm, mla_w_kvb, w_out, norm_post_mix,
           norm_pre_ffn, w_up, conv_w, conv_b, w_down, norm_post_ffn):
    depth = w_in.shape[0]
    nb_p, nb_s = c_prompt.shape[0], c_sample.shape[0]
    rows = -(-(nb_p + nb_s) // 8) * 8
    c_all = jnp.concatenate([c_prompt, c_sample, jnp.zeros((rows - nb_p - nb_s, D_MODEL), F32)], axis=0)
    mod_all = _ada(c_all, w_ada, b_ada)
    mods_p = [mod_all[l, :nb_p].reshape(nb_p, 6, D_MODEL) for l in range(depth)]
    mods_s = [mod_all[l, nb_p:nb_p + nb_s].reshape(nb_s, 6, D_MODEL) for l in range(depth)]

    c64 = 2.0 * np.pi * np.outer(np.arange(HEAD_DIM), np.arange(HEAD_DIM)) / HEAD_DIM
    shared = {
        "cc": _block_diag([jnp.asarray(np.cos(c64), BF16)] * N_HEADS),
        "sc": _block_diag([jnp.asarray(np.sin(c64), BF16)] * N_HEADS),
    }
    layers = []
    for l in range(depth):
        wq, wk, wv, place = _mla_weights(mla_w_qb[l], mla_w_kvb[l])
        shared["place"] = place
        layers.append({
            "g_pre_mix": norm_pre_mix[l][None, :],
            "w_in": _perm_w_in(w_in[l]),
            "lg": jnp.stack([jax.nn.log_sigmoid(ret_decay_fwd[l]), jax.nn.log_sigmoid(ret_decay_bwd[l])]),
            "wf": _block_diag([w_fmix[l, g] for g in range(N_HEADS)]).astype(BF16),
            "q_norm": mla_q_norm[l][None, :],
            "kv_norm": mla_kv_norm[l][None, :],
            "wq": wq, "wk": wk, "wv": wv,
            "w_out": w_out[l].astype(BF16),
            "g_post_mix": norm_post_mix[l][None, :],
            "g_pre_ffn": norm_pre_ffn[l][None, :],
            "w_up": w_up[l].astype(BF16),
            "conv_w": conv_w[l],
            "conv_b": conv_b[l][None, :],
            "w_down": w_down[l].astype(BF16),
            "g_post_ffn": norm_post_ffn[l][None, :],
        })
    y_prompt = _trunk(x_prompt, mods_p, layers, shared)
    y_sample = _trunk(x_sample, mods_s, layers, shared)
    return (y_prompt, y_sample)
```

```python
import functools
import math

import numpy as np
import jax
import jax.numpy as jnp
from jax import lax
from jax.experimental import pallas as pl
from jax.experimental.pallas import tpu as pltpu

F32 = jnp.float32
BF16 = jnp.bfloat16

D_MODEL = 1024
HEAD_DIM = 64
N_HEADS = 4
MIX_W = N_HEADS * HEAD_DIM
DIL_PAIRS = ((128, 1), (512, 4), (2048, 16))
N_DIL_GROUPS = 3
DIL_RADIUS = 64
MLA_NOPE = 64
MLA_ROPE = 32
Q_LORA = 256
KV_LORA = 128
D_FF = 2816
ROPE_THETA = 500000.0
RET_THETA = 10000.0
PARTIAL_ROT = HEAD_DIM // 4
EPS = 1e-6
NEG = -1e30

LANES = 128
D_IN_PAD = 4096
MLA_OFF = 3584
ROW_TILE = 512
RET_CHUNK = 256
DIL_TQ = 128
DIL_STEP_ROWS = 1024
FOUR_TR = 512
MLA_TQ = 256
MLA_STEP_SCORES = 2 ** 21
FFN_CHUNK = 256
HALO = 16
VMEM_LIMIT = 56 * 1024 * 1024


def _params(*sem):
    return pltpu.CompilerParams(dimension_semantics=sem, vmem_limit_bytes=VMEM_LIMIT)


def _rms(x, g):
    return x * lax.rsqrt(jnp.mean(x * x, axis=-1, keepdims=True) + EPS) * g


def _sigmoid(x):
    return 1.0 / (1.0 + jnp.exp(-x))


def _dot(a, b):
    return jnp.dot(a, b, preferred_element_type=F32)


def _dot_nt(a, b):
    return lax.dot_general(a, b, (((1,), (1,)), ((), ())), preferred_element_type=F32)


def _dot_tn(a, b):
    return lax.dot_general(a, b, (((0,), (0,)), ((), ())), preferred_element_type=F32)


def _ada_kernel(c_ref, w_ref, b_ref, o_ref):
    c = c_ref[...]
    cond = (c * _sigmoid(c)).astype(BF16)
    o_ref[0] = _dot(cond, w_ref[0].astype(BF16)) + b_ref[0]


def _ada(c_all, w_ada, b_ada):
    depth, _, n = w_ada.shape
    rows = c_all.shape[0]
    tn = 1536
    return pl.pallas_call(
        _ada_kernel,
        out_shape=jax.ShapeDtypeStruct((depth, rows, n), F32),
        grid=(depth, n // tn),
        in_specs=[
            pl.BlockSpec((rows, D_MODEL), lambda l, j: (0, 0)),
            pl.BlockSpec((1, D_MODEL, tn), lambda l, j: (l, 0, j)),
            pl.BlockSpec((1, 1, tn), lambda l, j: (l, 0, j)),
        ],
        out_specs=pl.BlockSpec((1, rows, tn), lambda l, j: (l, 0, j)),
        compiler_params=_params("arbitrary", "arbitrary"),
        name="ada",
    )(c_all, w_ada, b_ada.reshape(depth, 1, n))


def _inproj_kernel(x_ref, mod_ref, g_ref, w_ref, tab_ref, qn_ref, kvn_ref, wq_ref, wk_ref, wv_ref,
                   pk_ref, ret_ref, fu_ref, d0_ref, d1_ref, d2_ref, mq_ref, mk_ref, mv_ref, scr_ref, scr2_ref):
    x = x_ref[...]
    sh = mod_ref[0, 0:1, :]
    sc = mod_ref[0, 1:2, :]
    hb = (_rms(x, g_ref[...]) * (1.0 + sc) + sh).astype(BF16)

    def mm(c0, c1):
        return _dot(hb, w_ref[:, c0:c1])

    lane = lax.broadcasted_iota(jnp.int32, (1, LANES), 1)
    j64 = lane & (HEAD_DIM - 1)

    def make_rope(cos, sin, lo_mask, hi_mask, half):
        c = jnp.where(lo_mask | hi_mask, cos, 1.0)
        sa = jnp.where(lo_mask, -sin, 0.0)
        sb = jnp.where(hi_mask, sin, 0.0)

        def apply(z):
            return z * c + pltpu.roll(z, LANES - half, 1) * sa + pltpu.roll(z, half, 1) * sb
        return apply

    rope_ret = make_rope(tab_ref[:, 0:128], tab_ref[:, 128:256], j64 < 32, j64 >= 32, 32)
    rope_dil = make_rope(tab_ref[:, 256:384], tab_ref[:, 384:512], j64 < 8, (j64 >= 8) & (j64 < 16), 8)
    cos_m = tab_ref[:, 512:640]
    sin_m = tab_ref[:, 640:768]
    rope_kr = make_rope(cos_m, sin_m, lane < 16, (lane >= 16) & (lane < 32), 16)
    rope_mq = make_rope(cos_m, sin_m, (lane >= 64) & (lane < 80), (lane >= 80) & (lane < 96), 16)

    z = mm(MLA_OFF, D_IN_PAD)
    cqn = _rms(z[:, 0:Q_LORA], qn_ref[...]).astype(BF16)
    q = _dot(cqn, wq_ref[...])
    scale = (MLA_NOPE + MLA_ROPE) ** -0.5
    for h in range(N_HEADS):
        r = rope_mq(q[:, h * LANES:(h + 1) * LANES]) * scale
        mq_ref[:, h * LANES:(h + 1) * LANES] = r.astype(BF16)
    ckvn = _rms(z[:, Q_LORA:Q_LORA + KV_LORA], kvn_ref[...]).astype(BF16)
    kr = rope_kr(z[:, 384:512]).astype(BF16)
    mk_ref[...] = (_dot(ckvn, wk_ref[...]) + _dot(kr, pk_ref[...])).astype(BF16)
    mv_ref[...] = _dot(ckvn, wv_ref[...]).astype(BF16)
    z = mm(0, 512)
    for c in range(4):
        r = rope_ret(z[:, c * LANES:(c + 1) * LANES])
        if c >= 2:
            r = r * (HEAD_DIM ** -0.5)
        ret_ref[:, c * LANES:(c + 1) * LANES] = r.astype(BF16)
    ret_ref[:, 512:1024] = mm(512, 1024).astype(BF16)
    fu_ref[...] = mm(1024, 1280).astype(BF16)
    tm = x.shape[0]
    nslab = 3 * MIX_W // LANES
    for g, d_ref in enumerate((d0_ref, d1_ref, d2_ref)):
        dil = DIL_PAIRS[g][1]
        base = g * 3 * MIX_W
        z = mm(1280 + base, 1280 + base + 3 * MIX_W)
        slabs = []
        for c in range(nslab):
            r = z[:, c * LANES:(c + 1) * LANES]
            if c < 4:
                r = rope_dil(r)
            if c < 2:
                r = r * (HEAD_DIM ** -0.5)
            slabs.append(r)
        if dil == 1:
            d_ref[0, 0] = jnp.concatenate(slabs, axis=1).astype(BF16)
            continue
        for c in range(nslab):
            scr_ref[c] = slabs[c]
        n4 = tm // 4

        def rows4(ref, start, count):
            return jnp.concatenate([ref[c, pl.ds(start, count, stride=4), :] for c in range(nslab)], axis=1)

        if dil == 4:
            for r4 in range(4):
                d_ref[0, r4] = rows4(scr_ref, r4, n4).astype(BF16)
        else:
            for r4 in range(4):
                for c in range(nslab):
                    scr2_ref[c, r4 * n4:(r4 + 1) * n4, :] = scr_ref[c, pl.ds(r4, n4, stride=4), :]
            for r4 in range(4):
                for q4 in range(4):
                    d_ref[0, r4 + 4 * q4] = rows4(scr2_ref, r4 * n4 + q4, tm // 16).astype(BF16)


def _inproj(x2d, mod, g_pre, w_in_p, tab, qn, kvn, wq_p, wk_p, wv_p, pk, seq):
    t = x2d.shape[0]
    b = t // seq
    tm = ROW_TILE
    tps = seq // tm
    const = lambda i: (0, 0)
    row = lambda i: (i, 0)
    flat = [(t, 1024), (t, MIX_W)]
    flat2 = [(t, 512), (t, 512), (t, MIX_W)]
    dils = [d for _, d in DIL_PAIRS]
    assert dils == [1, 4, 16]
    bf = lambda shp: jax.ShapeDtypeStruct(shp, BF16)
    dil_shapes = [bf((b, d, seq // d, 3 * MIX_W)) for d in dils]
    dil_specs = [pl.BlockSpec((1, d, tm // d, 3 * MIX_W), lambda i: (i // tps, 0, i % tps, 0)) for d in dils]
    return pl.pallas_call(
        _inproj_kernel,
        out_shape=[bf(s) for s in flat] + dil_shapes + [bf(s) for s in flat2],
        grid=(t // tm,),
        in_specs=[
            pl.BlockSpec((tm, D_MODEL), row),
            pl.BlockSpec((1, 6, D_MODEL), lambda i: (i // tps, 0, 0)),
            pl.BlockSpec((1, D_MODEL), const),
            pl.BlockSpec((D_MODEL, D_IN_PAD), const),
            pl.BlockSpec((tm, 6 * LANES), lambda i: (i % tps, 0)),
            pl.BlockSpec((1, Q_LORA), const),
            pl.BlockSpec((1, KV_LORA), const),
            pl.BlockSpec((Q_LORA, 512), const),
            pl.BlockSpec((KV_LORA, 512), const),
            pl.BlockSpec((KV_LORA, MIX_W), const),
            pl.BlockSpec((LANES, 512), const),
        ],
        out_specs=[pl.BlockSpec((tm, s[1]), row) for s in flat] + dil_specs
                  + [pl.BlockSpec((tm, s[1]), row) for s in flat2],
        scratch_shapes=[pltpu.VMEM((3 * MIX_W // LANES, tm, LANES), F32),
                        pltpu.VMEM((3 * MIX_W // LANES, tm, LANES), F32)],
        compiler_params=_params("arbitrary"),
        name="inproj",
    )(x2d, mod, g_pre, w_in_p, tab, qn, kvn, wq_p, wk_p, wv_p, pk)


def _ret_kernel(lg_ref, q_ref, k_ref, v_ref, g_ref, o_ref, acc_ref, st_ref, dmat_ref, vec_ref, rdec_ref):
    c = RET_CHUNK
    seq = q_ref.shape[1]
    n_chunks = seq // c
    lane_head = lax.broadcasted_iota(jnp.int32, (1, MIX_W), 1) // HEAD_DIM
    row_head = lax.broadcasted_iota(jnp.int32, (MIX_W, 1), 0) // HEAD_DIM
    blockdiag = row_head == lane_head

    def per_head(idx, d):
        out = lg_ref[d, 0]
        for h in range(1, N_HEADS):
            out = jnp.where(idx == h, lg_ref[d, h], out)
        return out

    @pl.when(pl.program_id(0) == 0)
    def _tables():
        ri = lax.broadcasted_iota(jnp.int32, (c, c), 0)
        ci = lax.broadcasted_iota(jnp.int32, (c, c), 1)
        diff = (ri - ci).astype(F32)
        for h in range(N_HEADS):
            fwd = jnp.exp(jnp.where(diff >= 0, diff, 0.0) * lg_ref[0, h])
            bwd = jnp.exp(jnp.where(diff < 0, -diff, 0.0) * lg_ref[1, h])
            dmat_ref[h] = jnp.where(diff >= 0, fwd, bwd)
        pos = lax.broadcasted_iota(jnp.int32, (c, MIX_W), 0).astype(F32)
        lf = per_head(lane_head, 0)
        lb = per_head(lane_head, 1)
        vec_ref[0] = jnp.exp((pos + 1.0) * lf)
        vec_ref[1] = jnp.exp((c - 1.0 - pos) * lf)
        vec_ref[2] = jnp.exp((c - pos) * lb)
        vec_ref[3] = jnp.exp(pos * lb)
        rdec_ref[0] = jnp.broadcast_to(jnp.exp(c * per_head(row_head, 0)), (MIX_W, MIX_W))
        rdec_ref[1] = jnp.broadcast_to(jnp.exp(c * per_head(row_head, 1)), (MIX_W, MIX_W))

    ones_bd = jnp.where(blockdiag, 1.0, 0.0).astype(BF16)

    def chunk(ref, n):
        return ref[0, pl.ds(pl.multiple_of(n * c, c), c), :]

    def fwd_body(n, carry):
        qn, kn, vn = chunk(q_ref, n), chunk(k_ref, n), chunk(v_ref, n)
        acc = _dot((qn.astype(F32) * vec_ref[0]).astype(BF16), st_ref[...].astype(BF16))
        for h in range(N_HEADS):
            hm = lane_head == h
            s = _dot_nt(jnp.where(hm, qn, jnp.zeros_like(qn)), kn)
            p = (s * dmat_ref[h]).astype(BF16)
            acc = acc + _dot(p, jnp.where(hm, vn, jnp.zeros_like(vn)))
        acc_ref[pl.ds(pl.multiple_of(n * c, c), c), :] = acc
        kv = _dot_tn((kn.astype(F32) * vec_ref[1]).astype(BF16), vn)
        st_ref[...] = st_ref[...] * rdec_ref[0] + jnp.where(blockdiag, kv, 0.0)
        return carry

    st_ref[...] = jnp.zeros_like(st_ref)
    lax.fori_loop(0, n_chunks, fwd_body, 0, unroll=2)

    def bwd_body(t, carry):
        n = n_chunks - 1 - t
        qn, kn, vn = chunk(q_ref, n), chunk(k_ref, n), chunk(v_ref, n)
        r0 = pl.multiple_of(n * c, c)
        o = acc_ref[pl.ds(r0, c), :] + _dot((qn.astype(F32) * vec_ref[2]).astype(BF16),
                                           st_ref[...].astype(BF16))
        o2 = o * o
        hi = o2.astype(BF16)
        lo = (o2 - hi.astype(F32)).astype(BF16)
        ms = (_dot(hi, ones_bd) + _dot(lo, ones_bd)) * (1.0 / HEAD_DIM)
        gate = chunk(g_ref, n).astype(F32)
        o_ref[0, pl.ds(r0, c), :] = (gate * _sigmoid(gate) * (o * lax.rsqrt(ms + EPS))).astype(BF16)
        kv = _dot_tn((kn.astype(F32) * vec_ref[3]).astype(BF16), vn)
        st_ref[...] = st_ref[...] * rdec_ref[1] + jnp.where(blockdiag, kv, 0.0)
        return carry

    st_ref[...] = jnp.zeros_like(st_ref)
    lax.fori_loop(0, n_chunks, bwd_body, 0, unroll=2)


def _retention(ret3d, lg):
    b, seq, _ = ret3d.shape
    spec = lambda col: pl.BlockSpec((1, seq, MIX_W), lambda i, col=col: (i, 0, col))
    return pl.pallas_call(
        _ret_kernel,
        out_shape=jax.ShapeDtypeStruct((b, seq, MIX_W), BF16),
        grid=(b,),
        in_specs=[pl.BlockSpec(memory_space=pltpu.SMEM), spec(0), spec(1), spec(2), spec(3)],
        out_specs=pl.BlockSpec((1, seq, MIX_W), lambda i: (i, 0, 0)),
        scratch_shapes=[
            pltpu.VMEM((seq, MIX_W), F32),
            pltpu.VMEM((MIX_W, MIX_W), F32),
            pltpu.VMEM((N_HEADS, RET_CHUNK, RET_CHUNK), F32),
            pltpu.VMEM((4, RET_CHUNK, MIX_W), F32),
            pltpu.VMEM((2, MIX_W, MIX_W), F32),
        ],
        compiler_params=_params("arbitrary"),
        name="retention",
    )(lg, ret3d, ret3d, ret3d, ret3d)


def _fourier_kernel(cs_ref, ss_ref, u_ref, cc_ref, sc_ref, wf_ref, o_ref, *, scale):
    u = u_ref[0]
    z1 = _dot(cs_ref[...], u).astype(BF16)
    z2 = _dot(ss_ref[...], u).astype(BF16)
    f = (_dot(z1, cc_ref[...]) - _dot(z2, sc_ref[...])) * scale
    o_ref[0] = _dot(f.astype(BF16), wf_ref[...]).astype(BF16)


def _fourier(fu3d, cs, ss, cc, sc, wf):
    b, seq, _ = fu3d.shape
    tr = FOUR_TR
    const = lambda i, j: (0, 0)
    return pl.pallas_call(
        functools.partial(_fourier_kernel, scale=1.0 / math.sqrt(seq * HEAD_DIM)),
        out_shape=jax.ShapeDtypeStruct((b, seq, MIX_W), BF16),
        grid=(seq // tr, b),
        in_specs=[
            pl.BlockSpec((tr, seq), lambda i, j: (i, 0)),
            pl.BlockSpec((tr, seq), lambda i, j: (i, 0)),
            pl.BlockSpec((1, seq, MIX_W), lambda i, j: (j, 0, 0)),
            pl.BlockSpec((MIX_W, MIX_W), const),
            pl.BlockSpec((MIX_W, MIX_W), const),
            pl.BlockSpec((MIX_W, MIX_W), const),
        ],
        out_specs=pl.BlockSpec((1, tr, MIX_W), lambda i, j: (j, i, 0)),
        compiler_params=_params("arbitrary", "arbitrary"),
        name="fourier",
    )(cs, ss, fu3d, cc, sc, wf)


def _dil_kernel(q_ref, k_ref, v_ref, o_ref, l_ref, *, sub_len, ts, rb, tq, win):
    j = pl.program_id(2)
    lane_head = lax.broadcasted_iota(jnp.int32, (1, MIX_W), 1) // HEAD_DIM
    lse_head = lax.broadcasted_iota(jnp.int32, (1, LANES), 1) // (LANES // N_HEADS)
    rel = (lax.broadcasted_iota(jnp.int32, (N_HEADS * tq, win), 0) & (tq - 1)) \
        - lax.broadcasted_iota(jnp.int32, (N_HEADS * tq, win), 1)
    nblk = ts // tq
    interior = jnp.abs(rel + DIL_RADIUS) <= DIL_RADIUS
    for rr in range(rb):
        for blk in range(nblk):
            q0 = j * ts + blk * tq
            if win == sub_len:
                ws = 0
            else:
                ws = pl.multiple_of(jnp.clip(q0 - DIL_RADIUS, 0, sub_len - win), DIL_RADIUS)
            valid = interior if 0 < blk < nblk - 1 else jnp.abs(rel + (q0 - ws)) <= DIL_RADIUS
            q = q_ref[0, rr, blk * tq:(blk + 1) * tq, :]
            kw = k_ref[0, rr, pl.ds(ws, win), :]
            vw = v_ref[0, rr, pl.ds(ws, win), :]
            zero = jnp.zeros_like(q)
            qs = jnp.concatenate([jnp.where(lane_head == h, q, zero) for h in range(N_HEADS)], axis=0)
            s = _dot_nt(qs, kw)
            s = jnp.where(valid, s, NEG)
            m = jnp.max(s, axis=-1, keepdims=True)
            p = jnp.exp(s - m)
            den = jnp.sum(p, axis=-1, keepdims=True)
            r = _dot((p * (1.0 / den)).astype(BF16), vw)
            lse = m + jnp.log(den)
            o = jnp.zeros((tq, MIX_W), F32)
            lo = jnp.zeros((tq, LANES), F32)
            for h in range(N_HEADS):
                o = jnp.where(lane_head == h, r[h * tq:(h + 1) * tq], o)
                lo = jnp.where(lse_head == h, lse[h * tq:(h + 1) * tq], lo)
            o_ref[0, rr, blk * tq:(blk + 1) * tq, :] = o.astype(BF16)
            l_ref[0, rr, blk * tq:(blk + 1) * tq, :] = lo


def _dilated(dg, group):
    b, dil, sub_len, _ = dg.shape
    tq = min(DIL_TQ, sub_len)
    win = min(tq + 2 * DIL_RADIUS, sub_len)
    ts = min(sub_len, DIL_STEP_ROWS)
    rb = min(dil, DIL_STEP_ROWS // ts)
    part = lambda c: (lambda i, r, j: (i, r, 0, c))
    return pl.pallas_call(
        functools.partial(_dil_kernel, sub_len=sub_len, ts=ts, rb=rb, tq=tq, win=win),
        out_shape=[jax.ShapeDtypeStruct((b, dil, sub_len, MIX_W), BF16),
                   jax.ShapeDtypeStruct((b, dil, sub_len, LANES), F32)],
        grid=(b, dil // rb, sub_len // ts),
        in_specs=[
            pl.BlockSpec((1, rb, ts, MIX_W), lambda i, r, j: (i, r, j, 0)),
            pl.BlockSpec((1, rb, sub_len, MIX_W), part(1)),
            pl.BlockSpec((1, rb, sub_len, MIX_W), part(2)),
        ],
        out_specs=[pl.BlockSpec((1, rb, ts, MIX_W), lambda i, r, j: (i, r, j, 0)),
                   pl.BlockSpec((1, rb, ts, LANES), lambda i, r, j: (i, r, j, 0))],
        compiler_params=_params("arbitrary", "arbitrary", "arbitrary"),
        name=f"dilated{group}",
    )(dg, dg, dg)


def _mla_kernel(q_ref, k_ref, v_ref, o_ref):
    lane_head = lax.broadcasted_iota(jnp.int32, (1, MIX_W), 1) // HEAD_DIM
    v = v_ref[0]
    for b0 in range(0, q_ref.shape[1], MLA_TQ):
        out = jnp.zeros((MLA_TQ, MIX_W), F32)
        for h in range(N_HEADS):
            qh = q_ref[0, b0:b0 + MLA_TQ, h * LANES:(h + 1) * LANES]
            s = _dot_nt(qh, k_ref[0, :, h * LANES:(h + 1) * LANES])
            m = jnp.max(s, axis=-1, keepdims=True)
            p = jnp.exp(s - m)
            den = jnp.sum(p, axis=-1, keepdims=True)
            o = _dot(p.astype(BF16), v) * (1.0 / den)
            out = jnp.where(lane_head == h, o, out)
        o_ref[0, b0:b0 + MLA_TQ, :] = out.astype(BF16)


def _mla(mq3d, mk3d, mv3d):
    b, seq, _ = mq3d.shape
    ts = max(MLA_TQ, min(seq, MLA_STEP_SCORES // seq))
    return pl.pallas_call(
        _mla_kernel,
        out_shape=jax.ShapeDtypeStruct((b, seq, MIX_W), BF16),
        grid=(b, seq // ts),
        in_specs=[
            pl.BlockSpec((1, ts, N_HEADS * LANES), lambda i, j: (i, j, 0)),
            pl.BlockSpec((1, seq, N_HEADS * LANES), lambda i, j: (i, 0, 0)),
            pl.BlockSpec((1, seq, MIX_W), lambda i, j: (i, 0, 0)),
        ],
        out_specs=pl.BlockSpec((1, ts, MIX_W), lambda i, j: (i, j, 0)),
        compiler_params=_params("arbitrary", "arbitrary"),
        name="mla",
    )(mq3d, mk3d, mv3d)


def _outproj_kernel(x_ref, mod_ref, ro_ref, fo_ref, d0_ref, d1_ref, d2_ref, l0_ref, l1_ref, l2_ref,
                    mo_ref, w_ref, gpm_ref, gpf_ref, x1_ref, h2_ref, so1_ref, sl1_ref, so2_ref, sl2_ref):
    tm = x_ref.shape[0]

    def natural_order(o_ref, l_ref, so_ref, sl_ref):
        dil = o_ref.shape[1]
        if dil == 1:
            return o_ref[0, 0].astype(F32), l_ref[0, 0]
        n = tm // dil
        for r in range(dil):
            o = o_ref[0, r].astype(F32)
            for c in range(MIX_W // LANES):
                so_ref[c, pl.ds(r, n, stride=dil), :] = o[:, c * LANES:(c + 1) * LANES]
            sl_ref[pl.ds(r, n, stride=dil), :] = l_ref[0, r]
        return jnp.concatenate([so_ref[c] for c in range(MIX_W // LANES)], axis=1), sl_ref[...]

    o0, l0 = natural_order(d0_ref, l0_ref, None, None)
    o1, l1 = natural_order(d1_ref, l1_ref, so1_ref, sl1_ref)
    o2, l2 = natural_order(d2_ref, l2_ref, so2_ref, sl2_ref)
    m = jnp.maximum(l0, jnp.maximum(l1, l2))
    e0, e1, e2 = jnp.exp(l0 - m), jnp.exp(l1 - m), jnp.exp(l2 - m)
    inv = 1.0 / (e0 + e1 + e2)
    src = lax.broadcasted_iota(jnp.int32, (LANES, MIX_W), 0)
    dst = lax.broadcasted_iota(jnp.int32, (LANES, MIX_W), 1)
    spread = jnp.where(src == (dst // HEAD_DIM) * (LANES // N_HEADS), 1.0, 0.0).astype(BF16)
    od = (_dot((e0 * inv).astype(BF16), spread) * o0 + _dot((e1 * inv).astype(BF16), spread) * o1
          + _dot((e2 * inv).astype(BF16), spread) * o2).astype(BF16)
    y = (_dot(ro_ref[...], w_ref[0:256, :]) + _dot(fo_ref[...], w_ref[256:512, :])
         + _dot(od, w_ref[512:768, :]) + _dot(mo_ref[...], w_ref[768:1024, :]))
    g1 = mod_ref[0, 2:3, :]
    sh2 = mod_ref[0, 3:4, :]
    sc2 = mod_ref[0, 4:5, :]
    x1 = x_ref[...] + g1 * _rms(y, gpm_ref[...])
    x1_ref[...] = x1
    h2_ref[...] = (_rms(x1, gpf_ref[...]) * (1.0 + sc2) + sh2).astype(BF16)


def _outproj(x2d, mod, ro, fo, d_o, d_l, mo, w_out, g_post_mix, g_pre_ffn, seq):
    t = x2d.shape[0]
    tm = ROW_TILE
    tps = seq // tm
    row = lambda i: (i, 0)
    const = lambda i: (0, 0)
    mix = pl.BlockSpec((tm, MIX_W), row)
    res = lambda a: pl.BlockSpec((1, a.shape[1], tm // a.shape[1], a.shape[3]), lambda i: (i // tps, 0, i % tps, 0))
    return pl.pallas_call(
        _outproj_kernel,
        out_shape=[jax.ShapeDtypeStruct((t, D_MODEL), F32), jax.ShapeDtypeStruct((t, D_MODEL), BF16)],
        grid=(t // tm,),
        in_specs=[
            pl.BlockSpec((tm, D_MODEL), row),
            pl.BlockSpec((1, 6, D_MODEL), lambda i: (i // tps, 0, 0)),
            mix, mix, res(d_o[0]), res(d_o[1]), res(d_o[2]), res(d_l[0]), res(d_l[1]), res(d_l[2]), mix,
            pl.BlockSpec((D_MODEL, D_MODEL), const),
            pl.BlockSpec((1, D_MODEL), const),
            pl.BlockSpec((1, D_MODEL), const),
        ],
        out_specs=[pl.BlockSpec((tm, D_MODEL), row), pl.BlockSpec((tm, D_MODEL), row)],
        scratch_shapes=[pltpu.VMEM((MIX_W // LANES, tm, LANES), F32), pltpu.VMEM((tm, LANES), F32),
                        pltpu.VMEM((MIX_W // LANES, tm, LANES), F32), pltpu.VMEM((tm, LANES), F32)],
        compiler_params=_params("arbitrary"),
        name="outproj",
    )(x2d, mod, ro, fo, d_o[0], d_o[1], d_o[2], d_l[0], d_l[1], d_l[2], mo, w_out, g_post_mix, g_pre_ffn)


def _ffn_kernel(hp_ref, hc_ref, hn_ref, x1_ref, mod_ref, wu_ref, cw_ref, cb_ref, wd_ref, g_ref, o_ref,
                gate_ref, *, tiles_per_seq):
    tm = hc_ref.shape[0]
    t = pl.program_id(0) % tiles_per_seq
    hp = jnp.where(t == 0, jnp.zeros_like(hp_ref[...]), hp_ref[...])
    hn = jnp.where(t == tiles_per_seq - 1, jnp.zeros_like(hn_ref[...]), hn_ref[...])
    he = jnp.concatenate([hp, hc_ref[...], hn], axis=0)

    ext = tm + 2 * HALO

    def conv(c0):
        u = _dot(he, wu_ref[:, c0:c0 + FFN_CHUNK])
        w = cw_ref[:, c0:c0 + FFN_CHUNK]
        prev = pltpu.roll(u, 1, 0)[HALO:HALO + tm]
        nxt = pltpu.roll(u, ext - 1, 0)[HALO:HALO + tm]
        return prev * w[0:1] + u[HALO:HALO + tm] * w[1:2] + nxt * w[2:3] + cb_ref[:, c0:c0 + FFN_CHUNK]

    for c in range(D_FF // FFN_CHUNK):
        a = conv(c * FFN_CHUNK)
        bu = conv(D_FF + c * FFN_CHUNK)
        gate_ref[:, c * FFN_CHUNK:(c + 1) * FFN_CHUNK] = (a * _sigmoid(a) * bu).astype(BF16)
    acc = _dot(gate_ref[...], wd_ref[...])
    g2 = mod_ref[0, 5:6, :]
    o_ref[...] = x1_ref[...] + g2 * _rms(acc, g_ref[...])


def _ffn(h2, x1, mod, w_up, conv_w, conv_b, w_down, g_post_ffn, seq):
    t = x1.shape[0]
    tm = ROW_TILE
    tps = seq // tm
    hb = tm // HALO
    row = lambda i: (i, 0)
    const = lambda i: (0, 0)
    return pl.pallas_call(
        functools.partial(_ffn_kernel, tiles_per_seq=tps),
        out_shape=jax.ShapeDtypeStruct((t, D_MODEL), F32),
        grid=(t // tm,),
        in_specs=[
            pl.BlockSpec((HALO, D_MODEL), lambda i: (jnp.maximum(i * hb - 1, 0), 0)),
            pl.BlockSpec((tm, D_MODEL), row),
            pl.BlockSpec((HALO, D_MODEL), lambda i: (jnp.minimum((i + 1) * hb, t // HALO - 1), 0)),
            pl.BlockSpec((tm, D_MODEL), row),
            pl.BlockSpec((1, 6, D_MODEL), lambda i: (i // tps, 0, 0)),
            pl.BlockSpec((D_MODEL, 2 * D_FF), const),
            pl.BlockSpec((3, 2 * D_FF), const),
            pl.BlockSpec((1, 2 * D_FF), const),
            pl.BlockSpec((D_FF, D_MODEL), const),
            pl.BlockSpec((1, D_MODEL), const),
        ],
        out_specs=pl.BlockSpec((tm, D_MODEL), row),
        scratch_shapes=[pltpu.VMEM((tm, D_FF), BF16)],
        compiler_params=_params("arbitrary"),
        name="ffn",
    )(h2, h2, h2, x1, mod, w_up, conv_w, conv_b, w_down, g_post_ffn)


def _rope_tables(seq):
    pos = jnp.arange(seq, dtype=F32)[:, None]
    lane = np.arange(LANES)
    cols = []
    for theta, rot in ((RET_THETA, HEAD_DIM), (ROPE_THETA, PARTIAL_ROT), (ROPE_THETA, MLA_ROPE)):
        half = rot // 2
        inv = jnp.power(theta, -jnp.arange(half, dtype=F32) * 2.0 / rot)
        ang = pos * inv[lane % half][None, :]
        cols += [jnp.cos(ang), jnp.sin(ang)]
    return jnp.concatenate(cols, axis=1)


def _dft_tables(seq):
    n2 = 64
    n1 = seq // n2
    k = np.arange(seq)[:, None]
    a = 2.0 * np.pi * ((k * np.arange(n1)[None, :] * n2) % seq) / seq
    b = 2.0 * np.pi * ((k * np.arange(n2)[None, :]) % seq) / seq
    ca, sa = jnp.asarray(np.cos(a), F32)[:, :, None], jnp.asarray(np.sin(a), F32)[:, :, None]
    cb, sb = jnp.asarray(np.cos(b), F32)[:, None, :], jnp.asarray(np.sin(b), F32)[:, None, :]
    cs = (ca * cb - sa * sb).reshape(seq, seq).astype(BF16)
    ss = (sa * cb + ca * sb).reshape(seq, seq).astype(BF16)
    return cs, ss


def _block_diag(blocks):
    n = len(blocks)
    rows = [jnp.concatenate([blocks[i] if i == j else jnp.zeros_like(blocks[i]) for j in range(n)], axis=1)
            for i in range(n)]
    return jnp.concatenate(rows, axis=0)


def _perm_w_in(w_in):
    ret = w_in[:, 0:1280]
    dq, dk, dv = w_in[:, 1280:2048], w_in[:, 2048:2816], w_in[:, 2816:3584]
    groups = [jnp.concatenate([m[:, g * MIX_W:(g + 1) * MIX_W] for m in (dq, dk, dv)], axis=1)
              for g in range(N_DIL_GROUPS)]
    pad = jnp.zeros((D_MODEL, D_IN_PAD - w_in.shape[1]), w_in.dtype)
    return jnp.concatenate([ret] + groups + [w_in[:, 3584:], pad], axis=1).astype(BF16)


def _mla_weights(w_qb, w_kvb):
    qh = w_qb.reshape(Q_LORA, N_HEADS, MLA_NOPE + MLA_ROPE)
    wq = jnp.pad(qh, ((0, 0), (0, 0), (0, LANES - MLA_NOPE - MLA_ROPE))).reshape(Q_LORA, N_HEADS * LANES)
    kvh = w_kvb.reshape(KV_LORA, N_HEADS, MLA_NOPE + HEAD_DIM)
    wk = jnp.pad(kvh[:, :, :MLA_NOPE], ((0, 0), (0, 0), (0, LANES - MLA_NOPE))).reshape(KV_LORA, N_HEADS * LANES)
    wv = kvh[:, :, MLA_NOPE:].reshape(KV_LORA, MIX_W)
    place = np.zeros((LANES, N_HEADS * LANES), np.float32)
    for h in range(N_HEADS):
        for r in range(MLA_ROPE):
            place[r, h * LANES + MLA_NOPE + r] = 1.0
    return wq.astype(BF16), wk.astype(BF16), wv.astype(BF16), jnp.asarray(place, BF16)


def _trunk(x, mods, layers, shared):
    b, seq, _ = x.shape
    t = b * seq
    x2d = x.reshape(t, D_MODEL)
    tab = _rope_tables(seq)
    cs, ss = _dft_tables(seq)
    for mod, lw in zip(mods, layers):
        ret, fu, dg0, dg1, dg2, mq, mk, mv = _inproj(x2d, mod, lw["g_pre_mix"], lw["w_in"], tab, lw["q_norm"],
                                                     lw["kv_norm"], lw["wq"], lw["wk"], lw["wv"],
                                                     shared["place"], seq)
        ro = _retention(ret.reshape(b, seq, 1024), lw["lg"]).reshape(t, MIX_W)
        fo = _fourier(fu.reshape(b, seq, MIX_W), cs, ss, shared["cc"], shared["sc"], lw["wf"]).reshape(t, MIX_W)
        d_o, d_l = zip(*[_dilated(dg, g) for g, dg in enumerate((dg0, dg1, dg2))])
        mo = _mla(mq.reshape(b, seq, 512), mk.reshape(b, seq, 512), mv.reshape(b, seq, MIX_W)).reshape(t, MIX_W)
        x1, h2 = _outproj(x2d, mod, ro, fo, d_o, d_l, mo, lw["w_out"], lw["g_post_mix"], lw["g_pre_ffn"], seq)
        x2d = _ffn(h2, x1, mod, lw["w_up"], lw["conv_w"], lw["conv_b"], lw["w_down"], lw["g_post_ffn"], seq)
    return x2d.reshape(b, seq, D_MODEL)


def kernel(x_prompt, x_sample, c_prompt, c_sample, w_ada, b_ada, norm_pre_mix, w_in, ret_decay_fwd,
           ret_decay_bwd, w_fmix, mla_q_norm, mla_w_qb, mla_kv_norm, mla_w_kvb, w_out, norm_post_mix,
           norm_pre_ffn, w_up, conv_w, conv_b, w_down, norm_post_ffn):
    depth = w_in.shape[0]
    nb_p, nb_s = c_prompt.shape[0], c_sample.shape[0]
    rows = -(-(nb_p + nb_s) // 8) * 8
    c_all = jnp.concatenate([c_prompt, c_sample, jnp.zeros((rows - nb_p - nb_s, D_MODEL), F32)], axis=0)
    mod_all = _ada(c_all, w_ada, b_ada)
    mods_p = [mod_all[l, :nb_p].reshape(nb_p, 6, D_MODEL) for l in range(depth)]
    mods_s = [mod_all[l, nb_p:nb_p + nb_s].reshape(nb_s, 6, D_MODEL) for l in range(depth)]

    c64 = 2.0 * np.pi * np.outer(np.arange(HEAD_DIM), np.arange(HEAD_DIM)) / HEAD_DIM
    shared = {
        "cc": _block_diag([jnp.asarray(np.cos(c64), BF16)] * N_HEADS),
        "sc": _block_diag([jnp.asarray(np.sin(c64), BF16)] * N_HEADS),
    }
    layers = []
    for l in range(depth):
        wq, wk, wv, place = _mla_weights(mla_w_qb[l], mla_w_kvb[l])
        shared["place"] = place
        layers.append({
            "g_pre_mix": norm_pre_mix[l][None, :],
            "w_in": _perm_w_in(w_in[l]),
            "lg": jnp.stack([jax.nn.log_sigmoid(ret_decay_fwd[l]), jax.nn.log_sigmoid(ret_decay_bwd[l])]),
            "wf": _block_diag([w_fmix[l, g] for g in range(N_HEADS)]).astype(BF16),
            "q_norm": mla_q_norm[l][None, :],
            "kv_norm": mla_kv_norm[l][None, :],
            "wq": wq, "wk": wk, "wv": wv,
            "w_out": w_out[l].astype(BF16),
            "g_post_mix": norm_post_mix[l][None, :],
            "g_pre_ffn": norm_pre_ffn[l][None, :],
            "w_up": w_up[l].astype(BF16),
            "conv_w": conv_w[l],
            "conv_b": conv_b[l][None, :],
            "w_down": w_down[l].astype(BF16),
            "g_post_ffn": norm_post_ffn[l][None, :],
        })
    y_prompt = _trunk(x_prompt, mods_p, layers, shared)
    y_sample = _trunk(x_sample, mods_s, layers, shared)
    return (y_prompt, y_sample)
```

```python
import functools
import math

import numpy as np
import jax
import jax.numpy as jnp
from jax import lax
from jax.experimental import pallas as pl
from jax.experimental.pallas import tpu as pltpu

F32 = jnp.float32
BF16 = jnp.bfloat16

D_MODEL = 1024
HEAD_DIM = 64
N_HEADS = 4
MIX_W = N_HEADS * HEAD_DIM
DIL_PAIRS = ((128, 1), (512, 4), (2048, 16))
N_DIL_GROUPS = 3
DIL_RADIUS = 64
STAT_W = 16
MLA_NOPE = 64
MLA_ROPE = 32
Q_LORA = 256
KV_LORA = 128
D_FF = 2816
ROPE_THETA = 500000.0
RET_THETA = 10000.0
PARTIAL_ROT = HEAD_DIM // 4
EPS = 1e-6
NEG = -1e30
LOG2E = math.log2(math.e)

LANES = 128
D_IN_PAD = 4096
MLA_OFF = 3584
ROW_TILE = 512
FFN_TILE = 1024
RET_CHUNK = 256
DIL_TQ = 128
DIL_STEP_ROWS = 1024
FOUR_TR = 512
MLA_TQ = 256
MLA_STEP_SCORES = 2 ** 21
FFN_CHUNK = 256
HALO = 16
VMEM_LIMIT = 56 * 1024 * 1024


def _params(*sem):
    return pltpu.CompilerParams(dimension_semantics=sem, vmem_limit_bytes=VMEM_LIMIT)


def _rms(x, g):
    return x * lax.rsqrt(jnp.mean(x * x, axis=-1, keepdims=True) + EPS) * g


def _sigmoid(x):
    return 1.0 / (1.0 + jnp.exp(-x))


def _dot(a, b):
    return jnp.dot(a, b, preferred_element_type=F32)


def _dot_nt(a, b):
    return lax.dot_general(a, b, (((1,), (1,)), ((), ())), preferred_element_type=F32)


def _dot_tn(a, b):
    return lax.dot_general(a, b, (((0,), (0,)), ((), ())), preferred_element_type=F32)


def _ada_kernel(c_ref, w_ref, b_ref, o_ref):
    c = c_ref[...]
    cond = (c * _sigmoid(c)).astype(BF16)
    o_ref[0] = _dot(cond, w_ref[0].astype(BF16)) + b_ref[0]


def _ada(c_all, w_ada, b_ada):
    depth, _, n = w_ada.shape
    rows = c_all.shape[0]
    tn = 1536
    return pl.pallas_call(
        _ada_kernel,
        out_shape=jax.ShapeDtypeStruct((depth, rows, n), F32),
        grid=(depth, n // tn),
        in_specs=[
            pl.BlockSpec((rows, D_MODEL), lambda l, j: (0, 0)),
            pl.BlockSpec((1, D_MODEL, tn), lambda l, j: (l, 0, j)),
            pl.BlockSpec((1, 1, tn), lambda l, j: (l, 0, j)),
        ],
        out_specs=pl.BlockSpec((1, rows, tn), lambda l, j: (l, 0, j)),
        compiler_params=_params("arbitrary", "arbitrary"),
        name="ada",
    )(c_all, w_ada, b_ada.reshape(depth, 1, n))


def _inproj_kernel(x_ref, mod_ref, g_ref, w_ref, tab_ref, qn_ref, kvn_ref, wq_ref, wk_ref, wv_ref,
                   pk_ref, ret_ref, fu_ref, d0_ref, d1_ref, d2_ref, mq_ref, mk_ref, mv_ref, scr_ref, scr2_ref):
    x = x_ref[...]
    sh = mod_ref[0, 0:1, :]
    sc = mod_ref[0, 1:2, :]
    hb = (_rms(x, g_ref[...]) * (1.0 + sc) + sh).astype(BF16)

    def mm(c0, c1):
        return _dot(hb, w_ref[:, c0:c1])

    lane = lax.broadcasted_iota(jnp.int32, (1, LANES), 1)
    j64 = lane & (HEAD_DIM - 1)

    def make_rope(cos, sin, lo_mask, hi_mask, half):
        c = jnp.where(lo_mask | hi_mask, cos, 1.0)
        sa = jnp.where(lo_mask, -sin, 0.0)
        sb = jnp.where(hi_mask, sin, 0.0)

        def apply(z):
            return z * c + pltpu.roll(z, LANES - half, 1) * sa + pltpu.roll(z, half, 1) * sb
        return apply

    rope_ret = make_rope(tab_ref[:, 0:128], tab_ref[:, 128:256], j64 < 32, j64 >= 32, 32)
    rope_dil = make_rope(tab_ref[:, 256:384], tab_ref[:, 384:512], j64 < 8, (j64 >= 8) & (j64 < 16), 8)
    cos_m = tab_ref[:, 512:640]
    sin_m = tab_ref[:, 640:768]
    rope_kr = make_rope(cos_m, sin_m, lane < 16, (lane >= 16) & (lane < 32), 16)
    rope_mq = make_rope(cos_m, sin_m, (lane >= 64) & (lane < 80), (lane >= 80) & (lane < 96), 16)

    z = mm(MLA_OFF, D_IN_PAD)
    cqn = _rms(z[:, 0:Q_LORA], qn_ref[...]).astype(BF16)
    q = _dot(cqn, wq_ref[...])
    scale = (MLA_NOPE + MLA_ROPE) ** -0.5 * LOG2E
    for h in range(N_HEADS):
        r = rope_mq(q[:, h * LANES:(h + 1) * LANES]) * scale
        mq_ref[:, h * LANES:(h + 1) * LANES] = r.astype(BF16)
    ckvn = _rms(z[:, Q_LORA:Q_LORA + KV_LORA], kvn_ref[...]).astype(BF16)
    kr = rope_kr(z[:, 384:512]).astype(BF16)
    mk_ref[...] = (_dot(ckvn, wk_ref[...]) + _dot(kr, pk_ref[...])).astype(BF16)
    mv_ref[...] = _dot(ckvn, wv_ref[...]).astype(BF16)
    z = mm(0, 512)
    for c in range(4):
        r = rope_ret(z[:, c * LANES:(c + 1) * LANES])
        if c >= 2:
            r = r * (HEAD_DIM ** -0.5)
        ret_ref[:, c * LANES:(c + 1) * LANES] = r.astype(BF16)
    ret_ref[:, 512:1024] = mm(512, 1024).astype(BF16)
    fu_ref[...] = mm(1024, 1280).astype(BF16)
    tm = x.shape[0]
    nslab = 3 * MIX_W // LANES
    for g, d_ref in enumerate((d0_ref, d1_ref, d2_ref)):
        dil = DIL_PAIRS[g][1]
        base = g * 3 * MIX_W
        z = mm(1280 + base, 1280 + base + 3 * MIX_W)
        slabs = []
        for c in range(nslab):
            r = z[:, c * LANES:(c + 1) * LANES]
            if c < 4:
                r = rope_dil(r)
            if c < 2:
                r = r * (HEAD_DIM ** -0.5 * LOG2E)
            slabs.append(r)
        if dil == 1:
            d_ref[0, 0] = jnp.concatenate(slabs, axis=1).astype(BF16)
            continue
        for c in range(nslab):
            scr_ref[c] = slabs[c]
        n4 = tm // 4

        def rows4(ref, start, count):
            return jnp.concatenate([ref[c, pl.ds(start, count, stride=4), :] for c in range(nslab)], axis=1)

        if dil == 4:
            for r4 in range(4):
                d_ref[0, r4] = rows4(scr_ref, r4, n4).astype(BF16)
        else:
            for r4 in range(4):
                for c in range(nslab):
                    scr2_ref[c, r4 * n4:(r4 + 1) * n4, :] = scr_ref[c, pl.ds(r4, n4, stride=4), :]
            for r4 in range(4):
                for q4 in range(4):
                    d_ref[0, r4 + 4 * q4] = rows4(scr2_ref, r4 * n4 + q4, tm // 16).astype(BF16)


def _inproj(x2d, mod, g_pre, w_in_p, tab, qn, kvn, wq_p, wk_p, wv_p, pk, seq):
    t = x2d.shape[0]
    b = t // seq
    tm = ROW_TILE
    tps = seq // tm
    const = lambda i: (0, 0)
    row = lambda i: (i, 0)
    flat = [(t, 1024), (t, MIX_W)]
    flat2 = [(t, 512), (t, 512), (t, MIX_W)]
    dils = [d for _, d in DIL_PAIRS]
    assert dils == [1, 4, 16]
    bf = lambda shp: jax.ShapeDtypeStruct(shp, BF16)
    dil_shapes = [bf((b, d, seq // d, 3 * MIX_W)) for d in dils]
    dil_specs = [pl.BlockSpec((1, d, tm // d, 3 * MIX_W), lambda i: (i // tps, 0, i % tps, 0)) for d in dils]
    return pl.pallas_call(
        _inproj_kernel,
        out_shape=[bf(s) for s in flat] + dil_shapes + [bf(s) for s in flat2],
        grid=(t // tm,),
        in_specs=[
            pl.BlockSpec((tm, D_MODEL), row),
            pl.BlockSpec((1, 6, D_MODEL), lambda i: (i // tps, 0, 0)),
            pl.BlockSpec((1, D_MODEL), const),
            pl.BlockSpec((D_MODEL, D_IN_PAD), const),
            pl.BlockSpec((tm, 6 * LANES), lambda i: (i % tps, 0)),
            pl.BlockSpec((1, Q_LORA), const),
            pl.BlockSpec((1, KV_LORA), const),
            pl.BlockSpec((Q_LORA, 512), const),
            pl.BlockSpec((KV_LORA, 512), const),
            pl.BlockSpec((KV_LORA, MIX_W), const),
            pl.BlockSpec((LANES, 512), const),
        ],
        out_specs=[pl.BlockSpec((tm, s[1]), row) for s in flat] + dil_specs
                  + [pl.BlockSpec((tm, s[1]), row) for s in flat2],
        scratch_shapes=[pltpu.VMEM((3 * MIX_W // LANES, tm, LANES), F32),
                        pltpu.VMEM((3 * MIX_W // LANES, tm, LANES), F32)],
        compiler_params=_params("arbitrary"),
        name="inproj",
    )(x2d, mod, g_pre, w_in_p, tab, qn, kvn, wq_p, wk_p, wv_p, pk)


def _ret_kernel(lg_ref, q_ref, k_ref, v_ref, g_ref, o_ref, acc_ref, st_ref, dmat_ref, vec_ref, rdec_ref):
    c = RET_CHUNK
    seq = q_ref.shape[1]
    n_chunks = seq // c
    lane_head = lax.broadcasted_iota(jnp.int32, (1, MIX_W), 1) // HEAD_DIM
    row_head = lax.broadcasted_iota(jnp.int32, (MIX_W, 1), 0) // HEAD_DIM
    blockdiag = row_head == lane_head

    def per_head(idx, d):
        out = lg_ref[d, 0]
        for h in range(1, N_HEADS):
            out = jnp.where(idx == h, lg_ref[d, h], out)
        return out

    @pl.when(pl.program_id(0) == 0)
    def _tables():
        ri = lax.broadcasted_iota(jnp.int32, (c, c), 0)
        ci = lax.broadcasted_iota(jnp.int32, (c, c), 1)
        diff = (ri - ci).astype(F32)
        for h in range(N_HEADS):
            fwd = jnp.exp(jnp.where(diff >= 0, diff, 0.0) * lg_ref[0, h])
            bwd = jnp.exp(jnp.where(diff < 0, -diff, 0.0) * lg_ref[1, h])
            dmat_ref[h] = jnp.where(diff >= 0, fwd, bwd)
        pos = lax.broadcasted_iota(jnp.int32, (c, MIX_W), 0).astype(F32)
        lf = per_head(lane_head, 0)
        lb = per_head(lane_head, 1)
        vec_ref[0] = jnp.exp((pos + 1.0) * lf)
        vec_ref[1] = jnp.exp((c - 1.0 - pos) * lf)
        vec_ref[2] = jnp.exp((c - pos) * lb)
        vec_ref[3] = jnp.exp(pos * lb)
        rdec_ref[0] = jnp.broadcast_to(jnp.exp(c * per_head(row_head, 0)), (MIX_W, MIX_W))
        rdec_ref[1] = jnp.broadcast_to(jnp.exp(c * per_head(row_head, 1)), (MIX_W, MIX_W))

    ones_bd = jnp.where(blockdiag, 1.0, 0.0).astype(BF16)

    def chunk(ref, n):
        return ref[0, pl.ds(pl.multiple_of(n * c, c), c), :]

    def fwd_body(n, carry):
        qn, kn, vn = chunk(q_ref, n), chunk(k_ref, n), chunk(v_ref, n)
        acc = _dot((qn.astype(F32) * vec_ref[0]).astype(BF16), st_ref[...].astype(BF16))
        for h in range(N_HEADS):
            hm = lane_head == h
            s = _dot_nt(jnp.where(hm, qn, jnp.zeros_like(qn)), kn)
            p = (s * dmat_ref[h]).astype(BF16)
            acc = acc + _dot(p, jnp.where(hm, vn, jnp.zeros_like(vn)))
        acc_ref[pl.ds(pl.multiple_of(n * c, c), c), :] = acc
        kv = _dot_tn((kn.astype(F32) * vec_ref[1]).astype(BF16), vn)
        st_ref[...] = st_ref[...] * rdec_ref[0] + jnp.where(blockdiag, kv, 0.0)
        return carry

    st_ref[...] = jnp.zeros_like(st_ref)
    lax.fori_loop(0, n_chunks, fwd_body, 0, unroll=2)

    def bwd_body(t, carry):
        n = n_chunks - 1 - t
        qn, kn, vn = chunk(q_ref, n), chunk(k_ref, n), chunk(v_ref, n)
        r0 = pl.multiple_of(n * c, c)
        o = acc_ref[pl.ds(r0, c), :] + _dot((qn.astype(F32) * vec_ref[2]).astype(BF16),
                                           st_ref[...].astype(BF16))
        o2 = o * o
        hi = o2.astype(BF16)
        lo = (o2 - hi.astype(F32)).astype(BF16)
        ms = (_dot(hi, ones_bd) + _dot(lo, ones_bd)) * (1.0 / HEAD_DIM)
        gate = chunk(g_ref, n).astype(F32)
        o_ref[0, pl.ds(r0, c), :] = (gate * _sigmoid(gate) * (o * lax.rsqrt(ms + EPS))).astype(BF16)
        kv = _dot_tn((kn.astype(F32) * vec_ref[3]).astype(BF16), vn)
        st_ref[...] = st_ref[...] * rdec_ref[1] + jnp.where(blockdiag, kv, 0.0)
        return carry

    st_ref[...] = jnp.zeros_like(st_ref)
    lax.fori_loop(0, n_chunks, bwd_body, 0, unroll=2)


def _retention(ret3d, lg):
    b, seq, _ = ret3d.shape
    spec = lambda col: pl.BlockSpec((1, seq, MIX_W), lambda i, col=col: (i, 0, col))
    return pl.pallas_call(
        _ret_kernel,
        out_shape=jax.ShapeDtypeStruct((b, seq, MIX_W), BF16),
        grid=(b,),
        in_specs=[pl.BlockSpec(memory_space=pltpu.SMEM), spec(0), spec(1), spec(2), spec(3)],
        out_specs=pl.BlockSpec((1, seq, MIX_W), lambda i: (i, 0, 0)),
        scratch_shapes=[
            pltpu.VMEM((seq, MIX_W), F32),
            pltpu.VMEM((MIX_W, MIX_W), F32),
            pltpu.VMEM((N_HEADS, RET_CHUNK, RET_CHUNK), F32),
            pltpu.VMEM((4, RET_CHUNK, MIX_W), F32),
            pltpu.VMEM((2, MIX_W, MIX_W), F32),
        ],
        compiler_params=_params("arbitrary"),
        name="retention",
    )(lg, ret3d, ret3d, ret3d, ret3d)


def _fourier_kernel(cs_ref, ss_ref, u_ref, cc_ref, sc_ref, wf_ref, o_ref, *, scale):
    u = u_ref[0]
    z1 = _dot(cs_ref[...], u).astype(BF16)
    z2 = _dot(ss_ref[...], u).astype(BF16)
    f = (_dot(z1, cc_ref[...]) - _dot(z2, sc_ref[...])) * scale
    o_ref[0] = _dot(f.astype(BF16), wf_ref[...]).astype(BF16)


def _fourier(fu3d, cs, ss, cc, sc, wf):
    b, seq, _ = fu3d.shape
    tr = FOUR_TR
    const = lambda i, j: (0, 0)
    return pl.pallas_call(
        functools.partial(_fourier_kernel, scale=1.0 / math.sqrt(seq * HEAD_DIM)),
        out_shape=jax.ShapeDtypeStruct((b, seq, MIX_W), BF16),
        grid=(seq // tr, b),
        in_specs=[
            pl.BlockSpec((tr, seq), lambda i, j: (i, 0)),
            pl.BlockSpec((tr, seq), lambda i, j: (i, 0)),
            pl.BlockSpec((1, seq, MIX_W), lambda i, j: (j, 0, 0)),
            pl.BlockSpec((MIX_W, MIX_W), const),
            pl.BlockSpec((MIX_W, MIX_W), const),
            pl.BlockSpec((MIX_W, MIX_W), const),
        ],
        out_specs=pl.BlockSpec((1, tr, MIX_W), lambda i, j: (j, i, 0)),
        compiler_params=_params("arbitrary", "arbitrary"),
        name="fourier",
    )(cs, ss, fu3d, cc, sc, wf)


def _dil_kernel(q_ref, k_ref, v_ref, o_ref, st_ref, bias_ref, *, sub_len, ts, rb, tq, win):
    j = pl.program_id(2)
    nblk = ts // tq
    lane_head = lax.broadcasted_iota(jnp.int32, (1, MIX_W), 1) // HEAD_DIM
    stat_slot = lax.broadcasted_iota(jnp.int32, (1, LANES), 1) // STAT_W

    @pl.when((pl.program_id(0) == 0) & (pl.program_id(1) == 0) & (j == 0))
    def _bias():
        rel = (lax.broadcasted_iota(jnp.int32, (N_HEADS * tq, win), 0) & (tq - 1)) \
            - lax.broadcasted_iota(jnp.int32, (N_HEADS * tq, win), 1)
        for i in range(3):
            bias_ref[i] = jnp.where(jnp.abs(rel + i * DIL_RADIUS) <= DIL_RADIUS, 0.0, NEG)

    for rr in range(rb):
        for blk in range(nblk):
            q0 = j * ts + blk * tq
            if win == sub_len:
                ws = 0
                bias = bias_ref[blk * tq // DIL_RADIUS]
            else:
                ws = pl.multiple_of(jnp.clip(q0 - DIL_RADIUS, 0, sub_len - win), DIL_RADIUS)
                bias = bias_ref[1] if 0 < blk < nblk - 1 else bias_ref[(q0 - ws) // DIL_RADIUS]
            q = q_ref[0, rr, blk * tq:(blk + 1) * tq, :]
            kw = k_ref[0, rr, pl.ds(ws, win), :]
            vw = v_ref[0, rr, pl.ds(ws, win), :]
            zero = jnp.zeros_like(q)
            qs = jnp.concatenate([jnp.where(lane_head == h, q, zero) for h in range(N_HEADS)], axis=0)
            s = _dot_nt(qs, kw) + bias
            m = jnp.max(s, axis=-1, keepdims=True)
            p = jnp.exp2(s - m)
            den = jnp.sum(p, axis=-1, keepdims=True)
            r = _dot(p.astype(BF16), vw)
            o = jnp.zeros((tq, MIX_W), F32)
            st = jnp.zeros((tq, LANES), F32)
            for h in range(N_HEADS):
                o = jnp.where(lane_head == h, r[h * tq:(h + 1) * tq], o)
                st = jnp.where(stat_slot == 2 * h, m[h * tq:(h + 1) * tq], st)
                st = jnp.where(stat_slot == 2 * h + 1, den[h * tq:(h + 1) * tq], st)
            o_ref[0, rr, blk * tq:(blk + 1) * tq, :] = o.astype(BF16)
            st_ref[0, rr, blk * tq:(blk + 1) * tq, :] = st


def _dilated(dg, group):
    b, dil, sub_len, _ = dg.shape
    tq = min(DIL_TQ, sub_len)
    win = min(tq + 2 * DIL_RADIUS, sub_len)
    ts = min(sub_len, DIL_STEP_ROWS)
    rb = min(dil, DIL_STEP_ROWS // ts)
    part = lambda c: (lambda i, r, j: (i, r, 0, c))
    return pl.pallas_call(
        functools.partial(_dil_kernel, sub_len=sub_len, ts=ts, rb=rb, tq=tq, win=win),
        out_shape=[jax.ShapeDtypeStruct((b, dil, sub_len, MIX_W), BF16),
                   jax.ShapeDtypeStruct((b, dil, sub_len, LANES), F32)],
        grid=(b, dil // rb, sub_len // ts),
        in_specs=[
            pl.BlockSpec((1, rb, ts, MIX_W), lambda i, r, j: (i, r, j, 0)),
            pl.BlockSpec((1, rb, sub_len, MIX_W), part(1)),
            pl.BlockSpec((1, rb, sub_len, MIX_W), part(2)),
        ],
        out_specs=[pl.BlockSpec((1, rb, ts, MIX_W), lambda i, r, j: (i, r, j, 0)),
                   pl.BlockSpec((1, rb, ts, LANES), lambda i, r, j: (i, r, j, 0))],
        scratch_shapes=[pltpu.VMEM((3, N_HEADS * tq, win), F32)],
        compiler_params=_params("arbitrary", "arbitrary", "arbitrary"),
        name=f"dilated{group}",
    )(dg, dg, dg)


def _mla_kernel(q_ref, k_ref, v_ref, o_ref):
    lane_head = lax.broadcasted_iota(jnp.int32, (1, MIX_W), 1) // HEAD_DIM
    v = v_ref[0]
    for b0 in range(0, q_ref.shape[1], MLA_TQ):
        out = jnp.zeros((MLA_TQ, MIX_W), F32)
        for h in range(N_HEADS):
            qh = q_ref[0, b0:b0 + MLA_TQ, h * LANES:(h + 1) * LANES]
            s = _dot_nt(qh, k_ref[0, :, h * LANES:(h + 1) * LANES])
            m = jnp.max(s, axis=-1, keepdims=True)
            p = jnp.exp2(s - m)
            den = jnp.sum(p, axis=-1, keepdims=True)
            o = _dot(p.astype(BF16), v) * (1.0 / den)
            out = jnp.where(lane_head == h, o, out)
        o_ref[0, b0:b0 + MLA_TQ, :] = out.astype(BF16)


def _mla(mq3d, mk3d, mv3d):
    b, seq, _ = mq3d.shape
    ts = max(MLA_TQ, min(seq, MLA_STEP_SCORES // seq))
    return pl.pallas_call(
        _mla_kernel,
        out_shape=jax.ShapeDtypeStruct((b, seq, MIX_W), BF16),
        grid=(b, seq // ts),
        in_specs=[
            pl.BlockSpec((1, ts, N_HEADS * LANES), lambda i, j: (i, j, 0)),
            pl.BlockSpec((1, seq, N_HEADS * LANES), lambda i, j: (i, 0, 0)),
            pl.BlockSpec((1, seq, MIX_W), lambda i, j: (i, 0, 0)),
        ],
        out_specs=pl.BlockSpec((1, ts, MIX_W), lambda i, j: (i, j, 0)),
        compiler_params=_params("arbitrary", "arbitrary"),
        name="mla",
    )(mq3d, mk3d, mv3d)


def _outproj_kernel(x_ref, mod_ref, ro_ref, fo_ref, d0_ref, d1_ref, d2_ref, l0_ref, l1_ref, l2_ref,
                    mo_ref, w_ref, gpm_ref, gpf_ref, x1_ref, h2_ref, so1_ref, sl1_ref, so2_ref, sl2_ref):
    tm = x_ref.shape[0]

    def natural_order(o_ref, l_ref, so_ref, sl_ref):
        dil = o_ref.shape[1]
        if dil == 1:
            return o_ref[0, 0].astype(F32), l_ref[0, 0]
        n = tm // dil
        for r in range(dil):
            o = o_ref[0, r].astype(F32)
            for c in range(MIX_W // LANES):
                so_ref[c, pl.ds(r, n, stride=dil), :] = o[:, c * LANES:(c + 1) * LANES]
            sl_ref[pl.ds(r, n, stride=dil), :] = l_ref[0, r]
        return jnp.concatenate([so_ref[c] for c in range(MIX_W // LANES)], axis=1), sl_ref[...]

    o0, l0 = natural_order(d0_ref, l0_ref, None, None)
    o1, l1 = natural_order(d1_ref, l1_ref, so1_ref, sl1_ref)
    o2, l2 = natural_order(d2_ref, l2_ref, so2_ref, sl2_ref)
    m = jnp.maximum(l0, jnp.maximum(l1, l2))
    e = [jnp.exp2(l - m) for l in (l0, l1, l2)]
    den = sum(eg * pltpu.roll(l, LANES - STAT_W, 1) for eg, l in zip(e, (l0, l1, l2)))
    inv = 1.0 / den
    is_max_lane = (lax.broadcasted_iota(jnp.int32, (1, LANES), 1) & STAT_W) == 0
    src = lax.broadcasted_iota(jnp.int32, (LANES, MIX_W), 0)
    dst = lax.broadcasted_iota(jnp.int32, (LANES, MIX_W), 1)
    spread = jnp.where(src == (dst // HEAD_DIM) * (LANES // N_HEADS), 1.0, 0.0).astype(BF16)
    od = sum(_dot(jnp.where(is_max_lane, eg * inv, 0.0).astype(BF16), spread) * og
             for eg, og in zip(e, (o0, o1, o2))).astype(BF16)
    y = (_dot(ro_ref[...], w_ref[0:256, :]) + _dot(fo_ref[...], w_ref[256:512, :])
         + _dot(od, w_ref[512:768, :]) + _dot(mo_ref[...], w_ref[768:1024, :]))
    g1 = mod_ref[0, 2:3, :]
    sh2 = mod_ref[0, 3:4, :]
    sc2 = mod_ref[0, 4:5, :]
    x1 = x_ref[...] + g1 * _rms(y, gpm_ref[...])
    x1_ref[...] = x1
    h2_ref[...] = (_rms(x1, gpf_ref[...]) * (1.0 + sc2) + sh2).astype(BF16)


def _outproj(x2d, mod, ro, fo, d_o, d_l, mo, w_out, g_post_mix, g_pre_ffn, seq):
    t = x2d.shape[0]
    tm = ROW_TILE
    tps = seq // tm
    row = lambda i: (i, 0)
    const = lambda i: (0, 0)
    mix = pl.BlockSpec((tm, MIX_W), row)
    res = lambda a: pl.BlockSpec((1, a.shape[1], tm // a.shape[1], a.shape[3]), lambda i: (i // tps, 0, i % tps, 0))
    return pl.pallas_call(
        _outproj_kernel,
        out_shape=[jax.ShapeDtypeStruct((t, D_MODEL), F32), jax.ShapeDtypeStruct((t, D_MODEL), BF16)],
        grid=(t // tm,),
        in_specs=[
            pl.BlockSpec((tm, D_MODEL), row),
            pl.BlockSpec((1, 6, D_MODEL), lambda i: (i // tps, 0, 0)),
            mix, mix, res(d_o[0]), res(d_o[1]), res(d_o[2]), res(d_l[0]), res(d_l[1]), res(d_l[2]), mix,
            pl.BlockSpec((D_MODEL, D_MODEL), const),
            pl.BlockSpec((1, D_MODEL), const),
            pl.BlockSpec((1, D_MODEL), const),
        ],
        out_specs=[pl.BlockSpec((tm, D_MODEL), row), pl.BlockSpec((tm, D_MODEL), row)],
        scratch_shapes=[pltpu.VMEM((MIX_W // LANES, tm, LANES), F32), pltpu.VMEM((tm, LANES), F32),
                        pltpu.VMEM((MIX_W // LANES, tm, LANES), F32), pltpu.VMEM((tm, LANES), F32)],
        compiler_params=_params("arbitrary"),
        name="outproj",
    )(x2d, mod, ro, fo, d_o[0], d_o[1], d_o[2], d_l[0], d_l[1], d_l[2], mo, w_out, g_post_mix, g_pre_ffn)


def _ffn_kernel(hp_ref, hc_ref, hn_ref, x1_ref, mod_ref, wu_ref, cw_ref, cb_ref, wd_ref, g_ref, o_ref,
                gate_ref, *, tiles_per_seq):
    tm = hc_ref.shape[0]
    t = pl.program_id(0) % tiles_per_seq
    hp = jnp.where(t == 0, jnp.zeros_like(hp_ref[...]), hp_ref[...])
    hn = jnp.where(t == tiles_per_seq - 1, jnp.zeros_like(hn_ref[...]), hn_ref[...])
    he = jnp.concatenate([hp, hc_ref[...], hn], axis=0)

    ext = tm + 2 * HALO

    def conv(c0):
        u = _dot(he, wu_ref[:, c0:c0 + FFN_CHUNK])
        w = cw_ref[:, c0:c0 + FFN_CHUNK]
        prev = pltpu.roll(u, 1, 0)[HALO:HALO + tm]
        nxt = pltpu.roll(u, ext - 1, 0)[HALO:HALO + tm]
        return prev * w[0:1] + u[HALO:HALO + tm] * w[1:2] + nxt * w[2:3] + cb_ref[:, c0:c0 + FFN_CHUNK]

    for c in range(D_FF // FFN_CHUNK):
        a = conv(c * FFN_CHUNK)
        bu = conv(D_FF + c * FFN_CHUNK)
        gate_ref[:, c * FFN_CHUNK:(c + 1) * FFN_CHUNK] = (a * _sigmoid(a) * bu).astype(BF16)
    acc = _dot(gate_ref[...], wd_ref[...])
    g2 = mod_ref[0, 5:6, :]
    o_ref[...] = x1_ref[...] + g2 * _rms(acc, g_ref[...])


def _ffn(h2, x1, mod, w_up, conv_w, conv_b, w_down, g_post_ffn, seq):
    t = x1.shape[0]
    tm = FFN_TILE
    tps = seq // tm
    hb = tm // HALO
    row = lambda i: (i, 0)
    const = lambda i: (0, 0)
    resident = lambda shape: pl.BlockSpec(shape, const, pipeline_mode=pl.Buffered(1))
    return pl.pallas_call(
        functools.partial(_ffn_kernel, tiles_per_seq=tps),
        out_shape=jax.ShapeDtypeStruct((t, D_MODEL), F32),
        grid=(t // tm,),
        in_specs=[
            pl.BlockSpec((HALO, D_MODEL), lambda i: (jnp.maximum(i * hb - 1, 0), 0)),
            pl.BlockSpec((tm, D_MODEL), row),
            pl.BlockSpec((HALO, D_MODEL), lambda i: (jnp.minimum((i + 1) * hb, t // HALO - 1), 0)),
            pl.BlockSpec((tm, D_MODEL), row),
            pl.BlockSpec((1, 6, D_MODEL), lambda i: (i // tps, 0, 0)),
            resident((D_MODEL, 2 * D_FF)),
            pl.BlockSpec((3, 2 * D_FF), const),
            pl.BlockSpec((1, 2 * D_FF), const),
            resident((D_FF, D_MODEL)),
            pl.BlockSpec((1, D_MODEL), const),
        ],
        out_specs=pl.BlockSpec((tm, D_MODEL), row),
        scratch_shapes=[pltpu.VMEM((tm, D_FF), BF16)],
        compiler_params=_params("arbitrary"),
        name="ffn",
    )(h2, h2, h2, x1, mod, w_up, conv_w, conv_b, w_down, g_post_ffn)


def _rope_tables(seq):
    pos = jnp.arange(seq, dtype=F32)[:, None]
    lane = np.arange(LANES)
    cols = []
    for theta, rot in ((RET_THETA, HEAD_DIM), (ROPE_THETA, PARTIAL_ROT), (ROPE_THETA, MLA_ROPE)):
        half = rot // 2
        inv = jnp.power(theta, -jnp.arange(half, dtype=F32) * 2.0 / rot)
        ang = pos * inv[lane % half][None, :]
        cols += [jnp.cos(ang), jnp.sin(ang)]
    return jnp.concatenate(cols, axis=1)


def _dft_tables(seq):
    n2 = 64
    n1 = seq // n2
    k = np.arange(seq)[:, None]
    a = 2.0 * np.pi * ((k * np.arange(n1)[None, :] * n2) % seq) / seq
    b = 2.0 * np.pi * ((k * np.arange(n2)[None, :]) % seq) / seq
    ca, sa = jnp.asarray(np.cos(a), F32)[:, :, None], jnp.asarray(np.sin(a), F32)[:, :, None]
    cb, sb = jnp.asarray(np.cos(b), F32)[:, None, :], jnp.asarray(np.sin(b), F32)[:, None, :]
    cs = (ca * cb - sa * sb).reshape(seq, seq).astype(BF16)
    ss = (sa * cb + ca * sb).reshape(seq, seq).astype(BF16)
    return cs, ss


def _block_diag(blocks):
    n = len(blocks)
    rows = [jnp.concatenate([blocks[i] if i == j else jnp.zeros_like(blocks[i]) for j in range(n)], axis=1)
            for i in range(n)]
    return jnp.concatenate(rows, axis=0)


def _perm_w_in(w_in):
    ret = w_in[:, 0:1280]
    dq, dk, dv = w_in[:, 1280:2048], w_in[:, 2048:2816], w_in[:, 2816:3584]
    groups = [jnp.concatenate([m[:, g * MIX_W:(g + 1) * MIX_W] for m in (dq, dk, dv)], axis=1)
              for g in range(N_DIL_GROUPS)]
    pad = jnp.zeros((D_MODEL, D_IN_PAD - w_in.shape[1]), w_in.dtype)
    return jnp.concatenate([ret] + groups + [w_in[:, 3584:], pad], axis=1).astype(BF16)


def _mla_weights(w_qb, w_kvb):
    qh = w_qb.reshape(Q_LORA, N_HEADS, MLA_NOPE + MLA_ROPE)
    wq = jnp.pad(qh, ((0, 0), (0, 0), (0, LANES - MLA_NOPE - MLA_ROPE))).reshape(Q_LORA, N_HEADS * LANES)
    kvh = w_kvb.reshape(KV_LORA, N_HEADS, MLA_NOPE + HEAD_DIM)
    wk = jnp.pad(kvh[:, :, :MLA_NOPE], ((0, 0), (0, 0), (0, LANES - MLA_NOPE))).reshape(KV_LORA, N_HEADS * LANES)
    wv = kvh[:, :, MLA_NOPE:].reshape(KV_LORA, MIX_W)
    place = np.zeros((LANES, N_HEADS * LANES), np.float32)
    for h in range(N_HEADS):
        for r in range(MLA_ROPE):
            place[r, h * LANES + MLA_NOPE + r] = 1.0
    return wq.astype(BF16), wk.astype(BF16), wv.astype(BF16), jnp.asarray(place, BF16)


def _trunk(x, mods, layers, shared):
    b, seq, _ = x.shape
    t = b * seq
    x2d = x.reshape(t, D_MODEL)
    tab = _rope_tables(seq)
    cs, ss = _dft_tables(seq)
    for mod, lw in zip(mods, layers):
        ret, fu, dg0, dg1, dg2, mq, mk, mv = _inproj(x2d, mod, lw["g_pre_mix"], lw["w_in"], tab, lw["q_norm"],
                                                     lw["kv_norm"], lw["wq"], lw["wk"], lw["wv"],
                                                     shared["place"], seq)
        ro = _retention(ret.reshape(b, seq, 1024), lw["lg"]).reshape(t, MIX_W)
        fo = _fourier(fu.reshape(b, seq, MIX_W), cs, ss, shared["cc"], shared["sc"], lw["wf"]).reshape(t, MIX_W)
        d_o, d_l = zip(*[_dilated(dg, g) for g, dg in enumerate((dg0, dg1, dg2))])
        mo = _mla(mq.reshape(b, seq, 512), mk.reshape(b, seq, 512), mv.reshape(b, seq, MIX_W)).reshape(t, MIX_W)
        x1, h2 = _outproj(x2d, mod, ro, fo, d_o, d_l, mo, lw["w_out"], lw["g_post_mix"], lw["g_pre_ffn"], seq)
        x2d = _ffn(h2, x1, mod, lw["w_up"], lw["conv_w"], lw["conv_b"], lw["w_down"], lw["g_post_ffn"], seq)
    return x2d.reshape(b, seq, D_MODEL)


def kernel(x_prompt, x_sample, c_prompt, c_sample, w_ada, b_ada, norm_pre_mix, w_in, ret_decay_fwd,
           ret_decay_bwd, w_fmix, mla_q_norm, mla_w_qb, mla_kv_norm, mla_w_kvb, w_out, norm_post_mix,
           norm_pre_ffn, w_up, conv_w, conv_b, w_down, norm_post_ffn):
    depth = w_in.shape[0]
    nb_p, nb_s = c_prompt.shape[0], c_sample.shape[0]
    rows = -(-(nb_p + nb_s) // 8) * 8
    c_all = jnp.concatenate([c_prompt, c_sample, jnp.zeros((rows - nb_p - nb_s, D_MODEL), F32)], axis=0)
    mod_all = _ada(c_all, w_ada, b_ada)
    mods_p = [mod_all[l, :nb_p].reshape(nb_p, 6, D_MODEL) for l in range(depth)]
    mods_s = [mod_all[l, nb_p:nb_p + nb_s].reshape(nb_s, 6, D_MODEL) for l in range(depth)]

    c64 = 2.0 * np.pi * np.outer(np.arange(HEAD_DIM), np.arange(HEAD_DIM)) / HEAD_DIM
    shared = {
        "cc": _block_diag([jnp.asarray(np.cos(c64), BF16)] * N_HEADS),
        "sc": _block_diag([jnp.asarray(np.sin(c64), BF16)] * N_HEADS),
    }
    layers = []
    for l in range(depth):
        wq, wk, wv, place = _mla_weights(mla_w_qb[l], mla_w_kvb[l])
        shared["place"] = place
        layers.append({
            "g_pre_mix": norm_pre_mix[l][None, :],
            "w_in": _perm_w_in(w_in[l]),
            "lg": jnp.stack([jax.nn.log_sigmoid(ret_decay_fwd[l]), jax.nn.log_sigmoid(ret_decay_bwd[l])]),
            "wf": _block_diag([w_fmix[l, g] for g in range(N_HEADS)]).astype(BF16),
            "q_norm": mla_q_norm[l][None, :],
            "kv_norm": mla_kv_norm[l][None, :],
            "wq": wq, "wk": wk, "wv": wv,
            "w_out": w_out[l].astype(BF16),
            "g_post_mix": norm_post_mix[l][None, :],
            "g_pre_ffn": norm_pre_ffn[l][None, :],
            "w_up": w_up[l].astype(BF16),
            "conv_w": conv_w[l],
            "conv_b": conv_b[l][None, :],
            "w_down": w_down[l].astype(BF16),
            "g_post_ffn": norm_post_ffn[l][None, :],
        })
    y_prompt = _trunk(x_prompt, mods_p, layers, shared)
    y_sample = _trunk(x_sample, mods_s, layers, shared)
    return (y_prompt, y_sample)
```

```python
import functools
import math

import numpy as np
import jax
import jax.numpy as jnp
from jax import lax
from jax.experimental import pallas as pl
from jax.experimental.pallas import tpu as pltpu

F32 = jnp.float32
BF16 = jnp.bfloat16

D_MODEL = 1024
HEAD_DIM = 64
N_HEADS = 4
MIX_W = N_HEADS * HEAD_DIM
DIL_PAIRS = ((128, 1), (512, 4), (2048, 16))
N_DIL_GROUPS = 3
DIL_RADIUS = 64
STAT_W = 16
MLA_NOPE = 64
MLA_ROPE = 32
Q_LORA = 256
KV_LORA = 128
D_FF = 2816
ROPE_THETA = 500000.0
RET_THETA = 10000.0
PARTIAL_ROT = HEAD_DIM // 4
EPS = 1e-6
NEG = -1e30
LOG2E = math.log2(math.e)

LANES = 128
D_IN_PAD = 4096
MLA_OFF = 3584
ROW_TILE = 512
RET_CHUNK = 256
DIL_TQ = 128
DIL_STEP_ROWS = 1024
FOUR_TR = 512
MLA_TQ = 256
MLA_STEP_SCORES = 2 ** 21
FFN_CHUNK = 256
HALO = 16
VMEM_LIMIT = 56 * 1024 * 1024


def _params(*sem):
    return pltpu.CompilerParams(dimension_semantics=sem, vmem_limit_bytes=VMEM_LIMIT)


def _rms(x, g):
    return x * lax.rsqrt(jnp.mean(x * x, axis=-1, keepdims=True) + EPS) * g


def _sigmoid(x):
    return 1.0 / (1.0 + jnp.exp(-x))


def _dot(a, b):
    return jnp.dot(a, b, preferred_element_type=F32)


def _dot_nt(a, b):
    return lax.dot_general(a, b, (((1,), (1,)), ((), ())), preferred_element_type=F32)


def _dot_tn(a, b):
    return lax.dot_general(a, b, (((0,), (0,)), ((), ())), preferred_element_type=F32)


def _ada_kernel(c_ref, w_ref, b_ref, o_ref):
    c = c_ref[...]
    cond = (c * _sigmoid(c)).astype(BF16)
    o_ref[0] = _dot(cond, w_ref[0].astype(BF16)) + b_ref[0]


def _ada(c_all, w_ada, b_ada):
    depth, _, n = w_ada.shape
    rows = c_all.shape[0]
    tn = 1536
    return pl.pallas_call(
        _ada_kernel,
        out_shape=jax.ShapeDtypeStruct((depth, rows, n), F32),
        grid=(depth, n // tn),
        in_specs=[
            pl.BlockSpec((rows, D_MODEL), lambda l, j: (0, 0)),
            pl.BlockSpec((1, D_MODEL, tn), lambda l, j: (l, 0, j)),
            pl.BlockSpec((1, 1, tn), lambda l, j: (l, 0, j)),
        ],
        out_specs=pl.BlockSpec((1, rows, tn), lambda l, j: (l, 0, j)),
        compiler_params=_params("arbitrary", "arbitrary"),
        name="ada",
    )(c_all, w_ada, b_ada.reshape(depth, 1, n))


def _inproj_kernel(x_ref, mod_ref, g_ref, w_ref, tab_ref, qn_ref, kvn_ref, wq_ref, wk_ref, wv_ref,
                   pk_ref, ret_ref, fu_ref, d0_ref, d1_ref, d2_ref, mq_ref, mk_ref, mv_ref, scr_ref, scr2_ref):
    x = x_ref[...]
    sh = mod_ref[0, 0:1, :]
    sc = mod_ref[0, 1:2, :]
    hb = (_rms(x, g_ref[...]) * (1.0 + sc) + sh).astype(BF16)

    def mm(c0, c1):
        return _dot(hb, w_ref[:, c0:c1])

    lane = lax.broadcasted_iota(jnp.int32, (1, LANES), 1)
    j64 = lane & (HEAD_DIM - 1)

    def make_rope(cos, sin, lo_mask, hi_mask, half):
        c = jnp.where(lo_mask | hi_mask, cos, 1.0)
        sa = jnp.where(lo_mask, -sin, 0.0)
        sb = jnp.where(hi_mask, sin, 0.0)

        def apply(z):
            return z * c + pltpu.roll(z, LANES - half, 1) * sa + pltpu.roll(z, half, 1) * sb
        return apply

    rope_ret = make_rope(tab_ref[:, 0:128], tab_ref[:, 128:256], j64 < 32, j64 >= 32, 32)
    rope_dil = make_rope(tab_ref[:, 256:384], tab_ref[:, 384:512], j64 < 8, (j64 >= 8) & (j64 < 16), 8)
    cos_m = tab_ref[:, 512:640]
    sin_m = tab_ref[:, 640:768]
    rope_kr = make_rope(cos_m, sin_m, lane < 16, (lane >= 16) & (lane < 32), 16)
    rope_mq = make_rope(cos_m, sin_m, (lane >= 64) & (lane < 80), (lane >= 80) & (lane < 96), 16)

    z = mm(MLA_OFF, D_IN_PAD)
    cqn = _rms(z[:, 0:Q_LORA], qn_ref[...]).astype(BF16)
    q = _dot(cqn, wq_ref[...])
    scale = (MLA_NOPE + MLA_ROPE) ** -0.5 * LOG2E
    for h in range(N_HEADS):
        r = rope_mq(q[:, h * LANES:(h + 1) * LANES]) * scale
        mq_ref[:, h * LANES:(h + 1) * LANES] = r.astype(BF16)
    ckvn = _rms(z[:, Q_LORA:Q_LORA + KV_LORA], kvn_ref[...]).astype(BF16)
    kr = rope_kr(z[:, 384:512]).astype(BF16)
    mk_ref[...] = (_dot(ckvn, wk_ref[...]) + _dot(kr, pk_ref[...])).astype(BF16)
    mv_ref[...] = _dot(ckvn, wv_ref[...]).astype(BF16)
    z = mm(0, 512)
    for c in range(4):
        r = rope_ret(z[:, c * LANES:(c + 1) * LANES])
        if c >= 2:
            r = r * (HEAD_DIM ** -0.5)
        ret_ref[:, c * LANES:(c + 1) * LANES] = r.astype(BF16)
    ret_ref[:, 512:1024] = mm(512, 1024).astype(BF16)
    fu_ref[...] = mm(1024, 1280).astype(BF16)
    tm = x.shape[0]
    nslab = 3 * MIX_W // LANES
    for g, d_ref in enumerate((d0_ref, d1_ref, d2_ref)):
        dil = DIL_PAIRS[g][1]
        base = g * 3 * MIX_W
        z = mm(1280 + base, 1280 + base + 3 * MIX_W)
        slabs = []
        for c in range(nslab):
            r = z[:, c * LANES:(c + 1) * LANES]
            if c < 4:
                r = rope_dil(r)
            if c < 2:
                r = r * (HEAD_DIM ** -0.5 * LOG2E)
            slabs.append(r)
        if dil == 1:
            d_ref[0, 0] = jnp.concatenate(slabs, axis=1).astype(BF16)
            continue
        for c in range(nslab):
            scr_ref[c] = slabs[c]
        n4 = tm // 4

        def rows4(ref, start, count):
            return jnp.concatenate([ref[c, pl.ds(start, count, stride=4), :] for c in range(nslab)], axis=1)

        if dil == 4:
            for r4 in range(4):
                d_ref[0, r4] = rows4(scr_ref, r4, n4).astype(BF16)
        else:
            for r4 in range(4):
                for c in range(nslab):
                    scr2_ref[c, r4 * n4:(r4 + 1) * n4, :] = scr_ref[c, pl.ds(r4, n4, stride=4), :]
            for r4 in range(4):
                for q4 in range(4):
                    d_ref[0, r4 + 4 * q4] = rows4(scr2_ref, r4 * n4 + q4, tm // 16).astype(BF16)


def _inproj(x2d, mod, g_pre, w_in_p, tab, qn, kvn, wq_p, wk_p, wv_p, pk, seq):
    t = x2d.shape[0]
    b = t // seq
    tm = ROW_TILE
    tps = seq // tm
    const = lambda i: (0, 0)
    row = lambda i: (i, 0)
    flat = [(t, 1024), (t, MIX_W)]
    flat2 = [(t, 512), (t, 512), (t, MIX_W)]
    dils = [d for _, d in DIL_PAIRS]
    assert dils == [1, 4, 16]
    bf = lambda shp: jax.ShapeDtypeStruct(shp, BF16)
    dil_shapes = [bf((b, d, seq // d, 3 * MIX_W)) for d in dils]
    dil_specs = [pl.BlockSpec((1, d, tm // d, 3 * MIX_W), lambda i: (i // tps, 0, i % tps, 0)) for d in dils]
    return pl.pallas_call(
        _inproj_kernel,
        out_shape=[bf(s) for s in flat] + dil_shapes + [bf(s) for s in flat2],
        grid=(t // tm,),
        in_specs=[
            pl.BlockSpec((tm, D_MODEL), row),
            pl.BlockSpec((1, 6, D_MODEL), lambda i: (i // tps, 0, 0)),
            pl.BlockSpec((1, D_MODEL), const),
            pl.BlockSpec((D_MODEL, D_IN_PAD), const),
            pl.BlockSpec((tm, 6 * LANES), lambda i: (i % tps, 0)),
            pl.BlockSpec((1, Q_LORA), const),
            pl.BlockSpec((1, KV_LORA), const),
            pl.BlockSpec((Q_LORA, 512), const),
            pl.BlockSpec((KV_LORA, 512), const),
            pl.BlockSpec((KV_LORA, MIX_W), const),
            pl.BlockSpec((LANES, 512), const),
        ],
        out_specs=[pl.BlockSpec((tm, s[1]), row) for s in flat] + dil_specs
                  + [pl.BlockSpec((tm, s[1]), row) for s in flat2],
        scratch_shapes=[pltpu.VMEM((3 * MIX_W // LANES, tm, LANES), F32),
                        pltpu.VMEM((3 * MIX_W // LANES, tm, LANES), F32)],
        compiler_params=_params("arbitrary"),
        name="inproj",
    )(x2d, mod, g_pre, w_in_p, tab, qn, kvn, wq_p, wk_p, wv_p, pk)


def _ret_kernel(lg_ref, q_ref, k_ref, v_ref, g_ref, o_ref, acc_ref, st_ref, dmat_ref, vec_ref, rdec_ref):
    c = RET_CHUNK
    seq = q_ref.shape[1]
    n_chunks = seq // c
    lane_head = lax.broadcasted_iota(jnp.int32, (1, MIX_W), 1) // HEAD_DIM
    row_head = lax.broadcasted_iota(jnp.int32, (MIX_W, 1), 0) // HEAD_DIM
    blockdiag = row_head == lane_head

    def per_head(idx, d):
        out = lg_ref[d, 0]
        for h in range(1, N_HEADS):
            out = jnp.where(idx == h, lg_ref[d, h], out)
        return out

    @pl.when(pl.program_id(0) == 0)
    def _tables():
        ri = lax.broadcasted_iota(jnp.int32, (c, c), 0)
        ci = lax.broadcasted_iota(jnp.int32, (c, c), 1)
        diff = (ri - ci).astype(F32)
        for h in range(N_HEADS):
            fwd = jnp.exp(jnp.where(diff >= 0, diff, 0.0) * lg_ref[0, h])
            bwd = jnp.exp(jnp.where(diff < 0, -diff, 0.0) * lg_ref[1, h])
            dmat_ref[h] = jnp.where(diff >= 0, fwd, bwd)
        pos = lax.broadcasted_iota(jnp.int32, (c, MIX_W), 0).astype(F32)
        lf = per_head(lane_head, 0)
        lb = per_head(lane_head, 1)
        vec_ref[0] = jnp.exp((pos + 1.0) * lf)
        vec_ref[1] = jnp.exp((c - 1.0 - pos) * lf)
        vec_ref[2] = jnp.exp((c - pos) * lb)
        vec_ref[3] = jnp.exp(pos * lb)
        rdec_ref[0] = jnp.broadcast_to(jnp.exp(c * per_head(row_head, 0)), (MIX_W, MIX_W))
        rdec_ref[1] = jnp.broadcast_to(jnp.exp(c * per_head(row_head, 1)), (MIX_W, MIX_W))

    ones_bd = jnp.where(blockdiag, 1.0, 0.0).astype(BF16)

    def chunk(ref, n):
        return ref[0, pl.ds(pl.multiple_of(n * c, c), c), :]

    def fwd_body(n, carry):
        qn, kn, vn = chunk(q_ref, n), chunk(k_ref, n), chunk(v_ref, n)
        acc = _dot((qn.astype(F32) * vec_ref[0]).astype(BF16), st_ref[...].astype(BF16))
        for h in range(N_HEADS):
            hm = lane_head == h
            s = _dot_nt(jnp.where(hm, qn, jnp.zeros_like(qn)), kn)
            p = (s * dmat_ref[h]).astype(BF16)
            acc = acc + _dot(p, jnp.where(hm, vn, jnp.zeros_like(vn)))
        acc_ref[pl.ds(pl.multiple_of(n * c, c), c), :] = acc
        kv = _dot_tn((kn.astype(F32) * vec_ref[1]).astype(BF16), vn)
        st_ref[...] = st_ref[...] * rdec_ref[0] + jnp.where(blockdiag, kv, 0.0)
        return carry

    st_ref[...] = jnp.zeros_like(st_ref)
    lax.fori_loop(0, n_chunks, fwd_body, 0, unroll=2)

    def bwd_body(t, carry):
        n = n_chunks - 1 - t
        qn, kn, vn = chunk(q_ref, n), chunk(k_ref, n), chunk(v_ref, n)
        r0 = pl.multiple_of(n * c, c)
        o = acc_ref[pl.ds(r0, c), :] + _dot((qn.astype(F32) * vec_ref[2]).astype(BF16),
                                           st_ref[...].astype(BF16))
        o2 = o * o
        hi = o2.astype(BF16)
        lo = (o2 - hi.astype(F32)).astype(BF16)
        ms = (_dot(hi, ones_bd) + _dot(lo, ones_bd)) * (1.0 / HEAD_DIM)
        gate = chunk(g_ref, n).astype(F32)
        o_ref[0, pl.ds(r0, c), :] = (gate * _sigmoid(gate) * (o * lax.rsqrt(ms + EPS))).astype(BF16)
        kv = _dot_tn((kn.astype(F32) * vec_ref[3]).astype(BF16), vn)
        st_ref[...] = st_ref[...] * rdec_ref[1] + jnp.where(blockdiag, kv, 0.0)
        return carry

    st_ref[...] = jnp.zeros_like(st_ref)
    lax.fori_loop(0, n_chunks, bwd_body, 0, unroll=2)


def _retention(ret3d, lg):
    b, seq, _ = ret3d.shape
    spec = lambda col: pl.BlockSpec((1, seq, MIX_W), lambda i, col=col: (i, 0, col))
    return pl.pallas_call(
        _ret_kernel,
        out_shape=jax.ShapeDtypeStruct((b, seq, MIX_W), BF16),
        grid=(b,),
        in_specs=[pl.BlockSpec(memory_space=pltpu.SMEM), spec(0), spec(1), spec(2), spec(3)],
        out_specs=pl.BlockSpec((1, seq, MIX_W), lambda i: (i, 0, 0)),
        scratch_shapes=[
            pltpu.VMEM((seq, MIX_W), F32),
            pltpu.VMEM((MIX_W, MIX_W), F32),
            pltpu.VMEM((N_HEADS, RET_CHUNK, RET_CHUNK), F32),
            pltpu.VMEM((4, RET_CHUNK, MIX_W), F32),
            pltpu.VMEM((2, MIX_W, MIX_W), F32),
        ],
        compiler_params=_params("arbitrary"),
        name="retention",
    )(lg, ret3d, ret3d, ret3d, ret3d)


def _fourier_kernel(cs_ref, ss_ref, u_ref, cc_ref, sc_ref, wf_ref, o_ref, *, scale):
    u = u_ref[0]
    z1 = _dot(cs_ref[...], u).astype(BF16)
    z2 = _dot(ss_ref[...], u).astype(BF16)
    f = (_dot(z1, cc_ref[...]) - _dot(z2, sc_ref[...])) * scale
    o_ref[0] = _dot(f.astype(BF16), wf_ref[...]).astype(BF16)


def _fourier(fu3d, cs, ss, cc, sc, wf):
    b, seq, _ = fu3d.shape
    tr = FOUR_TR
    const = lambda i, j: (0, 0)
    return pl.pallas_call(
        functools.partial(_fourier_kernel, scale=1.0 / math.sqrt(seq * HEAD_DIM)),
        out_shape=jax.ShapeDtypeStruct((b, seq, MIX_W), BF16),
        grid=(seq // tr, b),
        in_specs=[
            pl.BlockSpec((tr, seq), lambda i, j: (i, 0)),
            pl.BlockSpec((tr, seq), lambda i, j: (i, 0)),
            pl.BlockSpec((1, seq, MIX_W), lambda i, j: (j, 0, 0)),
            pl.BlockSpec((MIX_W, MIX_W), const),
            pl.BlockSpec((MIX_W, MIX_W), const),
            pl.BlockSpec((MIX_W, MIX_W), const),
        ],
        out_specs=pl.BlockSpec((1, tr, MIX_W), lambda i, j: (j, i, 0)),
        compiler_params=_params("arbitrary", "arbitrary"),
        name="fourier",
    )(cs, ss, fu3d, cc, sc, wf)


def _dil_kernel(q_ref, k_ref, v_ref, o_ref, st_ref, bias_ref, *, sub_len, ts, rb, tq, win):
    j = pl.program_id(2)
    nblk = ts // tq
    lane_head = lax.broadcasted_iota(jnp.int32, (1, MIX_W), 1) // HEAD_DIM
    stat_slot = lax.broadcasted_iota(jnp.int32, (1, LANES), 1) // STAT_W

    @pl.when((pl.program_id(0) == 0) & (pl.program_id(1) == 0) & (j == 0))
    def _bias():
        rel = (lax.broadcasted_iota(jnp.int32, (N_HEADS * tq, win), 0) & (tq - 1)) \
            - lax.broadcasted_iota(jnp.int32, (N_HEADS * tq, win), 1)
        for i in range(3):
            bias_ref[i] = jnp.where(jnp.abs(rel + i * DIL_RADIUS) <= DIL_RADIUS, 0.0, NEG)

    for rr in range(rb):
        for blk in range(nblk):
            q0 = j * ts + blk * tq
            if win == sub_len:
                ws = 0
                bias = bias_ref[blk * tq // DIL_RADIUS]
            else:
                ws = pl.multiple_of(jnp.clip(q0 - DIL_RADIUS, 0, sub_len - win), DIL_RADIUS)
                bias = bias_ref[1] if 0 < blk < nblk - 1 else bias_ref[(q0 - ws) // DIL_RADIUS]
            q = q_ref[0, rr, blk * tq:(blk + 1) * tq, :]
            kw = k_ref[0, rr, pl.ds(ws, win), :]
            vw = v_ref[0, rr, pl.ds(ws, win), :]
            zero = jnp.zeros_like(q)
            qs = jnp.concatenate([jnp.where(lane_head == h, q, zero) for h in range(N_HEADS)], axis=0)
            s = _dot_nt(qs, kw) + bias
            m = jnp.max(s, axis=-1, keepdims=True)
            p = jnp.exp2(s - m)
            den = jnp.sum(p, axis=-1, keepdims=True)
            r = _dot(p.astype(BF16), vw)
            o = jnp.zeros((tq, MIX_W), F32)
            st = jnp.zeros((tq, LANES), F32)
            for h in range(N_HEADS):
                o = jnp.where(lane_head == h, r[h * tq:(h + 1) * tq], o)
                st = jnp.where(stat_slot == 2 * h, m[h * tq:(h + 1) * tq], st)
                st = jnp.where(stat_slot == 2 * h + 1, den[h * tq:(h + 1) * tq], st)
            o_ref[0, rr, blk * tq:(blk + 1) * tq, :] = o.astype(BF16)
            st_ref[0, rr, blk * tq:(blk + 1) * tq, :] = st


def _dilated(dg, group):
    b, dil, sub_len, _ = dg.shape
    tq = min(DIL_TQ, sub_len)
    win = min(tq + 2 * DIL_RADIUS, sub_len)
    ts = min(sub_len, DIL_STEP_ROWS)
    rb = min(dil, DIL_STEP_ROWS // ts)
    part = lambda c: (lambda i, r, j: (i, r, 0, c))
    return pl.pallas_call(
        functools.partial(_dil_kernel, sub_len=sub_len, ts=ts, rb=rb, tq=tq, win=win),
        out_shape=[jax.ShapeDtypeStruct((b, dil, sub_len, MIX_W), BF16),
                   jax.ShapeDtypeStruct((b, dil, sub_len, LANES), F32)],
        grid=(b, dil // rb, sub_len // ts),
        in_specs=[
            pl.BlockSpec((1, rb, ts, MIX_W), lambda i, r, j: (i, r, j, 0)),
            pl.BlockSpec((1, rb, sub_len, MIX_W), part(1)),
            pl.BlockSpec((1, rb, sub_len, MIX_W), part(2)),
        ],
        out_specs=[pl.BlockSpec((1, rb, ts, MIX_W), lambda i, r, j: (i, r, j, 0)),
                   pl.BlockSpec((1, rb, ts, LANES), lambda i, r, j: (i, r, j, 0))],
        scratch_shapes=[pltpu.VMEM((3, N_HEADS * tq, win), F32)],
        compiler_params=_params("arbitrary", "arbitrary", "arbitrary"),
        name=f"dilated{group}",
    )(dg, dg, dg)


def _mla_kernel(q_ref, k_ref, v_ref, o_ref):
    lane_head = lax.broadcasted_iota(jnp.int32, (1, MIX_W), 1) // HEAD_DIM
    v = v_ref[0]
    for b0 in range(0, q_ref.shape[1], MLA_TQ):
        out = jnp.zeros((MLA_TQ, MIX_W), F32)
        for h in range(N_HEADS):
            qh = q_ref[0, b0:b0 + MLA_TQ, h * LANES:(h + 1) * LANES]
            s = _dot_nt(qh, k_ref[0, :, h * LANES:(h + 1) * LANES])
            m = jnp.max(s, axis=-1, keepdims=True)
            p = jnp.exp2(s - m)
            den = jnp.sum(p, axis=-1, keepdims=True)
            o = _dot(p.astype(BF16), v) * (1.0 / den)
            out = jnp.where(lane_head == h, o, out)
        o_ref[0, b0:b0 + MLA_TQ, :] = out.astype(BF16)


def _mla(mq3d, mk3d, mv3d):
    b, seq, _ = mq3d.shape
    ts = max(MLA_TQ, min(seq, MLA_STEP_SCORES // seq))
    return pl.pallas_call(
        _mla_kernel,
        out_shape=jax.ShapeDtypeStruct((b, seq, MIX_W), BF16),
        grid=(b, seq // ts),
        in_specs=[
            pl.BlockSpec((1, ts, N_HEADS * LANES), lambda i, j: (i, j, 0)),
            pl.BlockSpec((1, seq, N_HEADS * LANES), lambda i, j: (i, 0, 0)),
            pl.BlockSpec((1, seq, MIX_W), lambda i, j: (i, 0, 0)),
        ],
        out_specs=pl.BlockSpec((1, ts, MIX_W), lambda i, j: (i, j, 0)),
        compiler_params=_params("arbitrary", "arbitrary"),
        name="mla",
    )(mq3d, mk3d, mv3d)


def _mixffn_kernel(x_ref, moda_ref, ro_ref, fo_ref, d0_ref, d1_ref, d2_ref, l0_ref, l1_ref, l2_ref, mo_ref,
                   wo_ref, gpm_ref, gpf_ref, modb_ref, wu_ref, cw_ref, cb_ref, wd_ref, gff_ref, o_ref,
                   so1_ref, sl1_ref, so2_ref, sl2_ref, x1c_ref, h2c_ref, hp_ref, gate_ref, *, tiles_per_seq):
    i = pl.program_id(0)
    tm = x_ref.shape[0]
    slot_new = i % 3
    slot_next = (i + 2) % 3
    slot_mlp = (i + 1) % 3

    @pl.when(i == 0)
    def _init():
        x1c_ref[...] = jnp.zeros_like(x1c_ref)
        h2c_ref[...] = jnp.zeros_like(h2c_ref)
        hp_ref[...] = jnp.zeros_like(hp_ref)

    def natural_order(o_ref, l_ref, so_ref, sl_ref):
        dil = o_ref.shape[1]
        if dil == 1:
            return o_ref[0, 0].astype(F32), l_ref[0, 0]
        n = tm // dil
        for r in range(dil):
            o = o_ref[0, r].astype(F32)
            for c in range(MIX_W // LANES):
                so_ref[c, pl.ds(r, n, stride=dil), :] = o[:, c * LANES:(c + 1) * LANES]
            sl_ref[pl.ds(r, n, stride=dil), :] = l_ref[0, r]
        return jnp.concatenate([so_ref[c] for c in range(MIX_W // LANES)], axis=1), sl_ref[...]

    o0, l0 = natural_order(d0_ref, l0_ref, None, None)
    o1, l1 = natural_order(d1_ref, l1_ref, so1_ref, sl1_ref)
    o2, l2 = natural_order(d2_ref, l2_ref, so2_ref, sl2_ref)
    m = jnp.maximum(l0, jnp.maximum(l1, l2))
    e = [jnp.exp2(l - m) for l in (l0, l1, l2)]
    den = sum(eg * pltpu.roll(l, LANES - STAT_W, 1) for eg, l in zip(e, (l0, l1, l2)))
    inv = 1.0 / den
    is_max_lane = (lax.broadcasted_iota(jnp.int32, (1, LANES), 1) & STAT_W) == 0
    src = lax.broadcasted_iota(jnp.int32, (LANES, MIX_W), 0)
    dst = lax.broadcasted_iota(jnp.int32, (LANES, MIX_W), 1)
    spread = jnp.where(src == (dst // HEAD_DIM) * (LANES // N_HEADS), 1.0, 0.0).astype(BF16)
    od = sum(_dot(jnp.where(is_max_lane, eg * inv, 0.0).astype(BF16), spread) * og
             for eg, og in zip(e, (o0, o1, o2))).astype(BF16)
    y = (_dot(ro_ref[...], wo_ref[0:256, :]) + _dot(fo_ref[...], wo_ref[256:512, :])
         + _dot(od, wo_ref[512:768, :]) + _dot(mo_ref[...], wo_ref[768:1024, :]))
    g1 = moda_ref[0, 2:3, :]
    sh2 = moda_ref[0, 3:4, :]
    sc2 = moda_ref[0, 4:5, :]
    x1 = x_ref[...] + g1 * _rms(y, gpm_ref[...])
    h2 = (_rms(x1, gpf_ref[...]) * (1.0 + sc2) + sh2).astype(BF16)

    ts = (i + 2 * tiles_per_seq - 2) % tiles_per_seq
    hp = jnp.where(ts == 0, jnp.zeros_like(hp_ref[...]), hp_ref[...])
    hn = h2c_ref[slot_next, 0:HALO, :]
    hn = jnp.where(ts == tiles_per_seq - 1, jnp.zeros_like(hn), hn)
    hc = h2c_ref[slot_mlp]
    he = jnp.concatenate([hp, hc, hn], axis=0)
    ext = tm + 2 * HALO

    def conv(c0):
        u = _dot(he, wu_ref[:, c0:c0 + FFN_CHUNK])
        w = cw_ref[:, c0:c0 + FFN_CHUNK]
        prev = pltpu.roll(u, 1, 0)[HALO:HALO + tm]
        nxt = pltpu.roll(u, ext - 1, 0)[HALO:HALO + tm]
        return prev * w[0:1] + u[HALO:HALO + tm] * w[1:2] + nxt * w[2:3] + cb_ref[:, c0:c0 + FFN_CHUNK]

    for c in range(D_FF // FFN_CHUNK):
        a = conv(c * FFN_CHUNK)
        bu = conv(D_FF + c * FFN_CHUNK)
        gate_ref[:, c * FFN_CHUNK:(c + 1) * FFN_CHUNK] = (a * _sigmoid(a) * bu).astype(BF16)
    acc = _dot(gate_ref[...], wd_ref[...])
    g2 = modb_ref[0, 5:6, :]
    o_ref[...] = x1c_ref[slot_mlp] + g2 * _rms(acc, gff_ref[...])

    hp_ref[...] = hc[tm - HALO:tm]
    x1c_ref[slot_new] = x1
    h2c_ref[slot_new] = h2


def _mixffn(x2d, mod, ro, fo, d_o, d_l, mo, lw, seq):
    t = x2d.shape[0]
    tm = ROW_TILE
    tps = seq // tm
    nt = t // tm
    cl = lambda i: jnp.minimum(i, nt - 1)
    pv = lambda i: jnp.maximum(i - 2, 0)
    row = lambda i: (cl(i), 0)
    const = lambda i: (0, 0)
    resident = lambda shape: pl.BlockSpec(shape, const, pipeline_mode=pl.Buffered(1))
    mix = pl.BlockSpec((tm, MIX_W), row)
    res = lambda a: pl.BlockSpec((1, a.shape[1], tm // a.shape[1], a.shape[3]),
                                 lambda i: (cl(i) // tps, 0, cl(i) % tps, 0))
    return pl.pallas_call(
        functools.partial(_mixffn_kernel, tiles_per_seq=tps),
        out_shape=jax.ShapeDtypeStruct((t, D_MODEL), F32),
        grid=(nt + 2,),
        in_specs=[
            pl.BlockSpec((tm, D_MODEL), row),
            pl.BlockSpec((1, 6, D_MODEL), lambda i: (cl(i) // tps, 0, 0)),
            mix, mix, res(d_o[0]), res(d_o[1]), res(d_o[2]), res(d_l[0]), res(d_l[1]), res(d_l[2]), mix,
            resident((D_MODEL, D_MODEL)),
            pl.BlockSpec((1, D_MODEL), const),
            pl.BlockSpec((1, D_MODEL), const),
            pl.BlockSpec((1, 6, D_MODEL), lambda i: (pv(i) // tps, 0, 0)),
            resident((D_MODEL, 2 * D_FF)),
            pl.BlockSpec((3, 2 * D_FF), const),
            pl.BlockSpec((1, 2 * D_FF), const),
            resident((D_FF, D_MODEL)),
            pl.BlockSpec((1, D_MODEL), const),
        ],
        out_specs=pl.BlockSpec((tm, D_MODEL), lambda i: (pv(i), 0)),
        scratch_shapes=[pltpu.VMEM((MIX_W // LANES, tm, LANES), F32), pltpu.VMEM((tm, LANES), F32),
                        pltpu.VMEM((MIX_W // LANES, tm, LANES), F32), pltpu.VMEM((tm, LANES), F32),
                        pltpu.VMEM((3, tm, D_MODEL), F32), pltpu.VMEM((3, tm, D_MODEL), BF16),
                        pltpu.VMEM((HALO, D_MODEL), BF16), pltpu.VMEM((tm, D_FF), BF16)],
        compiler_params=_params("arbitrary"),
        name="mixffn",
    )(x2d, mod, ro, fo, d_o[0], d_o[1], d_o[2], d_l[0], d_l[1], d_l[2], mo, lw["w_out"], lw["g_post_mix"],
      lw["g_pre_ffn"], mod, lw["w_up"], lw["conv_w"], lw["conv_b"], lw["w_down"], lw["g_post_ffn"])


def _rope_tables(seq):
    pos = jnp.arange(seq, dtype=F32)[:, None]
    lane = np.arange(LANES)
    cols = []
    for theta, rot in ((RET_THETA, HEAD_DIM), (ROPE_THETA, PARTIAL_ROT), (ROPE_THETA, MLA_ROPE)):
        half = rot // 2
        inv = jnp.power(theta, -jnp.arange(half, dtype=F32) * 2.0 / rot)
        ang = pos * inv[lane % half][None, :]
        cols += [jnp.cos(ang), jnp.sin(ang)]
    return jnp.concatenate(cols, axis=1)


def _dft_tables(seq):
    n2 = 64
    n1 = seq // n2
    k = np.arange(seq)[:, None]
    a = 2.0 * np.pi * ((k * np.arange(n1)[None, :] * n2) % seq) / seq
    b = 2.0 * np.pi * ((k * np.arange(n2)[None, :]) % seq) / seq
    ca, sa = jnp.asarray(np.cos(a), F32)[:, :, None], jnp.asarray(np.sin(a), F32)[:, :, None]
    cb, sb = jnp.asarray(np.cos(b), F32)[:, None, :], jnp.asarray(np.sin(b), F32)[:, None, :]
    cs = (ca * cb - sa * sb).reshape(seq, seq).astype(BF16)
    ss = (sa * cb + ca * sb).reshape(seq, seq).astype(BF16)
    return cs, ss


def _block_diag(blocks):
    n = len(blocks)
    rows = [jnp.concatenate([blocks[i] if i == j else jnp.zeros_like(blocks[i]) for j in range(n)], axis=1)
            for i in range(n)]
    return jnp.concatenate(rows, axis=0)


def _perm_w_in(w_in):
    ret = w_in[:, 0:1280]
    dq, dk, dv = w_in[:, 1280:2048], w_in[:, 2048:2816], w_in[:, 2816:3584]
    groups = [jnp.concatenate([m[:, g * MIX_W:(g + 1) * MIX_W] for m in (dq, dk, dv)], axis=1)
              for g in range(N_DIL_GROUPS)]
    pad = jnp.zeros((D_MODEL, D_IN_PAD - w_in.shape[1]), w_in.dtype)
    return jnp.concatenate([ret] + groups + [w_in[:, 3584:], pad], axis=1).astype(BF16)


def _mla_weights(w_qb, w_kvb):
    qh = w_qb.reshape(Q_LORA, N_HEADS, MLA_NOPE + MLA_ROPE)
    wq = jnp.pad(qh, ((0, 0), (0, 0), (0, LANES - MLA_NOPE - MLA_ROPE))).reshape(Q_LORA, N_HEADS * LANES)
    kvh = w_kvb.reshape(KV_LORA, N_HEADS, MLA_NOPE + HEAD_DIM)
    wk = jnp.pad(kvh[:, :, :MLA_NOPE], ((0, 0), (0, 0), (0, LANES - MLA_NOPE))).reshape(KV_LORA, N_HEADS * LANES)
    wv = kvh[:, :, MLA_NOPE:].reshape(KV_LORA, MIX_W)
    place = np.zeros((LANES, N_HEADS * LANES), np.float32)
    for h in range(N_HEADS):
        for r in range(MLA_ROPE):
            place[r, h * LANES + MLA_NOPE + r] = 1.0
    return wq.astype(BF16), wk.astype(BF16), wv.astype(BF16), jnp.asarray(place, BF16)


def _trunk(x, mods, layers, shared):
    b, seq, _ = x.shape
    t = b * seq
    x2d = x.reshape(t, D_MODEL)
    tab = _rope_tables(seq)
    cs, ss = _dft_tables(seq)
    for mod, lw in zip(mods, layers):
        ret, fu, dg0, dg1, dg2, mq, mk, mv = _inproj(x2d, mod, lw["g_pre_mix"], lw["w_in"], tab, lw["q_norm"],
                                                     lw["kv_norm"], lw["wq"], lw["wk"], lw["wv"],
                                                     shared["place"], seq)
        ro = _retention(ret.reshape(b, seq, 1024), lw["lg"]).reshape(t, MIX_W)
        fo = _fourier(fu.reshape(b, seq, MIX_W), cs, ss, shared["cc"], shared["sc"], lw["wf"]).reshape(t, MIX_W)
        d_o, d_l = zip(*[_dilated(dg, g) for g, dg in enumerate((dg0, dg1, dg2))])
        mo = _mla(mq.reshape(b, seq, 512), mk.reshape(b, seq, 512), mv.reshape(b, seq, MIX_W)).reshape(t, MIX_W)
        x2d = _mixffn(x2d, mod, ro, fo, d_o, d_l, mo, lw, seq)
    return x2d.reshape(b, seq, D_MODEL)


def kernel(x_prompt, x_sample, c_prompt, c_sample, w_ada, b_ada, norm_pre_mix, w_in, ret_decay_fwd,
           ret_decay_bwd, w_fmix, mla_q_norm, mla_w_qb, mla_kv_norm, mla_w_kvb, w_out, norm_post_mix,
           norm_pre_ffn, w_up, conv_w, conv_b, w_down, norm_post_ffn):
    depth = w_in.shape[0]
    nb_p, nb_s = c_prompt.shape[0], c_sample.shape[0]
    rows = -(-(nb_p + nb_s) // 8) * 8
    c_all = jnp.concatenate([c_prompt, c_sample, jnp.zeros((rows - nb_p - nb_s, D_MODEL), F32)], axis=0)
    mod_all = _ada(c_all, w_ada, b_ada)
    mods_p = [mod_all[l, :nb_p].reshape(nb_p, 6, D_MODEL) for l in range(depth)]
    mods_s = [mod_all[l, nb_p:nb_p + nb_s].reshape(nb_s, 6, D_MODEL) for l in range(depth)]

    c64 = 2.0 * np.pi * np.outer(np.arange(HEAD_DIM), np.arange(HEAD_DIM)) / HEAD_DIM
    shared = {
        "cc": _block_diag([jnp.asarray(np.cos(c64), BF16)] * N_HEADS),
        "sc": _block_diag([jnp.asarray(np.sin(c64), BF16)] * N_HEADS),
    }
    layers = []
    for l in range(depth):
        wq, wk, wv, place = _mla_weights(mla_w_qb[l], mla_w_kvb[l])
        shared["place"] = place
        layers.append({
            "g_pre_mix": norm_pre_mix[l][None, :],
            "w_in": _perm_w_in(w_in[l]),
            "lg": jnp.stack([jax.nn.log_sigmoid(ret_decay_fwd[l]), jax.nn.log_sigmoid(ret_decay_bwd[l])]),
            "wf": _block_diag([w_fmix[l, g] for g in range(N_HEADS)]).astype(BF16),
            "q_norm": mla_q_norm[l][None, :],
            "kv_norm": mla_kv_norm[l][None, :],
            "wq": wq, "wk": wk, "wv": wv,
            "w_out": w_out[l].astype(BF16),
            "g_post_mix": norm_post_mix[l][None, :],
            "g_pre_ffn": norm_pre_ffn[l][None, :],
            "w_up": w_up[l].astype(BF16),
            "conv_w": conv_w[l],
            "conv_b": conv_b[l][None, :],
            "w_down": w_down[l].astype(BF16),
            "g_post_ffn": norm_post_ffn[l][None, :],
        })
    y_prompt = _trunk(x_prompt, mods_p, layers, shared)
    y_sample = _trunk(x_sample, mods_s, layers, shared)
    return (y_prompt, y_sample)
```

```python
import functools
import math

import numpy as np
import jax
import jax.numpy as jnp
from jax import lax
from jax.experimental import pallas as pl
from jax.experimental.pallas import tpu as pltpu

F32 = jnp.float32
BF16 = jnp.bfloat16

D_MODEL = 1024
HEAD_DIM = 64
N_HEADS = 4
MIX_W = N_HEADS * HEAD_DIM
DIL_PAIRS = ((128, 1), (512, 4), (2048, 16))
N_DIL_GROUPS = 3
DIL_RADIUS = 64
STAT_W = 16
MLA_NOPE = 64
MLA_ROPE = 32
Q_LORA = 256
KV_LORA = 128
D_FF = 2816
ROPE_THETA = 500000.0
RET_THETA = 10000.0
PARTIAL_ROT = HEAD_DIM // 4
EPS = 1e-6
NEG = -1e30
LOG2E = math.log2(math.e)

LANES = 128
D_IN_PAD = 4096
MLA_OFF = 3584
ROW_TILE = 512
FFN_TILE = 1024
RET_CHUNK = 256
RET_UNROLL = 4
DIL_TQ = 128
DIL_STEP_ROWS = 1024
FOUR_TR = 1024
MLA_TQ = 256
MLA_STEP_SCORES = 2 ** 21
FFN_CHUNK = 256
HALO = 16
VMEM_LIMIT = 56 * 1024 * 1024


def _params(*sem):
    return pltpu.CompilerParams(dimension_semantics=sem, vmem_limit_bytes=VMEM_LIMIT)


def _rms(x, g):
    return x * lax.rsqrt(jnp.mean(x * x, axis=-1, keepdims=True) + EPS) * g


def _sigmoid(x):
    return 1.0 / (1.0 + jnp.exp(-x))


def _dot(a, b):
    return jnp.dot(a, b, preferred_element_type=F32)


def _dot_nt(a, b):
    return lax.dot_general(a, b, (((1,), (1,)), ((), ())), preferred_element_type=F32)


def _dot_tn(a, b):
    return lax.dot_general(a, b, (((0,), (0,)), ((), ())), preferred_element_type=F32)


def _ada_kernel(c_ref, w_ref, b_ref, o_ref):
    c = c_ref[...]
    cond = (c * _sigmoid(c)).astype(BF16)
    o_ref[0] = _dot(cond, w_ref[0].astype(BF16)) + b_ref[0]


def _ada(c_all, w_ada, b_ada):
    depth, _, n = w_ada.shape
    rows = c_all.shape[0]
    tn = 1536
    return pl.pallas_call(
        _ada_kernel,
        out_shape=jax.ShapeDtypeStruct((depth, rows, n), F32),
        grid=(depth, n // tn),
        in_specs=[
            pl.BlockSpec((rows, D_MODEL), lambda l, j: (0, 0)),
            pl.BlockSpec((1, D_MODEL, tn), lambda l, j: (l, 0, j)),
            pl.BlockSpec((1, 1, tn), lambda l, j: (l, 0, j)),
        ],
        out_specs=pl.BlockSpec((1, rows, tn), lambda l, j: (l, 0, j)),
        compiler_params=_params("arbitrary", "arbitrary"),
        name="ada",
    )(c_all, w_ada, b_ada.reshape(depth, 1, n))


def _inproj_kernel(x_ref, mod_ref, g_ref, w_ref, tab_ref, qn_ref, kvn_ref, wq_ref, wk_ref, wv_ref,
                   pk_ref, ret_ref, fu_ref, d0_ref, d1_ref, d2_ref, mq_ref, mk_ref, mv_ref, scr_ref, scr2_ref):
    x = x_ref[...]
    sh = mod_ref[0, 0:1, :]
    sc = mod_ref[0, 1:2, :]
    hb = (_rms(x, g_ref[...]) * (1.0 + sc) + sh).astype(BF16)

    def mm(c0, c1):
        return _dot(hb, w_ref[:, c0:c1])

    lane = lax.broadcasted_iota(jnp.int32, (1, LANES), 1)
    j64 = lane & (HEAD_DIM - 1)

    def make_rope(cos, sin, lo_mask, hi_mask, half):
        c = jnp.where(lo_mask | hi_mask, cos, 1.0)
        sa = jnp.where(lo_mask, -sin, 0.0)
        sb = jnp.where(hi_mask, sin, 0.0)

        def apply(z):
            return z * c + pltpu.roll(z, LANES - half, 1) * sa + pltpu.roll(z, half, 1) * sb
        return apply

    rope_ret = make_rope(tab_ref[:, 0:128], tab_ref[:, 128:256], j64 < 32, j64 >= 32, 32)
    rope_dil = make_rope(tab_ref[:, 256:384], tab_ref[:, 384:512], j64 < 8, (j64 >= 8) & (j64 < 16), 8)
    cos_m = tab_ref[:, 512:640]
    sin_m = tab_ref[:, 640:768]
    rope_kr = make_rope(cos_m, sin_m, lane < 16, (lane >= 16) & (lane < 32), 16)
    rope_mq = make_rope(cos_m, sin_m, (lane >= 64) & (lane < 80), (lane >= 80) & (lane < 96), 16)

    z = mm(MLA_OFF, D_IN_PAD)
    cqn = _rms(z[:, 0:Q_LORA], qn_ref[...]).astype(BF16)
    q = _dot(cqn, wq_ref[...])
    scale = (MLA_NOPE + MLA_ROPE) ** -0.5 * LOG2E
    for h in range(N_HEADS):
        r = rope_mq(q[:, h * LANES:(h + 1) * LANES]) * scale
        mq_ref[:, h * LANES:(h + 1) * LANES] = r.astype(BF16)
    ckvn = _rms(z[:, Q_LORA:Q_LORA + KV_LORA], kvn_ref[...]).astype(BF16)
    kr = rope_kr(z[:, 384:512]).astype(BF16)
    mk_ref[...] = (_dot(ckvn, wk_ref[...]) + _dot(kr, pk_ref[...])).astype(BF16)
    mv_ref[...] = _dot(ckvn, wv_ref[...]).astype(BF16)
    z = mm(0, 512)
    for c in range(4):
        r = rope_ret(z[:, c * LANES:(c + 1) * LANES])
        if c >= 2:
            r = r * (HEAD_DIM ** -0.5)
        ret_ref[:, c * LANES:(c + 1) * LANES] = r.astype(BF16)
    ret_ref[:, 512:1024] = mm(512, 1024).astype(BF16)
    fu_ref[...] = mm(1024, 1280).astype(BF16)
    tm = x.shape[0]
    nslab = 3 * MIX_W // LANES
    for g, d_ref in enumerate((d0_ref, d1_ref, d2_ref)):
        dil = DIL_PAIRS[g][1]
        base = g * 3 * MIX_W
        z = mm(1280 + base, 1280 + base + 3 * MIX_W)
        slabs = []
        for c in range(nslab):
            r = z[:, c * LANES:(c + 1) * LANES]
            if c < 4:
                r = rope_dil(r)
            if c < 2:
                r = r * (HEAD_DIM ** -0.5 * LOG2E)
            slabs.append(r)
        if dil == 1:
            d_ref[0, 0] = jnp.concatenate(slabs, axis=1).astype(BF16)
            continue
        for c in range(nslab):
            scr_ref[c] = slabs[c]
        n4 = tm // 4

        def rows4(ref, start, count):
            return jnp.concatenate([ref[c, pl.ds(start, count, stride=4), :] for c in range(nslab)], axis=1)

        if dil == 4:
            for r4 in range(4):
                d_ref[0, r4] = rows4(scr_ref, r4, n4).astype(BF16)
        else:
            for r4 in range(4):
                for c in range(nslab):
                    scr2_ref[c, r4 * n4:(r4 + 1) * n4, :] = scr_ref[c, pl.ds(r4, n4, stride=4), :]
            for r4 in range(4):
                for q4 in range(4):
                    d_ref[0, r4 + 4 * q4] = rows4(scr2_ref, r4 * n4 + q4, tm // 16).astype(BF16)


def _inproj(x2d, mod, g_pre, w_in_p, tab, qn, kvn, wq_p, wk_p, wv_p, pk, seq):
    t = x2d.shape[0]
    b = t // seq
    tm = ROW_TILE
    tps = seq // tm
    const = lambda i: (0, 0)
    row = lambda i: (i, 0)
    flat = [(t, 1024), (t, MIX_W)]
    flat2 = [(t, 512), (t, 512), (t, MIX_W)]
    dils = [d for _, d in DIL_PAIRS]
    assert dils == [1, 4, 16]
    bf = lambda shp: jax.ShapeDtypeStruct(shp, BF16)
    dil_shapes = [bf((b, d, seq // d, 3 * MIX_W)) for d in dils]
    dil_specs = [pl.BlockSpec((1, d, tm // d, 3 * MIX_W), lambda i: (i // tps, 0, i % tps, 0)) for d in dils]
    return pl.pallas_call(
        _inproj_kernel,
        out_shape=[bf(s) for s in flat] + dil_shapes + [bf(s) for s in flat2],
        grid=(t // tm,),
        in_specs=[
            pl.BlockSpec((tm, D_MODEL), row),
            pl.BlockSpec((1, 6, D_MODEL), lambda i: (i // tps, 0, 0)),
            pl.BlockSpec((1, D_MODEL), const),
            pl.BlockSpec((D_MODEL, D_IN_PAD), const),
            pl.BlockSpec((tm, 6 * LANES), lambda i: (i % tps, 0)),
            pl.BlockSpec((1, Q_LORA), const),
            pl.BlockSpec((1, KV_LORA), const),
            pl.BlockSpec((Q_LORA, 512), const),
            pl.BlockSpec((KV_LORA, 512), const),
            pl.BlockSpec((KV_LORA, MIX_W), const),
            pl.BlockSpec((LANES, 512), const),
        ],
        out_specs=[pl.BlockSpec((tm, s[1]), row) for s in flat] + dil_specs
                  + [pl.BlockSpec((tm, s[1]), row) for s in flat2],
        scratch_shapes=[pltpu.VMEM((3 * MIX_W // LANES, tm, LANES), F32),
                        pltpu.VMEM((3 * MIX_W // LANES, tm, LANES), F32)],
        compiler_params=_params("arbitrary"),
        name="inproj",
    )(x2d, mod, g_pre, w_in_p, tab, qn, kvn, wq_p, wk_p, wv_p, pk)


def _ret_kernel(lg_ref, q_ref, k_ref, v_ref, g_ref, o_ref, acc_ref, st_ref, dmat_ref, vec_ref, rdec_ref):
    c = RET_CHUNK
    seq = q_ref.shape[1]
    n_chunks = seq // c
    lane_head = lax.broadcasted_iota(jnp.int32, (1, MIX_W), 1) // HEAD_DIM
    row_head = lax.broadcasted_iota(jnp.int32, (MIX_W, 1), 0) // HEAD_DIM
    blockdiag = row_head == lane_head

    def per_head(idx, d):
        out = lg_ref[d, 0]
        for h in range(1, N_HEADS):
            out = jnp.where(idx == h, lg_ref[d, h], out)
        return out

    @pl.when(pl.program_id(0) == 0)
    def _tables():
        ri = lax.broadcasted_iota(jnp.int32, (c, c), 0)
        ci = lax.broadcasted_iota(jnp.int32, (c, c), 1)
        diff = (ri - ci).astype(F32)
        for h in range(N_HEADS):
            fwd = jnp.exp(jnp.where(diff >= 0, diff, 0.0) * lg_ref[0, h])
            bwd = jnp.exp(jnp.where(diff < 0, -diff, 0.0) * lg_ref[1, h])
            dmat_ref[h] = jnp.where(diff >= 0, fwd, bwd)
        pos = lax.broadcasted_iota(jnp.int32, (c, MIX_W), 0).astype(F32)
        lf = per_head(lane_head, 0)
        lb = per_head(lane_head, 1)
        vec_ref[0] = jnp.exp((pos + 1.0) * lf)
        vec_ref[1] = jnp.exp((c - 1.0 - pos) * lf)
        vec_ref[2] = jnp.exp((c - pos) * lb)
        vec_ref[3] = jnp.exp(pos * lb)
        rdec_ref[0] = jnp.broadcast_to(jnp.exp(c * per_head(row_head, 0)), (MIX_W, MIX_W))
        rdec_ref[1] = jnp.broadcast_to(jnp.exp(c * per_head(row_head, 1)), (MIX_W, MIX_W))

    ones_bd = jnp.where(blockdiag, 1.0, 0.0).astype(BF16)

    def chunk(ref, n):
        return ref[0, pl.ds(pl.multiple_of(n * c, c), c), :]

    def fwd_body(n, carry):
        qn, kn, vn = chunk(q_ref, n), chunk(k_ref, n), chunk(v_ref, n)
        acc = _dot((qn.astype(F32) * vec_ref[0]).astype(BF16), st_ref[...].astype(BF16))
        for h in range(N_HEADS):
            hm = lane_head == h
            s = _dot_nt(jnp.where(hm, qn, jnp.zeros_like(qn)), kn)
            p = (s * dmat_ref[h]).astype(BF16)
            acc = acc + _dot(p, jnp.where(hm, vn, jnp.zeros_like(vn)))
        acc_ref[pl.ds(pl.multiple_of(n * c, c), c), :] = acc
        kv = _dot_tn((kn.astype(F32) * vec_ref[1]).astype(BF16), vn)
        st_ref[...] = st_ref[...] * rdec_ref[0] + jnp.where(blockdiag, kv, 0.0)
        return carry

    st_ref[...] = jnp.zeros_like(st_ref)
    lax.fori_loop(0, n_chunks, fwd_body, 0, unroll=RET_UNROLL)

    def bwd_body(t, carry):
        n = n_chunks - 1 - t
        qn, kn, vn = chunk(q_ref, n), chunk(k_ref, n), chunk(v_ref, n)
        r0 = pl.multiple_of(n * c, c)
        o = acc_ref[pl.ds(r0, c), :] + _dot((qn.astype(F32) * vec_ref[2]).astype(BF16),
                                           st_ref[...].astype(BF16))
        o2 = o * o
        hi = o2.astype(BF16)
        lo = (o2 - hi.astype(F32)).astype(BF16)
        ms = (_dot(hi, ones_bd) + _dot(lo, ones_bd)) * (1.0 / HEAD_DIM)
        gate = chunk(g_ref, n).astype(F32)
        o_ref[0, pl.ds(r0, c), :] = (gate * _sigmoid(gate) * (o * lax.rsqrt(ms + EPS))).astype(BF16)
        kv = _dot_tn((kn.astype(F32) * vec_ref[3]).astype(BF16), vn)
        st_ref[...] = st_ref[...] * rdec_ref[1] + jnp.where(blockdiag, kv, 0.0)
        return carry

    st_ref[...] = jnp.zeros_like(st_ref)
    lax.fori_loop(0, n_chunks, bwd_body, 0, unroll=RET_UNROLL)


def _retention(ret3d, lg):
    b, seq, _ = ret3d.shape
    spec = lambda col: pl.BlockSpec((1, seq, MIX_W), lambda i, col=col: (i, 0, col))
    return pl.pallas_call(
        _ret_kernel,
        out_shape=jax.ShapeDtypeStruct((b, seq, MIX_W), BF16),
        grid=(b,),
        in_specs=[pl.BlockSpec(memory_space=pltpu.SMEM), spec(0), spec(1), spec(2), spec(3)],
        out_specs=pl.BlockSpec((1, seq, MIX_W), lambda i: (i, 0, 0)),
        scratch_shapes=[
            pltpu.VMEM((seq, MIX_W), F32),
            pltpu.VMEM((MIX_W, MIX_W), F32),
            pltpu.VMEM((N_HEADS, RET_CHUNK, RET_CHUNK), F32),
            pltpu.VMEM((4, RET_CHUNK, MIX_W), F32),
            pltpu.VMEM((2, MIX_W, MIX_W), F32),
        ],
        compiler_params=_params("arbitrary"),
        name="retention",
    )(lg, ret3d, ret3d, ret3d, ret3d)


def _fourier_kernel(cs_ref, ss_ref, u_ref, r_ref, cc_ref, sc_ref, wf_ref, o_ref, *, scale):
    half = r_ref.shape[1]
    u_lo = u_ref[0, 0:half, :].astype(F32)
    u_mid = u_ref[0, half:half + 1, :].astype(F32)
    rev = r_ref[0].astype(F32)
    row = lax.broadcasted_iota(jnp.int32, (cs_ref.shape[0], 1), 0)
    alt = jnp.where((row & 1) == 0, 1.0, -1.0)
    z1 = (_dot(cs_ref[...], (u_lo + rev).astype(BF16)) + alt * u_mid).astype(BF16)
    z2 = _dot(ss_ref[...], (u_lo - rev).astype(BF16)).astype(BF16)
    f = (_dot(z1, cc_ref[...]) - _dot(z2, sc_ref[...])) * scale
    o_ref[0] = _dot(f.astype(BF16), wf_ref[...]).astype(BF16)


def _fourier(fu3d, cs, ss, cc, sc, wf):
    b, seq, _ = fu3d.shape
    half = seq // 2
    tr = FOUR_TR
    const = lambda i, j: (0, 0)
    rev = jnp.concatenate([jnp.zeros((b, 1, MIX_W), fu3d.dtype), jnp.flip(fu3d[:, half + 1:, :], axis=1)], axis=1)
    return pl.pallas_call(
        functools.partial(_fourier_kernel, scale=1.0 / math.sqrt(seq * HEAD_DIM)),
        out_shape=jax.ShapeDtypeStruct((b, seq, MIX_W), BF16),
        grid=(seq // tr, b),
        in_specs=[
            pl.BlockSpec((tr, half), lambda i, j: (i, 0)),
            pl.BlockSpec((tr, half), lambda i, j: (i, 0)),
            pl.BlockSpec((1, seq, MIX_W), lambda i, j: (j, 0, 0)),
            pl.BlockSpec((1, half, MIX_W), lambda i, j: (j, 0, 0)),
            pl.BlockSpec((MIX_W, MIX_W), const),
            pl.BlockSpec((MIX_W, MIX_W), const),
            pl.BlockSpec((MIX_W, MIX_W), const),
        ],
        out_specs=pl.BlockSpec((1, tr, MIX_W), lambda i, j: (j, i, 0)),
        compiler_params=_params("arbitrary", "arbitrary"),
        name="fourier",
    )(cs, ss, fu3d, rev, cc, sc, wf)


def _dil_kernel(q_ref, k_ref, v_ref, o_ref, st_ref, bias_ref, *, sub_len, ts, rb, tq, win):
    j = pl.program_id(2)
    nblk = ts // tq
    lane_head = lax.broadcasted_iota(jnp.int32, (1, MIX_W), 1) // HEAD_DIM
    stat_slot = lax.broadcasted_iota(jnp.int32, (1, LANES), 1) // STAT_W

    @pl.when((pl.program_id(0) == 0) & (pl.program_id(1) == 0) & (j == 0))
    def _bias():
        rel = (lax.broadcasted_iota(jnp.int32, (N_HEADS * tq, win), 0) & (tq - 1)) \
            - lax.broadcasted_iota(jnp.int32, (N_HEADS * tq, win), 1)
        for i in range(3):
            bias_ref[i] = jnp.where(jnp.abs(rel + i * DIL_RADIUS) <= DIL_RADIUS, 0.0, NEG)

    for rr in range(rb):
        for blk in range(nblk):
            q0 = j * ts + blk * tq
            if win == sub_len:
                ws = 0
                bias = bias_ref[blk * tq // DIL_RADIUS]
            else:
                ws = pl.multiple_of(jnp.clip(q0 - DIL_RADIUS, 0, sub_len - win), DIL_RADIUS)
                bias = bias_ref[1] if 0 < blk < nblk - 1 else bias_ref[(q0 - ws) // DIL_RADIUS]
            q = q_ref[0, rr, blk * tq:(blk + 1) * tq, :]
            kw = k_ref[0, rr, pl.ds(ws, win), :]
            vw = v_ref[0, rr, pl.ds(ws, win), :]
            zero = jnp.zeros_like(q)
            qs = jnp.concatenate([jnp.where(lane_head == h, q, zero) for h in range(N_HEADS)], axis=0)
            s = _dot_nt(qs, kw) + bias
            m = jnp.max(s, axis=-1, keepdims=True)
            p = jnp.exp2(s - m)
            den = jnp.sum(p, axis=-1, keepdims=True)
            r = _dot(p.astype(BF16), vw)
            o = jnp.zeros((tq, MIX_W), F32)
            st = jnp.zeros((tq, LANES), F32)
            for h in range(N_HEADS):
                o = jnp.where(lane_head == h, r[h * tq:(h + 1) * tq], o)
                st = jnp.where(stat_slot == 2 * h, m[h * tq:(h + 1) * tq], st)
                st = jnp.where(stat_slot == 2 * h + 1, den[h * tq:(h + 1) * tq], st)
            o_ref[0, rr, blk * tq:(blk + 1) * tq, :] = o.astype(BF16)
            st_ref[0, rr, blk * tq:(blk + 1) * tq, :] = st


def _dilated(dg, group):
    b, dil, sub_len, _ = dg.shape
    tq = min(DIL_TQ, sub_len)
    win = min(tq + 2 * DIL_RADIUS, sub_len)
    ts = min(sub_len, DIL_STEP_ROWS)
    rb = min(dil, DIL_STEP_ROWS // ts)
    part = lambda c: (lambda i, r, j: (i, r, 0, c))
    return pl.pallas_call(
        functools.partial(_dil_kernel, sub_len=sub_len, ts=ts, rb=rb, tq=tq, win=win),
        out_shape=[jax.ShapeDtypeStruct((b, dil, sub_len, MIX_W), BF16),
                   jax.ShapeDtypeStruct((b, dil, sub_len, LANES), F32)],
        grid=(b, dil // rb, sub_len // ts),
        in_specs=[
            pl.BlockSpec((1, rb, ts, MIX_W), lambda i, r, j: (i, r, j, 0)),
            pl.BlockSpec((1, rb, sub_len, MIX_W), part(1)),
            pl.BlockSpec((1, rb, sub_len, MIX_W), part(2)),
        ],
        out_specs=[pl.BlockSpec((1, rb, ts, MIX_W), lambda i, r, j: (i, r, j, 0)),
                   pl.BlockSpec((1, rb, ts, LANES), lambda i, r, j: (i, r, j, 0))],
        scratch_shapes=[pltpu.VMEM((3, N_HEADS * tq, win), F32)],
        compiler_params=_params("arbitrary", "arbitrary", "arbitrary"),
        name=f"dilated{group}",
    )(dg, dg, dg)


def _mla_kernel(q_ref, k_ref, v_ref, o_ref):
    lane_head = lax.broadcasted_iota(jnp.int32, (1, MIX_W), 1) // HEAD_DIM
    v = v_ref[0]
    for b0 in range(0, q_ref.shape[1], MLA_TQ):
        out = jnp.zeros((MLA_TQ, MIX_W), F32)
        for h in range(N_HEADS):
            qh = q_ref[0, b0:b0 + MLA_TQ, h * LANES:(h + 1) * LANES]
            s = _dot_nt(qh, k_ref[0, :, h * LANES:(h + 1) * LANES])
            m = jnp.max(s, axis=-1, keepdims=True)
            p = jnp.exp2(s - m)
            den = jnp.sum(p, axis=-1, keepdims=True)
            o = _dot(p.astype(BF16), v) * (1.0 / den)
            out = jnp.where(lane_head == h, o, out)
        o_ref[0, b0:b0 + MLA_TQ, :] = out.astype(BF16)


def _mla(mq3d, mk3d, mv3d):
    b, seq, _ = mq3d.shape
    ts = max(MLA_TQ, min(seq, MLA_STEP_SCORES // seq))
    return pl.pallas_call(
        _mla_kernel,
        out_shape=jax.ShapeDtypeStruct((b, seq, MIX_W), BF16),
        grid=(b, seq // ts),
        in_specs=[
            pl.BlockSpec((1, ts, N_HEADS * LANES), lambda i, j: (i, j, 0)),
            pl.BlockSpec((1, seq, N_HEADS * LANES), lambda i, j: (i, 0, 0)),
            pl.BlockSpec((1, seq, MIX_W), lambda i, j: (i, 0, 0)),
        ],
        out_specs=pl.BlockSpec((1, ts, MIX_W), lambda i, j: (i, j, 0)),
        compiler_params=_params("arbitrary", "arbitrary"),
        name="mla",
    )(mq3d, mk3d, mv3d)


def _outproj_kernel(x_ref, mod_ref, ro_ref, fo_ref, d0_ref, d1_ref, d2_ref, l0_ref, l1_ref, l2_ref,
                    mo_ref, w_ref, gpm_ref, gpf_ref, x1_ref, h2_ref, so1_ref, sl1_ref, so2_ref, sl2_ref):
    tm = x_ref.shape[0]

    def natural_order(o_ref, l_ref, so_ref, sl_ref):
        dil = o_ref.shape[1]
        if dil == 1:
            return o_ref[0, 0].astype(F32), l_ref[0, 0]
        n = tm // dil
        for r in range(dil):
            o = o_ref[0, r].astype(F32)
            for c in range(MIX_W // LANES):
                so_ref[c, pl.ds(r, n, stride=dil), :] = o[:, c * LANES:(c + 1) * LANES]
            sl_ref[pl.ds(r, n, stride=dil), :] = l_ref[0, r]
        return jnp.concatenate([so_ref[c] for c in range(MIX_W // LANES)], axis=1), sl_ref[...]

    o0, l0 = natural_order(d0_ref, l0_ref, None, None)
    o1, l1 = natural_order(d1_ref, l1_ref, so1_ref, sl1_ref)
    o2, l2 = natural_order(d2_ref, l2_ref, so2_ref, sl2_ref)
    m = jnp.maximum(l0, jnp.maximum(l1, l2))
    e = [jnp.exp2(l - m) for l in (l0, l1, l2)]
    den = sum(eg * pltpu.roll(l, LANES - STAT_W, 1) for eg, l in zip(e, (l0, l1, l2)))
    inv = 1.0 / den
    is_max_lane = (lax.broadcasted_iota(jnp.int32, (1, LANES), 1) & STAT_W) == 0
    src = lax.broadcasted_iota(jnp.int32, (LANES, MIX_W), 0)
    dst = lax.broadcasted_iota(jnp.int32, (LANES, MIX_W), 1)
    spread = jnp.where(src == (dst // HEAD_DIM) * (LANES // N_HEADS), 1.0, 0.0).astype(BF16)
    od = sum(_dot(jnp.where(is_max_lane, eg * inv, 0.0).astype(BF16), spread) * og
             for eg, og in zip(e, (o0, o1, o2))).astype(BF16)
    y = (_dot(ro_ref[...], w_ref[0:256, :]) + _dot(fo_ref[...], w_ref[256:512, :])
         + _dot(od, w_ref[512:768, :]) + _dot(mo_ref[...], w_ref[768:1024, :]))
    g1 = mod_ref[0, 2:3, :]
    sh2 = mod_ref[0, 3:4, :]
    sc2 = mod_ref[0, 4:5, :]
    x1 = x_ref[...] + g1 * _rms(y, gpm_ref[...])
    x1_ref[...] = x1
    h2_ref[...] = (_rms(x1, gpf_ref[...]) * (1.0 + sc2) + sh2).astype(BF16)


def _outproj(x2d, mod, ro, fo, d_o, d_l, mo, w_out, g_post_mix, g_pre_ffn, seq):
    t = x2d.shape[0]
    tm = ROW_TILE
    tps = seq // tm
    row = lambda i: (i, 0)
    const = lambda i: (0, 0)
    mix = pl.BlockSpec((tm, MIX_W), row)
    res = lambda a: pl.BlockSpec((1, a.shape[1], tm // a.shape[1], a.shape[3]), lambda i: (i // tps, 0, i % tps, 0))
    return pl.pallas_call(
        _outproj_kernel,
        out_shape=[jax.ShapeDtypeStruct((t, D_MODEL), F32), jax.ShapeDtypeStruct((t, D_MODEL), BF16)],
        grid=(t // tm,),
        in_specs=[
            pl.BlockSpec((tm, D_MODEL), row),
            pl.BlockSpec((1, 6, D_MODEL), lambda i: (i // tps, 0, 0)),
            mix, mix, res(d_o[0]), res(d_o[1]), res(d_o[2]), res(d_l[0]), res(d_l[1]), res(d_l[2]), mix,
            pl.BlockSpec((D_MODEL, D_MODEL), const),
            pl.BlockSpec((1, D_MODEL), const),
            pl.BlockSpec((1, D_MODEL), const),
        ],
        out_specs=[pl.BlockSpec((tm, D_MODEL), row), pl.BlockSpec((tm, D_MODEL), row)],
        scratch_shapes=[pltpu.VMEM((MIX_W // LANES, tm, LANES), F32), pltpu.VMEM((tm, LANES), F32),
                        pltpu.VMEM((MIX_W // LANES, tm, LANES), F32), pltpu.VMEM((tm, LANES), F32)],
        compiler_params=_params("arbitrary"),
        name="outproj",
    )(x2d, mod, ro, fo, d_o[0], d_o[1], d_o[2], d_l[0], d_l[1], d_l[2], mo, w_out, g_post_mix, g_pre_ffn)


def _ffn_kernel(hp_ref, hc_ref, hn_ref, x1_ref, mod_ref, wu_ref, cw_ref, cb_ref, wd_ref, g_ref, o_ref,
                gate_ref, *, tiles_per_seq):
    tm = hc_ref.shape[0]
    t = pl.program_id(0) % tiles_per_seq
    hp = jnp.where(t == 0, jnp.zeros_like(hp_ref[...]), hp_ref[...])
    hn = jnp.where(t == tiles_per_seq - 1, jnp.zeros_like(hn_ref[...]), hn_ref[...])
    he = jnp.concatenate([hp, hc_ref[...], hn], axis=0)

    ext = tm + 2 * HALO

    def conv(c0):
        u = _dot(he, wu_ref[:, c0:c0 + FFN_CHUNK])
        w = cw_ref[:, c0:c0 + FFN_CHUNK]
        prev = pltpu.roll(u, 1, 0)[HALO:HALO + tm]
        nxt = pltpu.roll(u, ext - 1, 0)[HALO:HALO + tm]
        return prev * w[0:1] + u[HALO:HALO + tm] * w[1:2] + nxt * w[2:3] + cb_ref[:, c0:c0 + FFN_CHUNK]

    for c in range(D_FF // FFN_CHUNK):
        a = conv(c * FFN_CHUNK)
        bu = conv(D_FF + c * FFN_CHUNK)
        gate_ref[:, c * FFN_CHUNK:(c + 1) * FFN_CHUNK] = (a * _sigmoid(a) * bu).astype(BF16)
    acc = _dot(gate_ref[...], wd_ref[...])
    g2 = mod_ref[0, 5:6, :]
    o_ref[...] = x1_ref[...] + g2 * _rms(acc, g_ref[...])


def _ffn(h2, x1, mod, w_up, conv_w, conv_b, w_down, g_post_ffn, seq):
    t = x1.shape[0]
    tm = FFN_TILE
    tps = seq // tm
    hb = tm // HALO
    row = lambda i: (i, 0)
    const = lambda i: (0, 0)
    resident = lambda shape: pl.BlockSpec(shape, const, pipeline_mode=pl.Buffered(1))
    return pl.pallas_call(
        functools.partial(_ffn_kernel, tiles_per_seq=tps),
        out_shape=jax.ShapeDtypeStruct((t, D_MODEL), F32),
        grid=(t // tm,),
        in_specs=[
            pl.BlockSpec((HALO, D_MODEL), lambda i: (jnp.maximum(i * hb - 1, 0), 0)),
            pl.BlockSpec((tm, D_MODEL), row),
            pl.BlockSpec((HALO, D_MODEL), lambda i: (jnp.minimum((i + 1) * hb, t // HALO - 1), 0)),
            pl.BlockSpec((tm, D_MODEL), row),
            pl.BlockSpec((1, 6, D_MODEL), lambda i: (i // tps, 0, 0)),
            resident((D_MODEL, 2 * D_FF)),
            pl.BlockSpec((3, 2 * D_FF), const),
            pl.BlockSpec((1, 2 * D_FF), const),
            resident((D_FF, D_MODEL)),
            pl.BlockSpec((1, D_MODEL), const),
        ],
        out_specs=pl.BlockSpec((tm, D_MODEL), row),
        scratch_shapes=[pltpu.VMEM((tm, D_FF), BF16)],
        compiler_params=_params("arbitrary"),
        name="ffn",
    )(h2, h2, h2, x1, mod, w_up, conv_w, conv_b, w_down, g_post_ffn)


def _rope_tables(seq):
    pos = jnp.arange(seq, dtype=F32)[:, None]
    lane = np.arange(LANES)
    cols = []
    for theta, rot in ((RET_THETA, HEAD_DIM), (ROPE_THETA, PARTIAL_ROT), (ROPE_THETA, MLA_ROPE)):
        half = rot // 2
        inv = jnp.power(theta, -jnp.arange(half, dtype=F32) * 2.0 / rot)
        ang = pos * inv[lane % half][None, :]
        cols += [jnp.cos(ang), jnp.sin(ang)]
    return jnp.concatenate(cols, axis=1)


def _dft_tables(seq):
    n2 = 64
    n1 = seq // (2 * n2)
    k = np.arange(seq)[:, None]
    a = 2.0 * np.pi * ((k * np.arange(n1)[None, :] * n2) % seq) / seq
    b = 2.0 * np.pi * ((k * np.arange(n2)[None, :]) % seq) / seq
    ca, sa = jnp.asarray(np.cos(a), F32)[:, :, None], jnp.asarray(np.sin(a), F32)[:, :, None]
    cb, sb = jnp.asarray(np.cos(b), F32)[:, None, :], jnp.asarray(np.sin(b), F32)[:, None, :]
    cs = (ca * cb - sa * sb).reshape(seq, seq // 2).astype(BF16)
    ss = (sa * cb + ca * sb).reshape(seq, seq // 2).astype(BF16)
    return cs, ss


def _block_diag(blocks):
    n = len(blocks)
    rows = [jnp.concatenate([blocks[i] if i == j else jnp.zeros_like(blocks[i]) for j in range(n)], axis=1)
            for i in range(n)]
    return jnp.concatenate(rows, axis=0)


def _perm_w_in(w_in):
    ret = w_in[:, 0:1280]
    dq, dk, dv = w_in[:, 1280:2048], w_in[:, 2048:2816], w_in[:, 2816:3584]
    groups = [jnp.concatenate([m[:, g * MIX_W:(g + 1) * MIX_W] for m in (dq, dk, dv)], axis=1)
              for g in range(N_DIL_GROUPS)]
    pad = jnp.zeros((D_MODEL, D_IN_PAD - w_in.shape[1]), w_in.dtype)
    return jnp.concatenate([ret] + groups + [w_in[:, 3584:], pad], axis=1).astype(BF16)


def _mla_weights(w_qb, w_kvb):
    qh = w_qb.reshape(Q_LORA, N_HEADS, MLA_NOPE + MLA_ROPE)
    wq = jnp.pad(qh, ((0, 0), (0, 0), (0, LANES - MLA_NOPE - MLA_ROPE))).reshape(Q_LORA, N_HEADS * LANES)
    kvh = w_kvb.reshape(KV_LORA, N_HEADS, MLA_NOPE + HEAD_DIM)
    wk = jnp.pad(kvh[:, :, :MLA_NOPE], ((0, 0), (0, 0), (0, LANES - MLA_NOPE))).reshape(KV_LORA, N_HEADS * LANES)
    wv = kvh[:, :, MLA_NOPE:].reshape(KV_LORA, MIX_W)
    place = np.zeros((LANES, N_HEADS * LANES), np.float32)
    for h in range(N_HEADS):
        for r in range(MLA_ROPE):
            place[r, h * LANES + MLA_NOPE + r] = 1.0
    return wq.astype(BF16), wk.astype(BF16), wv.astype(BF16), jnp.asarray(place, BF16)


def _trunk(x, mods, layers, shared):
    b, seq, _ = x.shape
    t = b * seq
    x2d = x.reshape(t, D_MODEL)
    tab = _rope_tables(seq)
    cs, ss = _dft_tables(seq)
    for mod, lw in zip(mods, layers):
        ret, fu, dg0, dg1, dg2, mq, mk, mv = _inproj(x2d, mod, lw["g_pre_mix"], lw["w_in"], tab, lw["q_norm"],
                                                     lw["kv_norm"], lw["wq"], lw["wk"], lw["wv"],
                                                     shared["place"], seq)
        ro = _retention(ret.reshape(b, seq, 1024), lw["lg"]).reshape(t, MIX_W)
        fo = _fourier(fu.reshape(b, seq, MIX_W), cs, ss, shared["cc"], shared["sc"], lw["wf"]).reshape(t, MIX_W)
        d_o, d_l = zip(*[_dilated(dg, g) for g, dg in enumerate((dg0, dg1, dg2))])
        mo = _mla(mq.reshape(b, seq, 512), mk.reshape(b, seq, 512), mv.reshape(b, seq, MIX_W)).reshape(t, MIX_W)
        x1, h2 = _outproj(x2d, mod, ro, fo, d_o, d_l, mo, lw["w_out"], lw["g_post_mix"], lw["g_pre_ffn"], seq)
        x2d = _ffn(h2, x1, mod, lw["w_up"], lw["conv_w"], lw["conv_b"], lw["w_down"], lw["g_post_ffn"], seq)
    return x2d.reshape(b, seq, D_MODEL)


def kernel(x_prompt, x_sample, c_prompt, c_sample, w_ada, b_ada, norm_pre_mix, w_in, ret_decay_fwd,
           ret_decay_bwd, w_fmix, mla_q_norm, mla_w_qb, mla_kv_norm, mla_w_kvb, w_out, norm_post_mix,
           norm_pre_ffn, w_up, conv_w, conv_b, w_down, norm_post_ffn):
    depth = w_in.shape[0]
    nb_p, nb_s = c_prompt.shape[0], c_sample.shape[0]
    rows = -(-(nb_p + nb_s) // 8) * 8
    c_all = jnp.concatenate([c_prompt, c_sample, jnp.zeros((rows - nb_p - nb_s, D_MODEL), F32)], axis=0)
    mod_all = _ada(c_all, w_ada, b_ada)
    mods_p = [mod_all[l, :nb_p].reshape(nb_p, 6, D_MODEL) for l in range(depth)]
    mods_s = [mod_all[l, nb_p:nb_p + nb_s].reshape(nb_s, 6, D_MODEL) for l in range(depth)]

    c64 = 2.0 * np.pi * np.outer(np.arange(HEAD_DIM), np.arange(HEAD_DIM)) / HEAD_DIM
    shared = {
        "cc": _block_diag([jnp.asarray(np.cos(c64), BF16)] * N_HEADS),
        "sc": _block_diag([jnp.asarray(np.sin(c64), BF16)] * N_HEADS),
    }
    layers = []
    for l in range(depth):
        wq, wk, wv, place = _mla_weights(mla_w_qb[l], mla_w_kvb[l])
        shared["place"] = place
        layers.append({
            "g_pre_mix": norm_pre_mix[l][None, :],
            "w_in": _perm_w_in(w_in[l]),
            "lg": jnp.stack([jax.nn.log_sigmoid(ret_decay_fwd[l]), jax.nn.log_sigmoid(ret_decay_bwd[l])]),
            "wf": _block_diag([w_fmix[l, g] for g in range(N_HEADS)]).astype(BF16),
            "q_norm": mla_q_norm[l][None, :],
            "kv_norm": mla_kv_norm[l][None, :],
            "wq": wq, "wk": wk, "wv": wv,
            "w_out": w_out[l].astype(BF16),
            "g_post_mix": norm_post_mix[l][None, :],
            "g_pre_ffn": norm_pre_ffn[l][None, :],
            "w_up": w_up[l].astype(BF16),
            "conv_w": conv_w[l],
            "conv_b": conv_b[l][None, :],
            "w_down": w_down[l].astype(BF16),
            "g_post_ffn": norm_post_ffn[l][None, :],
        })
    y_prompt = _trunk(x_prompt, mods_p, layers, shared)
    y_sample = _trunk(x_sample, mods_s, layers, shared)
    return (y_prompt, y_sample)
```

```python
import functools
import math

import numpy as np
import jax
import jax.numpy as jnp
from jax import lax
from jax.experimental import pallas as pl
from jax.experimental.pallas import tpu as pltpu

F32 = jnp.float32
BF16 = jnp.bfloat16

D_MODEL = 1024
HEAD_DIM = 64
N_HEADS = 4
MIX_W = N_HEADS * HEAD_DIM
DIL_PAIRS = ((128, 1), (512, 4), (2048, 16))
N_DIL_GROUPS = 3
DIL_RADIUS = 64
STAT_W = 16
MLA_NOPE = 64
MLA_ROPE = 32
Q_LORA = 256
KV_LORA = 128
D_FF = 2816
ROPE_THETA = 500000.0
RET_THETA = 10000.0
PARTIAL_ROT = HEAD_DIM // 4
EPS = 1e-6
NEG = -1e30
LOG2E = math.log2(math.e)

LANES = 128
D_IN_PAD = 4096
MLA_OFF = 3584
ROW_TILE = 512
FFN_TILE = 1024
RET_CHUNK = 256
RET_UNROLL = 4
DIL_TQ = 128
DIL_STEP_ROWS = 1024
FOUR_TR = 1024
FLIP_BLOCK = 256
MLA_TQ = 256
MLA_STEP_SCORES = 2 ** 21
FFN_CHUNK = 256
HALO = 16
VMEM_LIMIT = 56 * 1024 * 1024


def _params(*sem):
    return pltpu.CompilerParams(dimension_semantics=sem, vmem_limit_bytes=VMEM_LIMIT)


def _rms(x, g):
    return x * lax.rsqrt(jnp.mean(x * x, axis=-1, keepdims=True) + EPS) * g


def _sigmoid(x):
    return 1.0 / (1.0 + jnp.exp(-x))


def _dot(a, b):
    return jnp.dot(a, b, preferred_element_type=F32)


def _dot_nt(a, b):
    return lax.dot_general(a, b, (((1,), (1,)), ((), ())), preferred_element_type=F32)


def _dot_tn(a, b):
    return lax.dot_general(a, b, (((0,), (0,)), ((), ())), preferred_element_type=F32)


def _ada_kernel(c_ref, w_ref, b_ref, o_ref):
    c = c_ref[...]
    cond = (c * _sigmoid(c)).astype(BF16)
    o_ref[0] = _dot(cond, w_ref[0].astype(BF16)) + b_ref[0]


def _ada(c_all, w_ada, b_ada):
    depth, _, n = w_ada.shape
    rows = c_all.shape[0]
    tn = 1536
    return pl.pallas_call(
        _ada_kernel,
        out_shape=jax.ShapeDtypeStruct((depth, rows, n), F32),
        grid=(depth, n // tn),
        in_specs=[
            pl.BlockSpec((rows, D_MODEL), lambda l, j: (0, 0)),
            pl.BlockSpec((1, D_MODEL, tn), lambda l, j: (l, 0, j)),
            pl.BlockSpec((1, 1, tn), lambda l, j: (l, 0, j)),
        ],
        out_specs=pl.BlockSpec((1, rows, tn), lambda l, j: (l, 0, j)),
        compiler_params=_params("arbitrary", "arbitrary"),
        name="ada",
    )(c_all, w_ada, b_ada.reshape(depth, 1, n))


def _inproj_kernel(x_ref, mod_ref, g_ref, w_ref, tab_ref, qn_ref, kvn_ref, wq_ref, wk_ref, wv_ref,
                   pk_ref, ret_ref, fu_ref, d0_ref, d1_ref, d2_ref, mq_ref, mk_ref, mv_ref, scr_ref, scr2_ref):
    x = x_ref[...]
    sh = mod_ref[0, 0:1, :]
    sc = mod_ref[0, 1:2, :]
    hb = (_rms(x, g_ref[...]) * (1.0 + sc) + sh).astype(BF16)

    def mm(c0, c1):
        return _dot(hb, w_ref[:, c0:c1])

    lane = lax.broadcasted_iota(jnp.int32, (1, LANES), 1)
    j64 = lane & (HEAD_DIM - 1)

    def make_rope(cos, sin, lo_mask, hi_mask, half):
        c = jnp.where(lo_mask | hi_mask, cos, 1.0)
        sa = jnp.where(lo_mask, -sin, 0.0)
        sb = jnp.where(hi_mask, sin, 0.0)

        def apply(z):
            return z * c + pltpu.roll(z, LANES - half, 1) * sa + pltpu.roll(z, half, 1) * sb
        return apply

    rope_ret = make_rope(tab_ref[:, 0:128], tab_ref[:, 128:256], j64 < 32, j64 >= 32, 32)
    rope_dil = make_rope(tab_ref[:, 256:384], tab_ref[:, 384:512], j64 < 8, (j64 >= 8) & (j64 < 16), 8)
    cos_m = tab_ref[:, 512:640]
    sin_m = tab_ref[:, 640:768]
    rope_kr = make_rope(cos_m, sin_m, lane < 16, (lane >= 16) & (lane < 32), 16)
    rope_mq = make_rope(cos_m, sin_m, (lane >= 64) & (lane < 80), (lane >= 80) & (lane < 96), 16)

    z = mm(MLA_OFF, D_IN_PAD)
    cqn = _rms(z[:, 0:Q_LORA], qn_ref[...]).astype(BF16)
    q = _dot(cqn, wq_ref[...])
    scale = (MLA_NOPE + MLA_ROPE) ** -0.5 * LOG2E
    for h in range(N_HEADS):
        r = rope_mq(q[:, h * LANES:(h + 1) * LANES]) * scale
        mq_ref[:, h * LANES:(h + 1) * LANES] = r.astype(BF16)
    ckvn = _rms(z[:, Q_LORA:Q_LORA + KV_LORA], kvn_ref[...]).astype(BF16)
    kr = rope_kr(z[:, 384:512]).astype(BF16)
    mk_ref[...] = (_dot(ckvn, wk_ref[...]) + _dot(kr, pk_ref[...])).astype(BF16)
    mv_ref[...] = _dot(ckvn, wv_ref[...]).astype(BF16)
    z = mm(0, 512)
    for c in range(4):
        r = rope_ret(z[:, c * LANES:(c + 1) * LANES])
        if c >= 2:
            r = r * (HEAD_DIM ** -0.5)
        ret_ref[:, c * LANES:(c + 1) * LANES] = r.astype(BF16)
    ret_ref[:, 512:1024] = mm(512, 1024).astype(BF16)
    fu_ref[...] = mm(1024, 1280).astype(BF16)
    tm = x.shape[0]
    nslab = 3 * MIX_W // LANES
    for g, d_ref in enumerate((d0_ref, d1_ref, d2_ref)):
        dil = DIL_PAIRS[g][1]
        base = g * 3 * MIX_W
        z = mm(1280 + base, 1280 + base + 3 * MIX_W)
        slabs = []
        for c in range(nslab):
            r = z[:, c * LANES:(c + 1) * LANES]
            if c < 4:
                r = rope_dil(r)
            if c < 2:
                r = r * (HEAD_DIM ** -0.5 * LOG2E)
            slabs.append(r)
        if dil == 1:
            d_ref[0, 0] = jnp.concatenate(slabs, axis=1).astype(BF16)
            continue
        for c in range(nslab):
            scr_ref[c] = slabs[c]
        n4 = tm // 4

        def rows4(ref, start, count):
            return jnp.concatenate([ref[c, pl.ds(start, count, stride=4), :] for c in range(nslab)], axis=1)

        if dil == 4:
            for r4 in range(4):
                d_ref[0, r4] = rows4(scr_ref, r4, n4).astype(BF16)
        else:
            for r4 in range(4):
                for c in range(nslab):
                    scr2_ref[c, r4 * n4:(r4 + 1) * n4, :] = scr_ref[c, pl.ds(r4, n4, stride=4), :]
            for r4 in range(4):
                for q4 in range(4):
                    d_ref[0, r4 + 4 * q4] = rows4(scr2_ref, r4 * n4 + q4, tm // 16).astype(BF16)


def _inproj(x2d, mod, g_pre, w_in_p, tab, qn, kvn, wq_p, wk_p, wv_p, pk, seq):
    t = x2d.shape[0]
    b = t // seq
    tm = ROW_TILE
    tps = seq // tm
    const = lambda i: (0, 0)
    row = lambda i: (i, 0)
    flat = [(t, 1024), (t, MIX_W)]
    flat2 = [(t, 512), (t, 512), (t, MIX_W)]
    dils = [d for _, d in DIL_PAIRS]
    assert dils == [1, 4, 16]
    bf = lambda shp: jax.ShapeDtypeStruct(shp, BF16)
    dil_shapes = [bf((b, d, seq // d, 3 * MIX_W)) for d in dils]
    dil_specs = [pl.BlockSpec((1, d, tm // d, 3 * MIX_W), lambda i: (i // tps, 0, i % tps, 0)) for d in dils]
    return pl.pallas_call(
        _inproj_kernel,
        out_shape=[bf(s) for s in flat] + dil_shapes + [bf(s) for s in flat2],
        grid=(t // tm,),
        in_specs=[
            pl.BlockSpec((tm, D_MODEL), row),
            pl.BlockSpec((1, 6, D_MODEL), lambda i: (i // tps, 0, 0)),
            pl.BlockSpec((1, D_MODEL), const),
            pl.BlockSpec((D_MODEL, D_IN_PAD), const),
            pl.BlockSpec((tm, 6 * LANES), lambda i: (i % tps, 0)),
            pl.BlockSpec((1, Q_LORA), const),
            pl.BlockSpec((1, KV_LORA), const),
            pl.BlockSpec((Q_LORA, 512), const),
            pl.BlockSpec((KV_LORA, 512), const),
            pl.BlockSpec((KV_LORA, MIX_W), const),
            pl.BlockSpec((LANES, 512), const),
        ],
        out_specs=[pl.BlockSpec((tm, s[1]), row) for s in flat] + dil_specs
                  + [pl.BlockSpec((tm, s[1]), row) for s in flat2],
        scratch_shapes=[pltpu.VMEM((3 * MIX_W // LANES, tm, LANES), F32),
                        pltpu.VMEM((3 * MIX_W // LANES, tm, LANES), F32)],
        compiler_params=_params("arbitrary"),
        name="inproj",
    )(x2d, mod, g_pre, w_in_p, tab, qn, kvn, wq_p, wk_p, wv_p, pk)


def _ret_kernel(lg_ref, q_ref, k_ref, v_ref, g_ref, o_ref, acc_ref, st_ref, dmat_ref, vec_ref, rdec_ref):
    c = RET_CHUNK
    seq = q_ref.shape[1]
    n_chunks = seq // c
    lane_head = lax.broadcasted_iota(jnp.int32, (1, MIX_W), 1) // HEAD_DIM
    row_head = lax.broadcasted_iota(jnp.int32, (MIX_W, 1), 0) // HEAD_DIM
    blockdiag = row_head == lane_head

    def per_head(idx, d):
        out = lg_ref[d, 0]
        for h in range(1, N_HEADS):
            out = jnp.where(idx == h, lg_ref[d, h], out)
        return out

    @pl.when(pl.program_id(0) == 0)
    def _tables():
        ri = lax.broadcasted_iota(jnp.int32, (c, c), 0)
        ci = lax.broadcasted_iota(jnp.int32, (c, c), 1)
        diff = (ri - ci).astype(F32)
        for h in range(N_HEADS):
            fwd = jnp.exp(jnp.where(diff >= 0, diff, 0.0) * lg_ref[0, h])
            bwd = jnp.exp(jnp.where(diff < 0, -diff, 0.0) * lg_ref[1, h])
            dmat_ref[h] = jnp.where(diff >= 0, fwd, bwd)
        pos = lax.broadcasted_iota(jnp.int32, (c, MIX_W), 0).astype(F32)
        lf = per_head(lane_head, 0)
        lb = per_head(lane_head, 1)
        vec_ref[0] = jnp.exp((pos + 1.0) * lf)
        vec_ref[1] = jnp.exp((c - 1.0 - pos) * lf)
        vec_ref[2] = jnp.exp((c - pos) * lb)
        vec_ref[3] = jnp.exp(pos * lb)
        rdec_ref[0] = jnp.broadcast_to(jnp.exp(c * per_head(row_head, 0)), (MIX_W, MIX_W))
        rdec_ref[1] = jnp.broadcast_to(jnp.exp(c * per_head(row_head, 1)), (MIX_W, MIX_W))

    ones_bd = jnp.where(blockdiag, 1.0, 0.0).astype(BF16)

    def chunk(ref, n):
        return ref[0, pl.ds(pl.multiple_of(n * c, c), c), :]

    def fwd_body(n, carry):
        qn, kn, vn = chunk(q_ref, n), chunk(k_ref, n), chunk(v_ref, n)
        acc = _dot((qn.astype(F32) * vec_ref[0]).astype(BF16), st_ref[...].astype(BF16))
        for h in range(N_HEADS):
            hm = lane_head == h
            s = _dot_nt(jnp.where(hm, qn, jnp.zeros_like(qn)), kn)
            p = (s * dmat_ref[h]).astype(BF16)
            acc = acc + _dot(p, jnp.where(hm, vn, jnp.zeros_like(vn)))
        acc_ref[pl.ds(pl.multiple_of(n * c, c), c), :] = acc
        kv = _dot_tn((kn.astype(F32) * vec_ref[1]).astype(BF16), vn)
        st_ref[...] = st_ref[...] * rdec_ref[0] + jnp.where(blockdiag, kv, 0.0)
        return carry

    st_ref[...] = jnp.zeros_like(st_ref)
    lax.fori_loop(0, n_chunks, fwd_body, 0, unroll=RET_UNROLL)

    def bwd_body(t, carry):
        n = n_chunks - 1 - t
        qn, kn, vn = chunk(q_ref, n), chunk(k_ref, n), chunk(v_ref, n)
        r0 = pl.multiple_of(n * c, c)
        o = acc_ref[pl.ds(r0, c), :] + _dot((qn.astype(F32) * vec_ref[2]).astype(BF16),
                                           st_ref[...].astype(BF16))
        o2 = o * o
        hi = o2.astype(BF16)
        lo = (o2 - hi.astype(F32)).astype(BF16)
        ms = (_dot(hi, ones_bd) + _dot(lo, ones_bd)) * (1.0 / HEAD_DIM)
        gate = chunk(g_ref, n).astype(F32)
        o_ref[0, pl.ds(r0, c), :] = (gate * _sigmoid(gate) * (o * lax.rsqrt(ms + EPS))).astype(BF16)
        kv = _dot_tn((kn.astype(F32) * vec_ref[3]).astype(BF16), vn)
        st_ref[...] = st_ref[...] * rdec_ref[1] + jnp.where(blockdiag, kv, 0.0)
        return carry

    st_ref[...] = jnp.zeros_like(st_ref)
    lax.fori_loop(0, n_chunks, bwd_body, 0, unroll=RET_UNROLL)


def _retention(ret3d, lg):
    b, seq, _ = ret3d.shape
    spec = lambda col: pl.BlockSpec((1, seq, MIX_W), lambda i, col=col: (i, 0, col))
    return pl.pallas_call(
        _ret_kernel,
        out_shape=jax.ShapeDtypeStruct((b, seq, MIX_W), BF16),
        grid=(b,),
        in_specs=[pl.BlockSpec(memory_space=pltpu.SMEM), spec(0), spec(1), spec(2), spec(3)],
        out_specs=pl.BlockSpec((1, seq, MIX_W), lambda i: (i, 0, 0)),
        scratch_shapes=[
            pltpu.VMEM((seq, MIX_W), F32),
            pltpu.VMEM((MIX_W, MIX_W), F32),
            pltpu.VMEM((N_HEADS, RET_CHUNK, RET_CHUNK), F32),
            pltpu.VMEM((4, RET_CHUNK, MIX_W), F32),
            pltpu.VMEM((2, MIX_W, MIX_W), F32),
        ],
        compiler_params=_params("arbitrary"),
        name="retention",
    )(lg, ret3d, ret3d, ret3d, ret3d)


def _fourier_kernel(cs_ref, ss_ref, u_ref, cc_ref, sc_ref, wf_ref, o_ref, *, scale):
    half = u_ref.shape[1] // 2
    blk = FLIP_BLOCK
    nb = half // blk
    ri = lax.broadcasted_iota(jnp.int32, (blk, blk), 0)
    ci = lax.broadcasted_iota(jnp.int32, (blk, blk), 1)
    flip = jnp.where(ri + ci == blk, 1.0, 0.0).astype(BF16)
    first = lax.broadcasted_iota(jnp.int32, (blk, 1), 0) == 0
    upper = lambda c: u_ref[0, half + c * blk:half + (c + 1) * blk, :]
    plus, minus = [], []
    for a in range(nb):
        row0 = upper(nb - a)[0:1].astype(F32) if a > 0 else jnp.zeros((1, MIX_W), F32)
        rev = jnp.where(first, row0, _dot(flip, upper(nb - a - 1)))
        lo = u_ref[0, a * blk:(a + 1) * blk, :].astype(F32)
        plus.append((lo + rev).astype(BF16))
        minus.append((lo - rev).astype(BF16))
    u_mid = u_ref[0, half:half + 1, :].astype(F32)
    row = lax.broadcasted_iota(jnp.int32, (cs_ref.shape[0], 1), 0)
    alt = jnp.where((row & 1) == 0, 1.0, -1.0)
    z1 = (_dot(cs_ref[...], jnp.concatenate(plus, axis=0)) + alt * u_mid).astype(BF16)
    z2 = _dot(ss_ref[...], jnp.concatenate(minus, axis=0)).astype(BF16)
    f = (_dot(z1, cc_ref[...]) - _dot(z2, sc_ref[...])) * scale
    o_ref[0] = _dot(f.astype(BF16), wf_ref[...]).astype(BF16)


def _fourier(fu3d, cs, ss, cc, sc, wf):
    b, seq, _ = fu3d.shape
    half = seq // 2
    tr = FOUR_TR
    const = lambda i, j: (0, 0)
    return pl.pallas_call(
        functools.partial(_fourier_kernel, scale=1.0 / math.sqrt(seq * HEAD_DIM)),
        out_shape=jax.ShapeDtypeStruct((b, seq, MIX_W), BF16),
        grid=(seq // tr, b),
        in_specs=[
            pl.BlockSpec((tr, half), lambda i, j: (i, 0)),
            pl.BlockSpec((tr, half), lambda i, j: (i, 0)),
            pl.BlockSpec((1, seq, MIX_W), lambda i, j: (j, 0, 0)),
            pl.BlockSpec((MIX_W, MIX_W), const),
            pl.BlockSpec((MIX_W, MIX_W), const),
            pl.BlockSpec((MIX_W, MIX_W), const),
        ],
        out_specs=pl.BlockSpec((1, tr, MIX_W), lambda i, j: (j, i, 0)),
        compiler_params=_params("arbitrary", "arbitrary"),
        name="fourier",
    )(cs, ss, fu3d, cc, sc, wf)


def _dil_kernel(q_ref, k_ref, v_ref, o_ref, st_ref, bias_ref, *, sub_len, ts, rb, tq, win):
    j = pl.program_id(2)
    nblk = ts // tq
    lane_head = lax.broadcasted_iota(jnp.int32, (1, MIX_W), 1) // HEAD_DIM
    stat_slot = lax.broadcasted_iota(jnp.int32, (1, LANES), 1) // STAT_W

    @pl.when((pl.program_id(0) == 0) & (pl.program_id(1) == 0) & (j == 0))
    def _bias():
        rel = (lax.broadcasted_iota(jnp.int32, (N_HEADS * tq, win), 0) & (tq - 1)) \
            - lax.broadcasted_iota(jnp.int32, (N_HEADS * tq, win), 1)
        for i in range(3):
            bias_ref[i] = jnp.where(jnp.abs(rel + i * DIL_RADIUS) <= DIL_RADIUS, 0.0, NEG)

    for rr in range(rb):
        for blk in range(nblk):
            q0 = j * ts + blk * tq
            if win == sub_len:
                ws = 0
                bias = bias_ref[blk * tq // DIL_RADIUS]
            else:
                ws = pl.multiple_of(jnp.clip(q0 - DIL_RADIUS, 0, sub_len - win), DIL_RADIUS)
                bias = bias_ref[1] if 0 < blk < nblk - 1 else bias_ref[(q0 - ws) // DIL_RADIUS]
            q = q_ref[0, rr, blk * tq:(blk + 1) * tq, :]
            kw = k_ref[0, rr, pl.ds(ws, win), :]
            vw = v_ref[0, rr, pl.ds(ws, win), :]
            zero = jnp.zeros_like(q)
            qs = jnp.concatenate([jnp.where(lane_head == h, q, zero) for h in range(N_HEADS)], axis=0)
            s = _dot_nt(qs, kw) + bias
            m = jnp.max(s, axis=-1, keepdims=True)
            p = jnp.exp2(s - m)
            den = jnp.sum(p, axis=-1, keepdims=True)
            r = _dot(p.astype(BF16), vw)
            o = jnp.zeros((tq, MIX_W), F32)
            st = jnp.zeros((tq, LANES), F32)
            for h in range(N_HEADS):
                o = jnp.where(lane_head == h, r[h * tq:(h + 1) * tq], o)
                st = jnp.where(stat_slot == 2 * h, m[h * tq:(h + 1) * tq], st)
                st = jnp.where(stat_slot == 2 * h + 1, den[h * tq:(h + 1) * tq], st)
            o_ref[0, rr, blk * tq:(blk + 1) * tq, :] = o.astype(BF16)
            st_ref[0, rr, blk * tq:(blk + 1) * tq, :] = st


def _dilated(dg, group):
    b, dil, sub_len, _ = dg.shape
    tq = min(DIL_TQ, sub_len)
    win = min(tq + 2 * DIL_RADIUS, sub_len)
    ts = min(sub_len, DIL_STEP_ROWS)
    rb = min(dil, DIL_STEP_ROWS // ts)
    part = lambda c: (lambda i, r, j: (i, r, 0, c))
    return pl.pallas_call(
        functools.partial(_dil_kernel, sub_len=sub_len, ts=ts, rb=rb, tq=tq, win=win),
        out_shape=[jax.ShapeDtypeStruct((b, dil, sub_len, MIX_W), BF16),
                   jax.ShapeDtypeStruct((b, dil, sub_len, LANES), F32)],
        grid=(b, dil // rb, sub_len // ts),
        in_specs=[
            pl.BlockSpec((1, rb, ts, MIX_W), lambda i, r, j: (i, r, j, 0)),
            pl.BlockSpec((1, rb, sub_len, MIX_W), part(1)),
            pl.BlockSpec((1, rb, sub_len, MIX_W), part(2)),
        ],
        out_specs=[pl.BlockSpec((1, rb, ts, MIX_W), lambda i, r, j: (i, r, j, 0)),
                   pl.BlockSpec((1, rb, ts, LANES), lambda i, r, j: (i, r, j, 0))],
        scratch_shapes=[pltpu.VMEM((3, N_HEADS * tq, win), F32)],
        compiler_params=_params("arbitrary", "arbitrary", "arbitrary"),
        name=f"dilated{group}",
    )(dg, dg, dg)


def _mla_kernel(q_ref, k_ref, v_ref, o_ref):
    lane_head = lax.broadcasted_iota(jnp.int32, (1, MIX_W), 1) // HEAD_DIM
    v = v_ref[0]
    for b0 in range(0, q_ref.shape[1], MLA_TQ):
        out = jnp.zeros((MLA_TQ, MIX_W), F32)
        for h in range(N_HEADS):
            qh = q_ref[0, b0:b0 + MLA_TQ, h * LANES:(h + 1) * LANES]
            s = _dot_nt(qh, k_ref[0, :, h * LANES:(h + 1) * LANES])
            m = jnp.max(s, axis=-1, keepdims=True)
            p = jnp.exp2(s - m)
            den = jnp.sum(p, axis=-1, keepdims=True)
            o = _dot(p.astype(BF16), v) * (1.0 / den)
            out = jnp.where(lane_head == h, o, out)
        o_ref[0, b0:b0 + MLA_TQ, :] = out.astype(BF16)


def _mla(mq3d, mk3d, mv3d):
    b, seq, _ = mq3d.shape
    ts = max(MLA_TQ, min(seq, MLA_STEP_SCORES // seq))
    return pl.pallas_call(
        _mla_kernel,
        out_shape=jax.ShapeDtypeStruct((b, seq, MIX_W), BF16),
        grid=(b, seq // ts),
        in_specs=[
            pl.BlockSpec((1, ts, N_HEADS * LANES), lambda i, j: (i, j, 0)),
            pl.BlockSpec((1, seq, N_HEADS * LANES), lambda i, j: (i, 0, 0)),
            pl.BlockSpec((1, seq, MIX_W), lambda i, j: (i, 0, 0)),
        ],
        out_specs=pl.BlockSpec((1, ts, MIX_W), lambda i, j: (i, j, 0)),
        compiler_params=_params("arbitrary", "arbitrary"),
        name="mla",
    )(mq3d, mk3d, mv3d)


def _outproj_kernel(x_ref, mod_ref, ro_ref, fo_ref, d0_ref, d1_ref, d2_ref, l0_ref, l1_ref, l2_ref,
                    mo_ref, w_ref, gpm_ref, gpf_ref, x1_ref, h2_ref, so1_ref, sl1_ref, so2_ref, sl2_ref):
    tm = x_ref.shape[0]

    def natural_order(o_ref, l_ref, so_ref, sl_ref):
        dil = o_ref.shape[1]
        if dil == 1:
            return o_ref[0, 0].astype(F32), l_ref[0, 0]
        n = tm // dil
        for r in range(dil):
            o = o_ref[0, r].astype(F32)
            for c in range(MIX_W // LANES):
                so_ref[c, pl.ds(r, n, stride=dil), :] = o[:, c * LANES:(c + 1) * LANES]
            sl_ref[pl.ds(r, n, stride=dil), :] = l_ref[0, r]
        return jnp.concatenate([so_ref[c] for c in range(MIX_W // LANES)], axis=1), sl_ref[...]

    o0, l0 = natural_order(d0_ref, l0_ref, None, None)
    o1, l1 = natural_order(d1_ref, l1_ref, so1_ref, sl1_ref)
    o2, l2 = natural_order(d2_ref, l2_ref, so2_ref, sl2_ref)
    m = jnp.maximum(l0, jnp.maximum(l1, l2))
    e = [jnp.exp2(l - m) for l in (l0, l1, l2)]
    den = sum(eg * pltpu.roll(l, LANES - STAT_W, 1) for eg, l in zip(e, (l0, l1, l2)))
    inv = 1.0 / den
    is_max_lane = (lax.broadcasted_iota(jnp.int32, (1, LANES), 1) & STAT_W) == 0
    src = lax.broadcasted_iota(jnp.int32, (LANES, MIX_W), 0)
    dst = lax.broadcasted_iota(jnp.int32, (LANES, MIX_W), 1)
    spread = jnp.where(src == (dst // HEAD_DIM) * (LANES // N_HEADS), 1.0, 0.0).astype(BF16)
    od = sum(_dot(jnp.where(is_max_lane, eg * inv, 0.0).astype(BF16), spread) * og
             for eg, og in zip(e, (o0, o1, o2))).astype(BF16)
    y = (_dot(ro_ref[...], w_ref[0:256, :]) + _dot(fo_ref[...], w_ref[256:512, :])
         + _dot(od, w_ref[512:768, :]) + _dot(mo_ref[...], w_ref[768:1024, :]))
    g1 = mod_ref[0, 2:3, :]
    sh2 = mod_ref[0, 3:4, :]
    sc2 = mod_ref[0, 4:5, :]
    x1 = x_ref[...] + g1 * _rms(y, gpm_ref[...])
    x1_ref[...] = x1
    h2_ref[...] = (_rms(x1, gpf_ref[...]) * (1.0 + sc2) + sh2).astype(BF16)


def _outproj(x2d, mod, ro, fo, d_o, d_l, mo, w_out, g_post_mix, g_pre_ffn, seq):
    t = x2d.shape[0]
    tm = ROW_TILE
    tps = seq // tm
    row = lambda i: (i, 0)
    const = lambda i: (0, 0)
    mix = pl.BlockSpec((tm, MIX_W), row)
    res = lambda a: pl.BlockSpec((1, a.shape[1], tm // a.shape[1], a.shape[3]), lambda i: (i // tps, 0, i % tps, 0))
    return pl.pallas_call(
        _outproj_kernel,
        out_shape=[jax.ShapeDtypeStruct((t, D_MODEL), F32), jax.ShapeDtypeStruct((t, D_MODEL), BF16)],
        grid=(t // tm,),
        in_specs=[
            pl.BlockSpec((tm, D_MODEL), row),
            pl.BlockSpec((1, 6, D_MODEL), lambda i: (i // tps, 0, 0)),
            mix, mix, res(d_o[0]), res(d_o[1]), res(d_o[2]), res(d_l[0]), res(d_l[1]), res(d_l[2]), mix,
            pl.BlockSpec((D_MODEL, D_MODEL), const),
            pl.BlockSpec((1, D_MODEL), const),
            pl.BlockSpec((1, D_MODEL), const),
        ],
        out_specs=[pl.BlockSpec((tm, D_MODEL), row), pl.BlockSpec((tm, D_MODEL), row)],
        scratch_shapes=[pltpu.VMEM((MIX_W // LANES, tm, LANES), F32), pltpu.VMEM((tm, LANES), F32),
                        pltpu.VMEM((MIX_W // LANES, tm, LANES), F32), pltpu.VMEM((tm, LANES), F32)],
        compiler_params=_params("arbitrary"),
        name="outproj",
    )(x2d, mod, ro, fo, d_o[0], d_o[1], d_o[2], d_l[0], d_l[1], d_l[2], mo, w_out, g_post_mix, g_pre_ffn)


def _ffn_kernel(hp_ref, hc_ref, hn_ref, x1_ref, mod_ref, wu_ref, cw_ref, cb_ref, wd_ref, g_ref, o_ref,
                gate_ref, *, tiles_per_seq):
    tm = hc_ref.shape[0]
    t = pl.program_id(0) % tiles_per_seq
    hp = jnp.where(t == 0, jnp.zeros_like(hp_ref[...]), hp_ref[...])
    hn = jnp.where(t == tiles_per_seq - 1, jnp.zeros_like(hn_ref[...]), hn_ref[...])
    he = jnp.concatenate([hp, hc_ref[...], hn], axis=0)

    ext = tm + 2 * HALO

    def conv(c0):
        u = _dot(he, wu_ref[:, c0:c0 + FFN_CHUNK])
        w = cw_ref[:, c0:c0 + FFN_CHUNK]
        prev = pltpu.roll(u, 1, 0)[HALO:HALO + tm]
        nxt = pltpu.roll(u, ext - 1, 0)[HALO:HALO + tm]
        return prev * w[0:1] + u[HALO:HALO + tm] * w[1:2] + nxt * w[2:3] + cb_ref[:, c0:c0 + FFN_CHUNK]

    for c in range(D_FF // FFN_CHUNK):
        a = conv(c * FFN_CHUNK)
        bu = conv(D_FF + c * FFN_CHUNK)
        gate_ref[:, c * FFN_CHUNK:(c + 1) * FFN_CHUNK] = (a * _sigmoid(a) * bu).astype(BF16)
    acc = _dot(gate_ref[...], wd_ref[...])
    g2 = mod_ref[0, 5:6, :]
    o_ref[...] = x1_ref[...] + g2 * _rms(acc, g_ref[...])


def _ffn(h2, x1, mod, w_up, conv_w, conv_b, w_down, g_post_ffn, seq):
    t = x1.shape[0]
    tm = FFN_TILE
    tps = seq // tm
    hb = tm // HALO
    row = lambda i: (i, 0)
    const = lambda i: (0, 0)
    resident = lambda shape: pl.BlockSpec(shape, const, pipeline_mode=pl.Buffered(1))
    return pl.pallas_call(
        functools.partial(_ffn_kernel, tiles_per_seq=tps),
        out_shape=jax.ShapeDtypeStruct((t, D_MODEL), F32),
        grid=(t // tm,),
        in_specs=[
            pl.BlockSpec((HALO, D_MODEL), lambda i: (jnp.maximum(i * hb - 1, 0), 0)),
            pl.BlockSpec((tm, D_MODEL), row),
            pl.BlockSpec((HALO, D_MODEL), lambda i: (jnp.minimum((i + 1) * hb, t // HALO - 1), 0)),
            pl.BlockSpec((tm, D_MODEL), row),
            pl.BlockSpec((1, 6, D_MODEL), lambda i: (i // tps, 0, 0)),
            resident((D_MODEL, 2 * D_FF)),
            pl.BlockSpec((3, 2 * D_FF), const),
            pl.BlockSpec((1, 2 * D_FF), const),
            resident((D_FF, D_MODEL)),
            pl.BlockSpec((1, D_MODEL), const),
        ],
        out_specs=pl.BlockSpec((tm, D_MODEL), row),
        scratch_shapes=[pltpu.VMEM((tm, D_FF), BF16)],
        compiler_params=_params("arbitrary"),
        name="ffn",
    )(h2, h2, h2, x1, mod, w_up, conv_w, conv_b, w_down, g_post_ffn)


def _rope_tables(seq):
    pos = jnp.arange(seq, dtype=F32)[:, None]
    lane = np.arange(LANES)
    cols = []
    for theta, rot in ((RET_THETA, HEAD_DIM), (ROPE_THETA, PARTIAL_ROT), (ROPE_THETA, MLA_ROPE)):
        half = rot // 2
        inv = jnp.power(theta, -jnp.arange(half, dtype=F32) * 2.0 / rot)
        ang = pos * inv[lane % half][None, :]
        cols += [jnp.cos(ang), jnp.sin(ang)]
    return jnp.concatenate(cols, axis=1)


def _dft_tables(seq):
    n2 = 64
    n1 = seq // (2 * n2)
    k = np.arange(seq)[:, None]
    a = 2.0 * np.pi * ((k * np.arange(n1)[None, :] * n2) % seq) / seq
    b = 2.0 * np.pi * ((k * np.arange(n2)[None, :]) % seq) / seq
    ca, sa = jnp.asarray(np.cos(a), F32)[:, :, None], jnp.asarray(np.sin(a), F32)[:, :, None]
    cb, sb = jnp.asarray(np.cos(b), F32)[:, None, :], jnp.asarray(np.sin(b), F32)[:, None, :]
    cs = (ca * cb - sa * sb).reshape(seq, seq // 2).astype(BF16)
    ss = (sa * cb + ca * sb).reshape(seq, seq // 2).astype(BF16)
    return cs, ss


def _block_diag(blocks):
    n = len(blocks)
    rows = [jnp.concatenate([blocks[i] if i == j else jnp.zeros_like(blocks[i]) for j in range(n)], axis=1)
            for i in range(n)]
    return jnp.concatenate(rows, axis=0)


def _perm_w_in(w_in):
    ret = w_in[:, 0:1280]
    dq, dk, dv = w_in[:, 1280:2048], w_in[:, 2048:2816], w_in[:, 2816:3584]
    groups = [jnp.concatenate([m[:, g * MIX_W:(g + 1) * MIX_W] for m in (dq, dk, dv)], axis=1)
              for g in range(N_DIL_GROUPS)]
    pad = jnp.zeros((D_MODEL, D_IN_PAD - w_in.shape[1]), w_in.dtype)
    return jnp.concatenate([ret] + groups + [w_in[:, 3584:], pad], axis=1).astype(BF16)


def _mla_weights(w_qb, w_kvb):
    qh = w_qb.reshape(Q_LORA, N_HEADS, MLA_NOPE + MLA_ROPE)
    wq = jnp.pad(qh, ((0, 0), (0, 0), (0, LANES - MLA_NOPE - MLA_ROPE))).reshape(Q_LORA, N_HEADS * LANES)
    kvh = w_kvb.reshape(KV_LORA, N_HEADS, MLA_NOPE + HEAD_DIM)
    wk = jnp.pad(kvh[:, :, :MLA_NOPE], ((0, 0), (0, 0), (0, LANES - MLA_NOPE))).reshape(KV_LORA, N_HEADS * LANES)
    wv = kvh[:, :, MLA_NOPE:].reshape(KV_LORA, MIX_W)
    place = np.zeros((LANES, N_HEADS * LANES), np.float32)
    for h in range(N_HEADS):
        for r in range(MLA_ROPE):
            place[r, h * LANES + MLA_NOPE + r] = 1.0
    return wq.astype(BF16), wk.astype(BF16), wv.astype(BF16), jnp.asarray(place, BF16)


def _trunk(x, mods, layers, shared):
    b, seq, _ = x.shape
    t = b * seq
    x2d = x.reshape(t, D_MODEL)
    tab = _rope_tables(seq)
    cs, ss = _dft_tables(seq)
    for mod, lw in zip(mods, layers):
        ret, fu, dg0, dg1, dg2, mq, mk, mv = _inproj(x2d, mod, lw["g_pre_mix"], lw["w_in"], tab, lw["q_norm"],
                                                     lw["kv_norm"], lw["wq"], lw["wk"], lw["wv"],
                                                     shared["place"], seq)
        ro = _retention(ret.reshape(b, seq, 1024), lw["lg"]).reshape(t, MIX_W)
        fo = _fourier(fu.reshape(b, seq, MIX_W), cs, ss, shared["cc"], shared["sc"], lw["wf"]).reshape(t, MIX_W)
        d_o, d_l = zip(*[_dilated(dg, g) for g, dg in enumerate((dg0, dg1, dg2))])
        mo = _mla(mq.reshape(b, seq, 512), mk.reshape(b, seq, 512), mv.reshape(b, seq, MIX_W)).reshape(t, MIX_W)
        x1, h2 = _outproj(x2d, mod, ro, fo, d_o, d_l, mo, lw["w_out"], lw["g_post_mix"], lw["g_pre_ffn"], seq)
        x2d = _ffn(h2, x1, mod, lw["w_up"], lw["conv_w"], lw["conv_b"], lw["w_down"], lw["g_post_ffn"], seq)
    return x2d.reshape(b, seq, D_MODEL)


def kernel(x_prompt, x_sample, c_prompt, c_sample, w_ada, b_ada, norm_pre_mix, w_in, ret_decay_fwd,
           ret_decay_bwd, w_fmix, mla_q_norm, mla_w_qb, mla_kv_norm, mla_w_kvb, w_out, norm_post_mix,
           norm_pre_ffn, w_up, conv_w, conv_b, w_down, norm_post_ffn):
    depth = w_in.shape[0]
    nb_p, nb_s = c_prompt.shape[0], c_sample.shape[0]
    rows = -(-(nb_p + nb_s) // 8) * 8
    c_all = jnp.concatenate([c_prompt, c_sample, jnp.zeros((rows - nb_p - nb_s, D_MODEL), F32)], axis=0)
    mod_all = _ada(c_all, w_ada, b_ada)
    mods_p = [mod_all[l, :nb_p].reshape(nb_p, 6, D_MODEL) for l in range(depth)]
    mods_s = [mod_all[l, nb_p:nb_p + nb_s].reshape(nb_s, 6, D_MODEL) for l in range(depth)]

    c64 = 2.0 * np.pi * np.outer(np.arange(HEAD_DIM), np.arange(HEAD_DIM)) / HEAD_DIM
    shared = {
        "cc": _block_diag([jnp.asarray(np.cos(c64), BF16)] * N_HEADS),
        "sc": _block_diag([jnp.asarray(np.sin(c64), BF16)] * N_HEADS),
    }
    layers = []
    for l in range(depth):
        wq, wk, wv, place = _mla_weights(mla_w_qb[l], mla_w_kvb[l])
        shared["place"] = place
        layers.append({
            "g_pre_mix": norm_pre_mix[l][None, :],
            "w_in": _perm_w_in(w_in[l]),
            "lg": jnp.stack([jax.nn.log_sigmoid(ret_decay_fwd[l]), jax.nn.log_sigmoid(ret_decay_bwd[l])]),
            "wf": _block_diag([w_fmix[l, g] for g in range(N_HEADS)]).astype(BF16),
            "q_norm": mla_q_norm[l][None, :],
            "kv_norm": mla_kv_norm[l][None, :],
            "wq": wq, "wk": wk, "wv": wv,
            "w_out": w_out[l].astype(BF16),
            "g_post_mix": norm_post_mix[l][None, :],
            "g_pre_ffn": norm_pre_ffn[l][None, :],
            "w_up": w_up[l].astype(BF16),
            "conv_w": conv_w[l],
            "conv_b": conv_b[l][None, :],
            "w_down": w_down[l].astype(BF16),
            "g_post_ffn": norm_post_ffn[l][None, :],
        })
    y_prompt = _trunk(x_prompt, mods_p, layers, shared)
    y_sample = _trunk(x_sample, mods_s, layers, shared)
    return (y_prompt, y_sample)
```

```python
import functools
import math

import numpy as np
import jax
import jax.numpy as jnp
from jax import lax
from jax.experimental import pallas as pl
from jax.experimental.pallas import tpu as pltpu

F32 = jnp.float32
BF16 = jnp.bfloat16

D_MODEL = 1024
HEAD_DIM = 64
N_HEADS = 4
MIX_W = N_HEADS * HEAD_DIM
DIL_PAIRS = ((128, 1), (512, 4), (2048, 16))
N_DIL_GROUPS = 3
DIL_RADIUS = 64
STAT_W = 16
MLA_NOPE = 64
MLA_ROPE = 32
Q_LORA = 256
KV_LORA = 128
D_FF = 2816
ROPE_THETA = 500000.0
RET_THETA = 10000.0
PARTIAL_ROT = HEAD_DIM // 4
EPS = 1e-6
NEG = -1e30
LOG2E = math.log2(math.e)

LANES = 128
D_IN_PAD = 4096
MLA_OFF = 3584
ROW_TILE = 512
FFN_TILE = 1024
RET_CHUNK = 256
RET_UNROLL = 4
DIL_TQ = 128
DIL_STEP_ROWS = 1024
FOUR_TR = 1024
FLIP_BLOCK = 256
MLA_TQ = 256
MLA_STEP_SCORES = 2 ** 21
FFN_CHUNK = 256
HALO = 16
VMEM_LIMIT = 56 * 1024 * 1024


def _params(*sem):
    return pltpu.CompilerParams(dimension_semantics=sem, vmem_limit_bytes=VMEM_LIMIT)


def _rms(x, g):
    return x * lax.rsqrt(jnp.mean(x * x, axis=-1, keepdims=True) + EPS) * g


def _sigmoid(x):
    return 1.0 / (1.0 + jnp.exp(-x))


def _dot(a, b):
    return jnp.dot(a, b, preferred_element_type=F32)


def _dot_nt(a, b):
    return lax.dot_general(a, b, (((1,), (1,)), ((), ())), preferred_element_type=F32)


def _dot_tn(a, b):
    return lax.dot_general(a, b, (((0,), (0,)), ((), ())), preferred_element_type=F32)


def _ada_kernel(c_ref, w_ref, b_ref, o_ref):
    c = c_ref[...]
    cond = (c * _sigmoid(c)).astype(BF16)
    o_ref[0] = _dot(cond, w_ref[0].astype(BF16)) + b_ref[0]


def _ada(c_all, w_ada, b_ada):
    depth, _, n = w_ada.shape
    rows = c_all.shape[0]
    tn = 1536
    return pl.pallas_call(
        _ada_kernel,
        out_shape=jax.ShapeDtypeStruct((depth, rows, n), F32),
        grid=(depth, n // tn),
        in_specs=[
            pl.BlockSpec((rows, D_MODEL), lambda l, j: (0, 0)),
            pl.BlockSpec((1, D_MODEL, tn), lambda l, j: (l, 0, j)),
            pl.BlockSpec((1, 1, tn), lambda l, j: (l, 0, j)),
        ],
        out_specs=pl.BlockSpec((1, rows, tn), lambda l, j: (l, 0, j)),
        compiler_params=_params("arbitrary", "arbitrary"),
        name="ada",
    )(c_all, w_ada, b_ada.reshape(depth, 1, n))


def _inproj_kernel(x_ref, mod_ref, g_ref, w_ref, tab_ref, qn_ref, kvn_ref, wq_ref, wk_ref, wv_ref,
                   pk_ref, ret_ref, fu_ref, d0_ref, d1_ref, d2_ref, mq_ref, mk_ref, mv_ref, scr_ref, scr2_ref):
    x = x_ref[...]
    sh = mod_ref[0, 0:1, :]
    sc = mod_ref[0, 1:2, :]
    hb = (_rms(x, g_ref[...]) * (1.0 + sc) + sh).astype(BF16)

    def mm(c0, c1):
        return _dot(hb, w_ref[:, c0:c1])

    lane = lax.broadcasted_iota(jnp.int32, (1, LANES), 1)
    j64 = lane & (HEAD_DIM - 1)

    def make_rope(cos, sin, lo_mask, hi_mask, half):
        c = jnp.where(lo_mask | hi_mask, cos, 1.0)
        sa = jnp.where(lo_mask, -sin, 0.0)
        sb = jnp.where(hi_mask, sin, 0.0)

        def apply(z):
            return z * c + pltpu.roll(z, LANES - half, 1) * sa + pltpu.roll(z, half, 1) * sb
        return apply

    rope_ret = make_rope(tab_ref[:, 0:128], tab_ref[:, 128:256], j64 < 32, j64 >= 32, 32)
    rope_dil = make_rope(tab_ref[:, 256:384], tab_ref[:, 384:512], j64 < 8, (j64 >= 8) & (j64 < 16), 8)
    cos_m = tab_ref[:, 512:640]
    sin_m = tab_ref[:, 640:768]
    rope_kr = make_rope(cos_m, sin_m, lane < 16, (lane >= 16) & (lane < 32), 16)
    rope_mq = make_rope(cos_m, sin_m, (lane >= 64) & (lane < 80), (lane >= 80) & (lane < 96), 16)

    z = mm(MLA_OFF, D_IN_PAD)
    cqn = _rms(z[:, 0:Q_LORA], qn_ref[...]).astype(BF16)
    q = _dot(cqn, wq_ref[...])
    scale = (MLA_NOPE + MLA_ROPE) ** -0.5 * LOG2E
    for h in range(N_HEADS):
        r = rope_mq(q[:, h * LANES:(h + 1) * LANES]) * scale
        mq_ref[:, h * LANES:(h + 1) * LANES] = r.astype(BF16)
    ckvn = _rms(z[:, Q_LORA:Q_LORA + KV_LORA], kvn_ref[...]).astype(BF16)
    kr = rope_kr(z[:, 384:512]).astype(BF16)
    mk_ref[...] = (_dot(ckvn, wk_ref[...]) + _dot(kr, pk_ref[...])).astype(BF16)
    mv_ref[...] = _dot(ckvn, wv_ref[...]).astype(BF16)
    z = mm(0, 512)
    for c in range(4):
        r = rope_ret(z[:, c * LANES:(c + 1) * LANES])
        if c >= 2:
            r = r * (HEAD_DIM ** -0.5)
        ret_ref[:, c * LANES:(c + 1) * LANES] = r.astype(BF16)
    ret_ref[:, 512:1024] = mm(512, 1024).astype(BF16)
    fu_ref[...] = mm(1024, 1280).astype(BF16)
    tm = x.shape[0]
    nslab = 3 * MIX_W // LANES
    for g, d_ref in enumerate((d0_ref, d1_ref, d2_ref)):
        dil = DIL_PAIRS[g][1]
        base = g * 3 * MIX_W
        z = mm(1280 + base, 1280 + base + 3 * MIX_W)
        slabs = []
        for c in range(nslab):
            r = z[:, c * LANES:(c + 1) * LANES]
            if c < 4:
                r = rope_dil(r)
            if c < 2:
                r = r * (HEAD_DIM ** -0.5 * LOG2E)
            slabs.append(r)
        if dil == 1:
            d_ref[0, 0] = jnp.concatenate(slabs, axis=1).astype(BF16)
            continue
        for c in range(nslab):
            scr_ref[c] = slabs[c]
        n4 = tm // 4

        def rows4(ref, start, count):
            return jnp.concatenate([ref[c, pl.ds(start, count, stride=4), :] for c in range(nslab)], axis=1)

        if dil == 4:
            for r4 in range(4):
                d_ref[0, r4] = rows4(scr_ref, r4, n4).astype(BF16)
        else:
            for r4 in range(4):
                for c in range(nslab):
                    scr2_ref[c, r4 * n4:(r4 + 1) * n4, :] = scr_ref[c, pl.ds(r4, n4, stride=4), :]
            for r4 in range(4):
                for q4 in range(4):
                    d_ref[0, r4 + 4 * q4] = rows4(scr2_ref, r4 * n4 + q4, tm // 16).astype(BF16)


def _inproj(x2d, mod, g_pre, w_in_p, tab, qn, kvn, wq_p, wk_p, wv_p, pk, seq):
    t = x2d.shape[0]
    b = t // seq
    tm = ROW_TILE
    tps = seq // tm
    const = lambda i: (0, 0)
    row = lambda i: (i, 0)
    flat = [(t, 1024), (t, MIX_W)]
    flat2 = [(t, 512), (t, 512), (t, MIX_W)]
    dils = [d for _, d in DIL_PAIRS]
    assert dils == [1, 4, 16]
    bf = lambda shp: jax.ShapeDtypeStruct(shp, BF16)
    dil_shapes = [bf((b, d, seq // d, 3 * MIX_W)) for d in dils]
    dil_specs = [pl.BlockSpec((1, d, tm // d, 3 * MIX_W), lambda i: (i // tps, 0, i % tps, 0)) for d in dils]
    return pl.pallas_call(
        _inproj_kernel,
        out_shape=[bf(s) for s in flat] + dil_shapes + [bf(s) for s in flat2],
        grid=(t // tm,),
        in_specs=[
            pl.BlockSpec((tm, D_MODEL), row),
            pl.BlockSpec((1, 6, D_MODEL), lambda i: (i // tps, 0, 0)),
            pl.BlockSpec((1, D_MODEL), const),
            pl.BlockSpec((D_MODEL, D_IN_PAD), const),
            pl.BlockSpec((tm, 6 * LANES), lambda i: (i % tps, 0)),
            pl.BlockSpec((1, Q_LORA), const),
            pl.BlockSpec((1, KV_LORA), const),
            pl.BlockSpec((Q_LORA, 512), const),
            pl.BlockSpec((KV_LORA, 512), const),
            pl.BlockSpec((KV_LORA, MIX_W), const),
            pl.BlockSpec((LANES, 512), const),
        ],
        out_specs=[pl.BlockSpec((tm, s[1]), row) for s in flat] + dil_specs
                  + [pl.BlockSpec((tm, s[1]), row) for s in flat2],
        scratch_shapes=[pltpu.VMEM((3 * MIX_W // LANES, tm, LANES), F32),
                        pltpu.VMEM((3 * MIX_W // LANES, tm, LANES), F32)],
        compiler_params=_params("arbitrary"),
        name="inproj",
    )(x2d, mod, g_pre, w_in_p, tab, qn, kvn, wq_p, wk_p, wv_p, pk)


def _ret_kernel(lg_ref, q_ref, k_ref, v_ref, g_ref, o_ref, acc_ref, st_ref, dmat_ref, vec_ref, rdec_ref):
    c = RET_CHUNK
    seq = q_ref.shape[1]
    n_chunks = seq // c
    lane_head = lax.broadcasted_iota(jnp.int32, (1, MIX_W), 1) // HEAD_DIM
    row_head = lax.broadcasted_iota(jnp.int32, (MIX_W, 1), 0) // HEAD_DIM
    blockdiag = row_head == lane_head

    def per_head(idx, d):
        out = lg_ref[d, 0]
        for h in range(1, N_HEADS):
            out = jnp.where(idx == h, lg_ref[d, h], out)
        return out

    @pl.when(pl.program_id(0) == 0)
    def _tables():
        ri = lax.broadcasted_iota(jnp.int32, (c, c), 0)
        ci = lax.broadcasted_iota(jnp.int32, (c, c), 1)
        diff = (ri - ci).astype(F32)
        for h in range(N_HEADS):
            fwd = jnp.exp(jnp.where(diff >= 0, diff, 0.0) * lg_ref[0, h])
            bwd = jnp.exp(jnp.where(diff < 0, -diff, 0.0) * lg_ref[1, h])
            dmat_ref[h] = jnp.where(diff >= 0, fwd, bwd)
        pos = lax.broadcasted_iota(jnp.int32, (c, MIX_W), 0).astype(F32)
        lf = per_head(lane_head, 0)
        lb = per_head(lane_head, 1)
        vec_ref[0] = jnp.exp((pos + 1.0) * lf)
        vec_ref[1] = jnp.exp((c - 1.0 - pos) * lf)
        vec_ref[2] = jnp.exp((c - pos) * lb)
        vec_ref[3] = jnp.exp(pos * lb)
        rdec_ref[0] = jnp.broadcast_to(jnp.exp(c * per_head(row_head, 0)), (MIX_W, MIX_W))
        rdec_ref[1] = jnp.broadcast_to(jnp.exp(c * per_head(row_head, 1)), (MIX_W, MIX_W))

    ones_bd = jnp.where(blockdiag, 1.0, 0.0).astype(BF16)

    def chunk(ref, n):
        return ref[0, pl.ds(pl.multiple_of(n * c, c), c), :]

    def fwd_body(n, carry):
        qn, kn, vn = chunk(q_ref, n), chunk(k_ref, n), chunk(v_ref, n)
        acc = _dot((qn.astype(F32) * vec_ref[0]).astype(BF16), st_ref[...].astype(BF16))
        for h in range(N_HEADS):
            hm = lane_head == h
            s = _dot_nt(jnp.where(hm, qn, jnp.zeros_like(qn)), kn)
            p = (s * dmat_ref[h]).astype(BF16)
            acc = acc + _dot(p, jnp.where(hm, vn, jnp.zeros_like(vn)))
        acc_ref[pl.ds(pl.multiple_of(n * c, c), c), :] = acc
        kv = _dot_tn((kn.astype(F32) * vec_ref[1]).astype(BF16), vn)
        st_ref[...] = st_ref[...] * rdec_ref[0] + jnp.where(blockdiag, kv, 0.0)
        return carry

    st_ref[...] = jnp.zeros_like(st_ref)
    lax.fori_loop(0, n_chunks, fwd_body, 0, unroll=RET_UNROLL)

    def bwd_body(t, carry):
        n = n_chunks - 1 - t
        qn, kn, vn = chunk(q_ref, n), chunk(k_ref, n), chunk(v_ref, n)
        r0 = pl.multiple_of(n * c, c)
        o = acc_ref[pl.ds(r0, c), :] + _dot((qn.astype(F32) * vec_ref[2]).astype(BF16),
                                           st_ref[...].astype(BF16))
        o2 = o * o
        hi = o2.astype(BF16)
        lo = (o2 - hi.astype(F32)).astype(BF16)
        ms = (_dot(hi, ones_bd) + _dot(lo, ones_bd)) * (1.0 / HEAD_DIM)
        gate = chunk(g_ref, n).astype(F32)
        o_ref[0, pl.ds(r0, c), :] = (gate * _sigmoid(gate) * (o * lax.rsqrt(ms + EPS))).astype(BF16)
        kv = _dot_tn((kn.astype(F32) * vec_ref[3]).astype(BF16), vn)
        st_ref[...] = st_ref[...] * rdec_ref[1] + jnp.where(blockdiag, kv, 0.0)
        return carry

    st_ref[...] = jnp.zeros_like(st_ref)
    lax.fori_loop(0, n_chunks, bwd_body, 0, unroll=RET_UNROLL)


def _retention(ret3d, lg):
    b, seq, _ = ret3d.shape
    spec = lambda col: pl.BlockSpec((1, seq, MIX_W), lambda i, col=col: (i, 0, col))
    return pl.pallas_call(
        _ret_kernel,
        out_shape=jax.ShapeDtypeStruct((b, seq, MIX_W), BF16),
        grid=(b,),
        in_specs=[pl.BlockSpec(memory_space=pltpu.SMEM), spec(0), spec(1), spec(2), spec(3)],
        out_specs=pl.BlockSpec((1, seq, MIX_W), lambda i: (i, 0, 0)),
        scratch_shapes=[
            pltpu.VMEM((seq, MIX_W), F32),
            pltpu.VMEM((MIX_W, MIX_W), F32),
            pltpu.VMEM((N_HEADS, RET_CHUNK, RET_CHUNK), F32),
            pltpu.VMEM((4, RET_CHUNK, MIX_W), F32),
            pltpu.VMEM((2, MIX_W, MIX_W), F32),
        ],
        compiler_params=_params("arbitrary"),
        name="retention",
    )(lg, ret3d, ret3d, ret3d, ret3d)


def _fourier_kernel(cs_ref, csh_ref, ss_ref, ssh_ref, u_ref, cc_ref, sc_ref, wf_ref, lo_ref, hi_ref, *, scale):
    half = u_ref.shape[1] // 2
    tr = cs_ref.shape[0]
    blk = FLIP_BLOCK
    ri = lax.broadcasted_iota(jnp.int32, (blk, blk), 0)
    ci = lax.broadcasted_iota(jnp.int32, (blk, blk), 1)
    flip_shift = jnp.where(ri + ci == blk, 1.0, 0.0).astype(BF16)
    flip = jnp.where(ri + ci == blk - 1, 1.0, 0.0).astype(BF16)
    first = lax.broadcasted_iota(jnp.int32, (blk, 1), 0) == 0
    nb = half // blk
    upper = lambda c: u_ref[0, half + c * blk:half + (c + 1) * blk, :]
    plus, minus = [], []
    for a in range(nb):
        row0 = upper(nb - a)[0:1].astype(F32) if a > 0 else jnp.zeros((1, MIX_W), F32)
        rev = jnp.where(first, row0, _dot(flip_shift, upper(nb - a - 1)))
        lo = u_ref[0, a * blk:(a + 1) * blk, :].astype(F32)
        plus.append((lo + rev).astype(BF16))
        minus.append((lo - rev).astype(BF16))
    u_mid = u_ref[0, half:half + 1, :].astype(F32)
    ext = tr + HALO
    row = lax.broadcasted_iota(jnp.int32, (ext, 1), 0)
    alt = jnp.where((row & 1) == 0, 1.0, -1.0)
    cs = jnp.concatenate([cs_ref[...], csh_ref[...]], axis=0)
    ss = jnp.concatenate([ss_ref[...], ssh_ref[...]], axis=0)
    z1 = (_dot(cs, jnp.concatenate(plus, axis=0)) + alt * u_mid).astype(BF16)
    z2 = _dot(ss, jnp.concatenate(minus, axis=0)).astype(BF16)
    a1 = _dot(z1, cc_ref[...])
    a2 = _dot(z2, sc_ref[...])
    lo_ref[0] = _dot(((a1 - a2)[0:tr] * scale).astype(BF16), wf_ref[...]).astype(BF16)
    mirrored = pltpu.roll(a1 + a2, ext - 1, 0)[0:tr]
    hi = _dot((mirrored * scale).astype(BF16), wf_ref[...]).astype(BF16)
    nbt = tr // blk
    for c in range(nbt):
        src = hi[(nbt - 1 - c) * blk:(nbt - c) * blk]
        hi_ref[0, c * blk:(c + 1) * blk, :] = _dot(flip, src).astype(BF16)


def _fourier(fu3d, cs, ss, cc, sc, wf):
    b, seq, _ = fu3d.shape
    half = seq // 2
    tr = min(FOUR_TR, half)
    steps = half // tr
    const = lambda i, j: (0, 0)
    main = pl.BlockSpec((tr, half), lambda i, j: (i, 0))
    halo = pl.BlockSpec((HALO, half), lambda i, j: ((i + 1) * (tr // HALO), 0))
    return pl.pallas_call(
        functools.partial(_fourier_kernel, scale=1.0 / math.sqrt(seq * HEAD_DIM)),
        out_shape=[jax.ShapeDtypeStruct((b, half, MIX_W), BF16), jax.ShapeDtypeStruct((b, half, MIX_W), BF16)],
        grid=(steps, b),
        in_specs=[
            main, halo, main, halo,
            pl.BlockSpec((1, seq, MIX_W), lambda i, j: (j, 0, 0)),
            pl.BlockSpec((MIX_W, MIX_W), const),
            pl.BlockSpec((MIX_W, MIX_W), const),
            pl.BlockSpec((MIX_W, MIX_W), const),
        ],
        out_specs=[pl.BlockSpec((1, tr, MIX_W), lambda i, j: (j, i, 0)),
                   pl.BlockSpec((1, tr, MIX_W), lambda i, j: (j, steps - 1 - i, 0))],
        compiler_params=_params("arbitrary", "arbitrary"),
        name="fourier",
    )(cs, cs, ss, ss, fu3d, cc, sc, wf)


def _dil_kernel(q_ref, k_ref, v_ref, o_ref, st_ref, bias_ref, *, sub_len, ts, rb, tq, win):
    j = pl.program_id(2)
    nblk = ts // tq
    lane_head = lax.broadcasted_iota(jnp.int32, (1, MIX_W), 1) // HEAD_DIM
    stat_slot = lax.broadcasted_iota(jnp.int32, (1, LANES), 1) // STAT_W

    @pl.when((pl.program_id(0) == 0) & (pl.program_id(1) == 0) & (j == 0))
    def _bias():
        rel = (lax.broadcasted_iota(jnp.int32, (N_HEADS * tq, win), 0) & (tq - 1)) \
            - lax.broadcasted_iota(jnp.int32, (N_HEADS * tq, win), 1)
        for i in range(3):
            bias_ref[i] = jnp.where(jnp.abs(rel + i * DIL_RADIUS) <= DIL_RADIUS, 0.0, NEG)

    for rr in range(rb):
        for blk in range(nblk):
            q0 = j * ts + blk * tq
            if win == sub_len:
                ws = 0
                bias = bias_ref[blk * tq // DIL_RADIUS]
            else:
                ws = pl.multiple_of(jnp.clip(q0 - DIL_RADIUS, 0, sub_len - win), DIL_RADIUS)
                bias = bias_ref[1] if 0 < blk < nblk - 1 else bias_ref[(q0 - ws) // DIL_RADIUS]
            q = q_ref[0, rr, blk * tq:(blk + 1) * tq, :]
            kw = k_ref[0, rr, pl.ds(ws, win), :]
            vw = v_ref[0, rr, pl.ds(ws, win), :]
            zero = jnp.zeros_like(q)
            qs = jnp.concatenate([jnp.where(lane_head == h, q, zero) for h in range(N_HEADS)], axis=0)
            s = _dot_nt(qs, kw) + bias
            m = jnp.max(s, axis=-1, keepdims=True)
            p = jnp.exp2(s - m)
            den = jnp.sum(p, axis=-1, keepdims=True)
            r = _dot(p.astype(BF16), vw)
            o = jnp.zeros((tq, MIX_W), F32)
            st = jnp.zeros((tq, LANES), F32)
            for h in range(N_HEADS):
                o = jnp.where(lane_head == h, r[h * tq:(h + 1) * tq], o)
                st = jnp.where(stat_slot == 2 * h, m[h * tq:(h + 1) * tq], st)
                st = jnp.where(stat_slot == 2 * h + 1, den[h * tq:(h + 1) * tq], st)
            o_ref[0, rr, blk * tq:(blk + 1) * tq, :] = o.astype(BF16)
            st_ref[0, rr, blk * tq:(blk + 1) * tq, :] = st


def _dilated(dg, group):
    b, dil, sub_len, _ = dg.shape
    tq = min(DIL_TQ, sub_len)
    win = min(tq + 2 * DIL_RADIUS, sub_len)
    ts = min(sub_len, DIL_STEP_ROWS)
    rb = min(dil, DIL_STEP_ROWS // ts)
    part = lambda c: (lambda i, r, j: (i, r, 0, c))
    return pl.pallas_call(
        functools.partial(_dil_kernel, sub_len=sub_len, ts=ts, rb=rb, tq=tq, win=win),
        out_shape=[jax.ShapeDtypeStruct((b, dil, sub_len, MIX_W), BF16),
                   jax.ShapeDtypeStruct((b, dil, sub_len, LANES), F32)],
        grid=(b, dil // rb, sub_len // ts),
        in_specs=[
            pl.BlockSpec((1, rb, ts, MIX_W), lambda i, r, j: (i, r, j, 0)),
            pl.BlockSpec((1, rb, sub_len, MIX_W), part(1)),
            pl.BlockSpec((1, rb, sub_len, MIX_W), part(2)),
        ],
        out_specs=[pl.BlockSpec((1, rb, ts, MIX_W), lambda i, r, j: (i, r, j, 0)),
                   pl.BlockSpec((1, rb, ts, LANES), lambda i, r, j: (i, r, j, 0))],
        scratch_shapes=[pltpu.VMEM((3, N_HEADS * tq, win), F32)],
        compiler_params=_params("arbitrary", "arbitrary", "arbitrary"),
        name=f"dilated{group}",
    )(dg, dg, dg)


def _mla_kernel(q_ref, k_ref, v_ref, o_ref):
    lane_head = lax.broadcasted_iota(jnp.int32, (1, MIX_W), 1) // HEAD_DIM
    v = v_ref[0]
    for b0 in range(0, q_ref.shape[1], MLA_TQ):
        out = jnp.zeros((MLA_TQ, MIX_W), F32)
        for h in range(N_HEADS):
            qh = q_ref[0, b0:b0 + MLA_TQ, h * LANES:(h + 1) * LANES]
            s = _dot_nt(qh, k_ref[0, :, h * LANES:(h + 1) * LANES])
            m = jnp.max(s, axis=-1, keepdims=True)
            p = jnp.exp2(s - m)
            den = jnp.sum(p, axis=-1, keepdims=True)
            o = _dot(p.astype(BF16), v) * (1.0 / den)
            out = jnp.where(lane_head == h, o, out)
        o_ref[0, b0:b0 + MLA_TQ, :] = out.astype(BF16)


def _mla(mq3d, mk3d, mv3d):
    b, seq, _ = mq3d.shape
    ts = max(MLA_TQ, min(seq, MLA_STEP_SCORES // seq))
    return pl.pallas_call(
        _mla_kernel,
        out_shape=jax.ShapeDtypeStruct((b, seq, MIX_W), BF16),
        grid=(b, seq // ts),
        in_specs=[
            pl.BlockSpec((1, ts, N_HEADS * LANES), lambda i, j: (i, j, 0)),
            pl.BlockSpec((1, seq, N_HEADS * LANES), lambda i, j: (i, 0, 0)),
            pl.BlockSpec((1, seq, MIX_W), lambda i, j: (i, 0, 0)),
        ],
        out_specs=pl.BlockSpec((1, ts, MIX_W), lambda i, j: (i, j, 0)),
        compiler_params=_params("arbitrary", "arbitrary"),
        name="mla",
    )(mq3d, mk3d, mv3d)


def _outproj_kernel(x_ref, mod_ref, ro_ref, flo_ref, fhi_ref, d0_ref, d1_ref, d2_ref, l0_ref, l1_ref, l2_ref,
                    mo_ref, w_ref, gpm_ref, gpf_ref, x1_ref, h2_ref, so1_ref, sl1_ref, so2_ref, sl2_ref,
                    *, tiles_per_seq):
    tm = x_ref.shape[0]
    in_lower = (pl.program_id(0) % tiles_per_seq) < tiles_per_seq // 2
    fo = jnp.where(in_lower, flo_ref[0], fhi_ref[0])

    def natural_order(o_ref, l_ref, so_ref, sl_ref):
        dil = o_ref.shape[1]
        if dil == 1:
            return o_ref[0, 0].astype(F32), l_ref[0, 0]
        n = tm // dil
        for r in range(dil):
            o = o_ref[0, r].astype(F32)
            for c in range(MIX_W // LANES):
                so_ref[c, pl.ds(r, n, stride=dil), :] = o[:, c * LANES:(c + 1) * LANES]
            sl_ref[pl.ds(r, n, stride=dil), :] = l_ref[0, r]
        return jnp.concatenate([so_ref[c] for c in range(MIX_W // LANES)], axis=1), sl_ref[...]

    o0, l0 = natural_order(d0_ref, l0_ref, None, None)
    o1, l1 = natural_order(d1_ref, l1_ref, so1_ref, sl1_ref)
    o2, l2 = natural_order(d2_ref, l2_ref, so2_ref, sl2_ref)
    m = jnp.maximum(l0, jnp.maximum(l1, l2))
    e = [jnp.exp2(l - m) for l in (l0, l1, l2)]
    den = sum(eg * pltpu.roll(l, LANES - STAT_W, 1) for eg, l in zip(e, (l0, l1, l2)))
    inv = 1.0 / den
    is_max_lane = (lax.broadcasted_iota(jnp.int32, (1, LANES), 1) & STAT_W) == 0
    src = lax.broadcasted_iota(jnp.int32, (LANES, MIX_W), 0)
    dst = lax.broadcasted_iota(jnp.int32, (LANES, MIX_W), 1)
    spread = jnp.where(src == (dst // HEAD_DIM) * (LANES // N_HEADS), 1.0, 0.0).astype(BF16)
    od = sum(_dot(jnp.where(is_max_lane, eg * inv, 0.0).astype(BF16), spread) * og
             for eg, og in zip(e, (o0, o1, o2))).astype(BF16)
    y = (_dot(ro_ref[...], w_ref[0:256, :]) + _dot(fo, w_ref[256:512, :])
         + _dot(od, w_ref[512:768, :]) + _dot(mo_ref[...], w_ref[768:1024, :]))
    g1 = mod_ref[0, 2:3, :]
    sh2 = mod_ref[0, 3:4, :]
    sc2 = mod_ref[0, 4:5, :]
    x1 = x_ref[...] + g1 * _rms(y, gpm_ref[...])
    x1_ref[...] = x1
    h2_ref[...] = (_rms(x1, gpf_ref[...]) * (1.0 + sc2) + sh2).astype(BF16)


def _outproj(x2d, mod, ro, fo_halves, d_o, d_l, mo, w_out, g_post_mix, g_pre_ffn, seq):
    t = x2d.shape[0]
    tm = ROW_TILE
    tps = seq // tm
    row = lambda i: (i, 0)
    const = lambda i: (0, 0)
    mix = pl.BlockSpec((tm, MIX_W), row)
    res = lambda a: pl.BlockSpec((1, a.shape[1], tm // a.shape[1], a.shape[3]), lambda i: (i // tps, 0, i % tps, 0))
    hps = tps // 2
    flo = pl.BlockSpec((1, tm, MIX_W), lambda i: (i // tps, jnp.minimum(i % tps, hps - 1), 0))
    fhi = pl.BlockSpec((1, tm, MIX_W), lambda i: (i // tps, jnp.maximum(i % tps - hps, 0), 0))
    return pl.pallas_call(
        functools.partial(_outproj_kernel, tiles_per_seq=tps),
        out_shape=[jax.ShapeDtypeStruct((t, D_MODEL), F32), jax.ShapeDtypeStruct((t, D_MODEL), BF16)],
        grid=(t // tm,),
        in_specs=[
            pl.BlockSpec((tm, D_MODEL), row),
            pl.BlockSpec((1, 6, D_MODEL), lambda i: (i // tps, 0, 0)),
            mix, flo, fhi, res(d_o[0]), res(d_o[1]), res(d_o[2]), res(d_l[0]), res(d_l[1]), res(d_l[2]), mix,
            pl.BlockSpec((D_MODEL, D_MODEL), const),
            pl.BlockSpec((1, D_MODEL), const),
            pl.BlockSpec((1, D_MODEL), const),
        ],
        out_specs=[pl.BlockSpec((tm, D_MODEL), row), pl.BlockSpec((tm, D_MODEL), row)],
        scratch_shapes=[pltpu.VMEM((MIX_W // LANES, tm, LANES), F32), pltpu.VMEM((tm, LANES), F32),
                        pltpu.VMEM((MIX_W // LANES, tm, LANES), F32), pltpu.VMEM((tm, LANES), F32)],
        compiler_params=_params("arbitrary"),
        name="outproj",
    )(x2d, mod, ro, fo_halves[0], fo_halves[1], d_o[0], d_o[1], d_o[2], d_l[0], d_l[1], d_l[2], mo, w_out,
      g_post_mix, g_pre_ffn)


def _ffn_kernel(hp_ref, hc_ref, hn_ref, x1_ref, mod_ref, wu_ref, cw_ref, cb_ref, wd_ref, g_ref, o_ref,
                gate_ref, *, tiles_per_seq):
    tm = hc_ref.shape[0]
    t = pl.program_id(0) % tiles_per_seq
    hp = jnp.where(t == 0, jnp.zeros_like(hp_ref[...]), hp_ref[...])
    hn = jnp.where(t == tiles_per_seq - 1, jnp.zeros_like(hn_ref[...]), hn_ref[...])
    he = jnp.concatenate([hp, hc_ref[...], hn], axis=0)

    ext = tm + 2 * HALO

    def conv(c0):
        u = _dot(he, wu_ref[:, c0:c0 + FFN_CHUNK])
        w = cw_ref[:, c0:c0 + FFN_CHUNK]
        prev = pltpu.roll(u, 1, 0)[HALO:HALO + tm]
        nxt = pltpu.roll(u, ext - 1, 0)[HALO:HALO + tm]
        return prev * w[0:1] + u[HALO:HALO + tm] * w[1:2] + nxt * w[2:3] + cb_ref[:, c0:c0 + FFN_CHUNK]

    for c in range(D_FF // FFN_CHUNK):
        a = conv(c * FFN_CHUNK)
        bu = conv(D_FF + c * FFN_CHUNK)
        gate_ref[:, c * FFN_CHUNK:(c + 1) * FFN_CHUNK] = (a * _sigmoid(a) * bu).astype(BF16)
    acc = _dot(gate_ref[...], wd_ref[...])
    g2 = mod_ref[0, 5:6, :]
    o_ref[...] = x1_ref[...] + g2 * _rms(acc, g_ref[...])


def _ffn(h2, x1, mod, w_up, conv_w, conv_b, w_down, g_post_ffn, seq):
    t = x1.shape[0]
    tm = FFN_TILE
    tps = seq // tm
    hb = tm // HALO
    row = lambda i: (i, 0)
    const = lambda i: (0, 0)
    resident = lambda shape: pl.BlockSpec(shape, const, pipeline_mode=pl.Buffered(1))
    return pl.pallas_call(
        functools.partial(_ffn_kernel, tiles_per_seq=tps),
        out_shape=jax.ShapeDtypeStruct((t, D_MODEL), F32),
        grid=(t // tm,),
        in_specs=[
            pl.BlockSpec((HALO, D_MODEL), lambda i: (jnp.maximum(i * hb - 1, 0), 0)),
            pl.BlockSpec((tm, D_MODEL), row),
            pl.BlockSpec((HALO, D_MODEL), lambda i: (jnp.minimum((i + 1) * hb, t // HALO - 1), 0)),
            pl.BlockSpec((tm, D_MODEL), row),
            pl.BlockSpec((1, 6, D_MODEL), lambda i: (i // tps, 0, 0)),
            resident((D_MODEL, 2 * D_FF)),
            pl.BlockSpec((3, 2 * D_FF), const),
            pl.BlockSpec((1, 2 * D_FF), const),
            resident((D_FF, D_MODEL)),
            pl.BlockSpec((1, D_MODEL), const),
        ],
        out_specs=pl.BlockSpec((tm, D_MODEL), row),
        scratch_shapes=[pltpu.VMEM((tm, D_FF), BF16)],
        compiler_params=_params("arbitrary"),
        name="ffn",
    )(h2, h2, h2, x1, mod, w_up, conv_w, conv_b, w_down, g_post_ffn)


def _rope_tables(seq):
    pos = jnp.arange(seq, dtype=F32)[:, None]
    lane = np.arange(LANES)
    cols = []
    for theta, rot in ((RET_THETA, HEAD_DIM), (ROPE_THETA, PARTIAL_ROT), (ROPE_THETA, MLA_ROPE)):
        half = rot // 2
        inv = jnp.power(theta, -jnp.arange(half, dtype=F32) * 2.0 / rot)
        ang = pos * inv[lane % half][None, :]
        cols += [jnp.cos(ang), jnp.sin(ang)]
    return jnp.concatenate(cols, axis=1)


def _dft_tables(seq):
    n2 = 64
    n1 = seq // (2 * n2)
    k = np.arange(seq)[:, None]
    a = 2.0 * np.pi * ((k * np.arange(n1)[None, :] * n2) % seq) / seq
    b = 2.0 * np.pi * ((k * np.arange(n2)[None, :]) % seq) / seq
    ca, sa = jnp.asarray(np.cos(a), F32)[:, :, None], jnp.asarray(np.sin(a), F32)[:, :, None]
    cb, sb = jnp.asarray(np.cos(b), F32)[:, None, :], jnp.asarray(np.sin(b), F32)[:, None, :]
    cs = (ca * cb - sa * sb).reshape(seq, seq // 2).astype(BF16)
    ss = (sa * cb + ca * sb).reshape(seq, seq // 2).astype(BF16)
    return cs, ss


def _block_diag(blocks):
    n = len(blocks)
    rows = [jnp.concatenate([blocks[i] if i == j else jnp.zeros_like(blocks[i]) for j in range(n)], axis=1)
            for i in range(n)]
    return jnp.concatenate(rows, axis=0)


def _perm_w_in(w_in):
    ret = w_in[:, 0:1280]
    dq, dk, dv = w_in[:, 1280:2048], w_in[:, 2048:2816], w_in[:, 2816:3584]
    groups = [jnp.concatenate([m[:, g * MIX_W:(g + 1) * MIX_W] for m in (dq, dk, dv)], axis=1)
              for g in range(N_DIL_GROUPS)]
    pad = jnp.zeros((D_MODEL, D_IN_PAD - w_in.shape[1]), w_in.dtype)
    return jnp.concatenate([ret] + groups + [w_in[:, 3584:], pad], axis=1).astype(BF16)


def _mla_weights(w_qb, w_kvb):
    qh = w_qb.reshape(Q_LORA, N_HEADS, MLA_NOPE + MLA_ROPE)
    wq = jnp.pad(qh, ((0, 0), (0, 0), (0, LANES - MLA_NOPE - MLA_ROPE))).reshape(Q_LORA, N_HEADS * LANES)
    kvh = w_kvb.reshape(KV_LORA, N_HEADS, MLA_NOPE + HEAD_DIM)
    wk = jnp.pad(kvh[:, :, :MLA_NOPE], ((0, 0), (0, 0), (0, LANES - MLA_NOPE))).reshape(KV_LORA, N_HEADS * LANES)
    wv = kvh[:, :, MLA_NOPE:].reshape(KV_LORA, MIX_W)
    place = np.zeros((LANES, N_HEADS * LANES), np.float32)
    for h in range(N_HEADS):
        for r in range(MLA_ROPE):
            place[r, h * LANES + MLA_NOPE + r] = 1.0
    return wq.astype(BF16), wk.astype(BF16), wv.astype(BF16), jnp.asarray(place, BF16)


def _trunk(x, mods, layers, shared):
    b, seq, _ = x.shape
    t = b * seq
    x2d = x.reshape(t, D_MODEL)
    tab = _rope_tables(seq)
    cs, ss = _dft_tables(seq)
    for mod, lw in zip(mods, layers):
        ret, fu, dg0, dg1, dg2, mq, mk, mv = _inproj(x2d, mod, lw["g_pre_mix"], lw["w_in"], tab, lw["q_norm"],
                                                     lw["kv_norm"], lw["wq"], lw["wk"], lw["wv"],
                                                     shared["place"], seq)
        ro = _retention(ret.reshape(b, seq, 1024), lw["lg"]).reshape(t, MIX_W)
        fo = _fourier(fu.reshape(b, seq, MIX_W), cs, ss, shared["cc"], shared["sc"], lw["wf"])
        d_o, d_l = zip(*[_dilated(dg, g) for g, dg in enumerate((dg0, dg1, dg2))])
        mo = _mla(mq.reshape(b, seq, 512), mk.reshape(b, seq, 512), mv.reshape(b, seq, MIX_W)).reshape(t, MIX_W)
        x1, h2 = _outproj(x2d, mod, ro, fo, d_o, d_l, mo, lw["w_out"], lw["g_post_mix"], lw["g_pre_ffn"], seq)
        x2d = _ffn(h2, x1, mod, lw["w_up"], lw["conv_w"], lw["conv_b"], lw["w_down"], lw["g_post_ffn"], seq)
    return x2d.reshape(b, seq, D_MODEL)


def kernel(x_prompt, x_sample, c_prompt, c_sample, w_ada, b_ada, norm_pre_mix, w_in, ret_decay_fwd,
           ret_decay_bwd, w_fmix, mla_q_norm, mla_w_qb, mla_kv_norm, mla_w_kvb, w_out, norm_post_mix,
           norm_pre_ffn, w_up, conv_w, conv_b, w_down, norm_post_ffn):
    depth = w_in.shape[0]
    nb_p, nb_s = c_prompt.shape[0], c_sample.shape[0]
    rows = -(-(nb_p + nb_s) // 8) * 8
    c_all = jnp.concatenate([c_prompt, c_sample, jnp.zeros((rows - nb_p - nb_s, D_MODEL), F32)], axis=0)
    mod_all = _ada(c_all, w_ada, b_ada)
    mods_p = [mod_all[l, :nb_p].reshape(nb_p, 6, D_MODEL) for l in range(depth)]
    mods_s = [mod_all[l, nb_p:nb_p + nb_s].reshape(nb_s, 6, D_MODEL) for l in range(depth)]

    c64 = 2.0 * np.pi * np.outer(np.arange(HEAD_DIM), np.arange(HEAD_DIM)) / HEAD_DIM
    shared = {
        "cc": _block_diag([jnp.asarray(np.cos(c64), BF16)] * N_HEADS),
        "sc": _block_diag([jnp.asarray(np.sin(c64), BF16)] * N_HEADS),
    }
    layers = []
    for l in range(depth):
        wq, wk, wv, place = _mla_weights(mla_w_qb[l], mla_w_kvb[l])
        shared["place"] = place
        layers.append({
            "g_pre_mix": norm_pre_mix[l][None, :],
            "w_in": _perm_w_in(w_in[l]),
            "lg": jnp.stack([jax.nn.log_sigmoid(ret_decay_fwd[l]), jax.nn.log_sigmoid(ret_decay_bwd[l])]),
            "wf": _block_diag([w_fmix[l, g] for g in range(N_HEADS)]).astype(BF16),
            "q_norm": mla_q_norm[l][None, :],
            "kv_norm": mla_kv_norm[l][None, :],
            "wq": wq, "wk": wk, "wv": wv,
            "w_out": w_out[l].astype(BF16),
            "g_post_mix": norm_post_mix[l][None, :],
            "g_pre_ffn": norm_pre_ffn[l][None, :],
            "w_up": w_up[l].astype(BF16),
            "conv_w": conv_w[l],
            "conv_b": conv_b[l][None, :],
            "w_down": w_down[l].astype(BF16),
            "g_post_ffn": norm_post_ffn[l][None, :],
        })
    y_prompt = _trunk(x_prompt, mods_p, layers, shared)
    y_sample = _trunk(x_sample, mods_s, layers, shared)
    return (y_prompt, y_sample)
```

```python
import functools
import math

import numpy as np
import jax
import jax.numpy as jnp
from jax import lax
from jax.experimental import pallas as pl
from jax.experimental.pallas import tpu as pltpu

F32 = jnp.float32
BF16 = jnp.bfloat16

D_MODEL = 1024
HEAD_DIM = 64
N_HEADS = 4
MIX_W = N_HEADS * HEAD_DIM
DIL_PAIRS = ((128, 1), (512, 4), (2048, 16))
N_DIL_GROUPS = 3
DIL_RADIUS = 64
STAT_W = 16
MLA_NOPE = 64
MLA_ROPE = 32
Q_LORA = 256
KV_LORA = 128
D_FF = 2816
ROPE_THETA = 500000.0
RET_THETA = 10000.0
PARTIAL_ROT = HEAD_DIM // 4
EPS = 1e-6
NEG = -1e30
LOG2E = math.log2(math.e)

LANES = 128
D_IN_PAD = 4096
MLA_OFF = 3584
ROW_TILE = 512
FFN_TILE = 1024
RET_CHUNK = 256
RET_UNROLL = 8
DIL_TQ = 128
DIL_STEP_ROWS = 2048
FOUR_TR = 1024
FLIP_BLOCK = 256
MLA_TQ = 256
MLA_STEP_SCORES = 2 ** 22
FFN_CHUNK = 256
HALO = 16
VMEM_LIMIT = 56 * 1024 * 1024


def _params(*sem):
    return pltpu.CompilerParams(dimension_semantics=sem, vmem_limit_bytes=VMEM_LIMIT)


def _rms(x, g):
    return x * lax.rsqrt(jnp.mean(x * x, axis=-1, keepdims=True) + EPS) * g


def _sigmoid(x):
    return 1.0 / (1.0 + jnp.exp(-x))


def _dot(a, b):
    return jnp.dot(a, b, preferred_element_type=F32)


def _dot_nt(a, b):
    return lax.dot_general(a, b, (((1,), (1,)), ((), ())), preferred_element_type=F32)


def _dot_tn(a, b):
    return lax.dot_general(a, b, (((0,), (0,)), ((), ())), preferred_element_type=F32)


def _ada_kernel(c_ref, w_ref, b_ref, o_ref):
    c = c_ref[...]
    cond = (c * _sigmoid(c)).astype(BF16)
    o_ref[0] = _dot(cond, w_ref[0].astype(BF16)) + b_ref[0]


def _ada(c_all, w_ada, b_ada):
    depth, _, n = w_ada.shape
    rows = c_all.shape[0]
    tn = 1536
    return pl.pallas_call(
        _ada_kernel,
        out_shape=jax.ShapeDtypeStruct((depth, rows, n), F32),
        grid=(depth, n // tn),
        in_specs=[
            pl.BlockSpec((rows, D_MODEL), lambda l, j: (0, 0)),
            pl.BlockSpec((1, D_MODEL, tn), lambda l, j: (l, 0, j)),
            pl.BlockSpec((1, 1, tn), lambda l, j: (l, 0, j)),
        ],
        out_specs=pl.BlockSpec((1, rows, tn), lambda l, j: (l, 0, j)),
        compiler_params=_params("arbitrary", "arbitrary"),
        name="ada",
    )(c_all, w_ada, b_ada.reshape(depth, 1, n))


def _inproj_kernel(x_ref, mod_ref, g_ref, w_ref, tab_ref, qn_ref, kvn_ref, wq_ref, wk_ref, wv_ref,
                   pk_ref, ret_ref, fu_ref, d0_ref, d1_ref, d2_ref, mq_ref, mk_ref, mv_ref, scr_ref, scr2_ref):
    x = x_ref[...]
    sh = mod_ref[0, 0:1, :]
    sc = mod_ref[0, 1:2, :]
    hb = (_rms(x, g_ref[...]) * (1.0 + sc) + sh).astype(BF16)

    def mm(c0, c1):
        return _dot(hb, w_ref[:, c0:c1])

    lane = lax.broadcasted_iota(jnp.int32, (1, LANES), 1)
    j64 = lane & (HEAD_DIM - 1)

    def make_rope(cos, sin, lo_mask, hi_mask, half):
        c = jnp.where(lo_mask | hi_mask, cos, 1.0)
        sa = jnp.where(lo_mask, -sin, 0.0)
        sb = jnp.where(hi_mask, sin, 0.0)

        def apply(z):
            return z * c + pltpu.roll(z, LANES - half, 1) * sa + pltpu.roll(z, half, 1) * sb
        return apply

    rope_ret = make_rope(tab_ref[:, 0:128], tab_ref[:, 128:256], j64 < 32, j64 >= 32, 32)
    rope_dil = make_rope(tab_ref[:, 256:384], tab_ref[:, 384:512], j64 < 8, (j64 >= 8) & (j64 < 16), 8)
    cos_m = tab_ref[:, 512:640]
    sin_m = tab_ref[:, 640:768]
    rope_kr = make_rope(cos_m, sin_m, lane < 16, (lane >= 16) & (lane < 32), 16)
    rope_mq = make_rope(cos_m, sin_m, (lane >= 64) & (lane < 80), (lane >= 80) & (lane < 96), 16)

    z = mm(MLA_OFF, D_IN_PAD)
    cqn = _rms(z[:, 0:Q_LORA], qn_ref[...]).astype(BF16)
    q = _dot(cqn, wq_ref[...])
    scale = (MLA_NOPE + MLA_ROPE) ** -0.5 * LOG2E
    for h in range(N_HEADS):
        r = rope_mq(q[:, h * LANES:(h + 1) * LANES]) * scale
        mq_ref[:, h * LANES:(h + 1) * LANES] = r.astype(BF16)
    ckvn = _rms(z[:, Q_LORA:Q_LORA + KV_LORA], kvn_ref[...]).astype(BF16)
    kr = rope_kr(z[:, 384:512]).astype(BF16)
    mk_ref[...] = (_dot(ckvn, wk_ref[...]) + _dot(kr, pk_ref[...])).astype(BF16)
    mv_ref[...] = _dot(ckvn, wv_ref[...]).astype(BF16)
    z = mm(0, 512)
    for c in range(4):
        r = rope_ret(z[:, c * LANES:(c + 1) * LANES])
        if c >= 2:
            r = r * (HEAD_DIM ** -0.5)
        ret_ref[:, c * LANES:(c + 1) * LANES] = r.astype(BF16)
    ret_ref[:, 512:1024] = mm(512, 1024).astype(BF16)
    fu_ref[...] = mm(1024, 1280).astype(BF16)
    tm = x.shape[0]
    nslab = 3 * MIX_W // LANES
    for g, d_ref in enumerate((d0_ref, d1_ref, d2_ref)):
        dil = DIL_PAIRS[g][1]
        base = g * 3 * MIX_W
        z = mm(1280 + base, 1280 + base + 3 * MIX_W)
        slabs = []
        for c in range(nslab):
            r = z[:, c * LANES:(c + 1) * LANES]
            if c < 4:
                r = rope_dil(r)
            if c < 2:
                r = r * (HEAD_DIM ** -0.5 * LOG2E)
            slabs.append(r)
        if dil == 1:
            d_ref[0, 0] = jnp.concatenate(slabs, axis=1).astype(BF16)
            continue
        for c in range(nslab):
            scr_ref[c] = slabs[c]
        n4 = tm // 4

        def rows4(ref, start, count):
            return jnp.concatenate([ref[c, pl.ds(start, count, stride=4), :] for c in range(nslab)], axis=1)

        if dil == 4:
            for r4 in range(4):
                d_ref[0, r4] = rows4(scr_ref, r4, n4).astype(BF16)
        else:
            for r4 in range(4):
                for c in range(nslab):
                    scr2_ref[c, r4 * n4:(r4 + 1) * n4, :] = scr_ref[c, pl.ds(r4, n4, stride=4), :]
            for r4 in range(4):
                for q4 in range(4):
                    d_ref[0, r4 + 4 * q4] = rows4(scr2_ref, r4 * n4 + q4, tm // 16).astype(BF16)


def _inproj(x2d, mod, g_pre, w_in_p, tab, qn, kvn, wq_p, wk_p, wv_p, pk, seq):
    t = x2d.shape[0]
    b = t // seq
    tm = ROW_TILE
    tps = seq // tm
    const = lambda i: (0, 0)
    row = lambda i: (i, 0)
    flat = [(t, 1024), (t, MIX_W)]
    flat2 = [(t, 512), (t, 512), (t, MIX_W)]
    dils = [d for _, d in DIL_PAIRS]
    assert dils == [1, 4, 16]
    bf = lambda shp: jax.ShapeDtypeStruct(shp, BF16)
    dil_shapes = [bf((b, d, seq // d, 3 * MIX_W)) for d in dils]
    dil_specs = [pl.BlockSpec((1, d, tm // d, 3 * MIX_W), lambda i: (i // tps, 0, i % tps, 0)) for d in dils]
    return pl.pallas_call(
        _inproj_kernel,
        out_shape=[bf(s) for s in flat] + dil_shapes + [bf(s) for s in flat2],
        grid=(t // tm,),
        in_specs=[
            pl.BlockSpec((tm, D_MODEL), row),
            pl.BlockSpec((1, 6, D_MODEL), lambda i: (i // tps, 0, 0)),
            pl.BlockSpec((1, D_MODEL), const),
            pl.BlockSpec((D_MODEL, D_IN_PAD), const),
            pl.BlockSpec((tm, 6 * LANES), lambda i: (i % tps, 0)),
            pl.BlockSpec((1, Q_LORA), const),
            pl.BlockSpec((1, KV_LORA), const),
            pl.BlockSpec((Q_LORA, 512), const),
            pl.BlockSpec((KV_LORA, 512), const),
            pl.BlockSpec((KV_LORA, MIX_W), const),
            pl.BlockSpec((LANES, 512), const),
        ],
        out_specs=[pl.BlockSpec((tm, s[1]), row) for s in flat] + dil_specs
                  + [pl.BlockSpec((tm, s[1]), row) for s in flat2],
        scratch_shapes=[pltpu.VMEM((3 * MIX_W // LANES, tm, LANES), F32),
                        pltpu.VMEM((3 * MIX_W // LANES, tm, LANES), F32)],
        compiler_params=_params("arbitrary"),
        name="inproj",
    )(x2d, mod, g_pre, w_in_p, tab, qn, kvn, wq_p, wk_p, wv_p, pk)


def _ret_kernel(lg_ref, q_ref, k_ref, v_ref, g_ref, o_ref, acc_ref, st_ref, dmat_ref, vec_ref, rdec_ref):
    c = RET_CHUNK
    seq = q_ref.shape[1]
    n_chunks = seq // c
    lane_head = lax.broadcasted_iota(jnp.int32, (1, MIX_W), 1) // HEAD_DIM
    row_head = lax.broadcasted_iota(jnp.int32, (MIX_W, 1), 0) // HEAD_DIM
    blockdiag = row_head == lane_head

    def per_head(idx, d):
        out = lg_ref[d, 0]
        for h in range(1, N_HEADS):
            out = jnp.where(idx == h, lg_ref[d, h], out)
        return out

    @pl.when(pl.program_id(0) == 0)
    def _tables():
        ri = lax.broadcasted_iota(jnp.int32, (c, c), 0)
        ci = lax.broadcasted_iota(jnp.int32, (c, c), 1)
        diff = (ri - ci).astype(F32)
        for h in range(N_HEADS):
            fwd = jnp.exp(jnp.where(diff >= 0, diff, 0.0) * lg_ref[0, h])
            bwd = jnp.exp(jnp.where(diff < 0, -diff, 0.0) * lg_ref[1, h])
            dmat_ref[h] = jnp.where(diff >= 0, fwd, bwd)
        pos = lax.broadcasted_iota(jnp.int32, (c, MIX_W), 0).astype(F32)
        lf = per_head(lane_head, 0)
        lb = per_head(lane_head, 1)
        vec_ref[0] = jnp.exp((pos + 1.0) * lf)
        vec_ref[1] = jnp.exp((c - 1.0 - pos) * lf)
        vec_ref[2] = jnp.exp((c - pos) * lb)
        vec_ref[3] = jnp.exp(pos * lb)
        rdec_ref[0] = jnp.broadcast_to(jnp.exp(c * per_head(row_head, 0)), (MIX_W, MIX_W))
        rdec_ref[1] = jnp.broadcast_to(jnp.exp(c * per_head(row_head, 1)), (MIX_W, MIX_W))

    ones_bd = jnp.where(blockdiag, 1.0, 0.0).astype(BF16)

    def chunk(ref, n):
        return ref[0, pl.ds(pl.multiple_of(n * c, c), c), :]

    def fwd_body(n, carry):
        qn, kn, vn = chunk(q_ref, n), chunk(k_ref, n), chunk(v_ref, n)
        acc = _dot((qn.astype(F32) * vec_ref[0]).astype(BF16), st_ref[...].astype(BF16))
        for h in range(N_HEADS):
            hm = lane_head == h
            s = _dot_nt(jnp.where(hm, qn, jnp.zeros_like(qn)), kn)
            p = (s * dmat_ref[h]).astype(BF16)
            acc = acc + _dot(p, jnp.where(hm, vn, jnp.zeros_like(vn)))
        acc_ref[pl.ds(pl.multiple_of(n * c, c), c), :] = acc
        kv = _dot_tn((kn.astype(F32) * vec_ref[1]).astype(BF16), vn)
        st_ref[...] = st_ref[...] * rdec_ref[0] + jnp.where(blockdiag, kv, 0.0)
        return carry

    st_ref[...] = jnp.zeros_like(st_ref)
    lax.fori_loop(0, n_chunks, fwd_body, 0, unroll=RET_UNROLL)

    def bwd_body(t, carry):
        n = n_chunks - 1 - t
        qn, kn, vn = chunk(q_ref, n), chunk(k_ref, n), chunk(v_ref, n)
        r0 = pl.multiple_of(n * c, c)
        o = acc_ref[pl.ds(r0, c), :] + _dot((qn.astype(F32) * vec_ref[2]).astype(BF16),
                                           st_ref[...].astype(BF16))
        o2 = o * o
        hi = o2.astype(BF16)
        lo = (o2 - hi.astype(F32)).astype(BF16)
        ms = (_dot(hi, ones_bd) + _dot(lo, ones_bd)) * (1.0 / HEAD_DIM)
        gate = chunk(g_ref, n).astype(F32)
        o_ref[0, pl.ds(r0, c), :] = (gate * _sigmoid(gate) * (o * lax.rsqrt(ms + EPS))).astype(BF16)
        kv = _dot_tn((kn.astype(F32) * vec_ref[3]).astype(BF16), vn)
        st_ref[...] = st_ref[...] * rdec_ref[1] + jnp.where(blockdiag, kv, 0.0)
        return carry

    st_ref[...] = jnp.zeros_like(st_ref)
    lax.fori_loop(0, n_chunks, bwd_body, 0, unroll=RET_UNROLL)


def _retention(ret3d, lg):
    b, seq, _ = ret3d.shape
    spec = lambda col: pl.BlockSpec((1, seq, MIX_W), lambda i, col=col: (i, 0, col))
    return pl.pallas_call(
        _ret_kernel,
        out_shape=jax.ShapeDtypeStruct((b, seq, MIX_W), BF16),
        grid=(b,),
        in_specs=[pl.BlockSpec(memory_space=pltpu.SMEM), spec(0), spec(1), spec(2), spec(3)],
        out_specs=pl.BlockSpec((1, seq, MIX_W), lambda i: (i, 0, 0)),
        scratch_shapes=[
            pltpu.VMEM((seq, MIX_W), F32),
            pltpu.VMEM((MIX_W, MIX_W), F32),
            pltpu.VMEM((N_HEADS, RET_CHUNK, RET_CHUNK), F32),
            pltpu.VMEM((4, RET_CHUNK, MIX_W), F32),
            pltpu.VMEM((2, MIX_W, MIX_W), F32),
        ],
        compiler_params=_params("arbitrary"),
        name="retention",
    )(lg, ret3d, ret3d, ret3d, ret3d)


def _fourier_kernel(cs_ref, csh_ref, ss_ref, ssh_ref, u_ref, cc_ref, sc_ref, wf_ref, lo_ref, hi_ref, *, scale):
    half = u_ref.shape[1] // 2
    tr = cs_ref.shape[0]
    blk = FLIP_BLOCK
    ri = lax.broadcasted_iota(jnp.int32, (blk, blk), 0)
    ci = lax.broadcasted_iota(jnp.int32, (blk, blk), 1)
    flip_shift = jnp.where(ri + ci == blk, 1.0, 0.0).astype(BF16)
    flip = jnp.where(ri + ci == blk - 1, 1.0, 0.0).astype(BF16)
    first = lax.broadcasted_iota(jnp.int32, (blk, 1), 0) == 0
    nb = half // blk
    upper = lambda c: u_ref[0, half + c * blk:half + (c + 1) * blk, :]
    plus, minus = [], []
    for a in range(nb):
        row0 = upper(nb - a)[0:1].astype(F32) if a > 0 else jnp.zeros((1, MIX_W), F32)
        rev = jnp.where(first, row0, _dot(flip_shift, upper(nb - a - 1)))
        lo = u_ref[0, a * blk:(a + 1) * blk, :].astype(F32)
        plus.append((lo + rev).astype(BF16))
        minus.append((lo - rev).astype(BF16))
    u_mid = u_ref[0, half:half + 1, :].astype(F32)
    ext = tr + HALO
    row = lax.broadcasted_iota(jnp.int32, (ext, 1), 0)
    alt = jnp.where((row & 1) == 0, 1.0, -1.0)
    cs = jnp.concatenate([cs_ref[...], csh_ref[...]], axis=0)
    ss = jnp.concatenate([ss_ref[...], ssh_ref[...]], axis=0)
    z1 = (_dot(cs, jnp.concatenate(plus, axis=0)) + alt * u_mid).astype(BF16)
    z2 = _dot(ss, jnp.concatenate(minus, axis=0)).astype(BF16)
    a1 = _dot(z1, cc_ref[...])
    a2 = _dot(z2, sc_ref[...])
    lo_ref[0] = _dot(((a1 - a2)[0:tr] * scale).astype(BF16), wf_ref[...]).astype(BF16)
    mirrored = pltpu.roll(a1 + a2, ext - 1, 0)[0:tr]
    hi = _dot((mirrored * scale).astype(BF16), wf_ref[...]).astype(BF16)
    nbt = tr // blk
    for c in range(nbt):
        src = hi[(nbt - 1 - c) * blk:(nbt - c) * blk]
        hi_ref[0, c * blk:(c + 1) * blk, :] = _dot(flip, src).astype(BF16)


def _fourier(fu3d, cs, ss, cc, sc, wf):
    b, seq, _ = fu3d.shape
    half = seq // 2
    tr = min(FOUR_TR, half)
    steps = half // tr
    const = lambda i, j: (0, 0)
    main = pl.BlockSpec((tr, half), lambda i, j: (i, 0))
    halo = pl.BlockSpec((HALO, half), lambda i, j: ((i + 1) * (tr // HALO), 0))
    return pl.pallas_call(
        functools.partial(_fourier_kernel, scale=1.0 / math.sqrt(seq * HEAD_DIM)),
        out_shape=[jax.ShapeDtypeStruct((b, half, MIX_W), BF16), jax.ShapeDtypeStruct((b, half, MIX_W), BF16)],
        grid=(steps, b),
        in_specs=[
            main, halo, main, halo,
            pl.BlockSpec((1, seq, MIX_W), lambda i, j: (j, 0, 0)),
            pl.BlockSpec((MIX_W, MIX_W), const),
            pl.BlockSpec((MIX_W, MIX_W), const),
            pl.BlockSpec((MIX_W, MIX_W), const),
        ],
        out_specs=[pl.BlockSpec((1, tr, MIX_W), lambda i, j: (j, i, 0)),
                   pl.BlockSpec((1, tr, MIX_W), lambda i, j: (j, steps - 1 - i, 0))],
        compiler_params=_params("arbitrary", "arbitrary"),
        name="fourier",
    )(cs, cs, ss, ss, fu3d, cc, sc, wf)


def _dil_kernel(q_ref, k_ref, v_ref, o_ref, st_ref, bias_ref, *, sub_len, ts, rb, tq, win):
    j = pl.program_id(2)
    nblk = ts // tq
    lane_head = lax.broadcasted_iota(jnp.int32, (1, MIX_W), 1) // HEAD_DIM
    stat_slot = lax.broadcasted_iota(jnp.int32, (1, LANES), 1) // STAT_W

    @pl.when((pl.program_id(0) == 0) & (pl.program_id(1) == 0) & (j == 0))
    def _bias():
        rel = (lax.broadcasted_iota(jnp.int32, (N_HEADS * tq, win), 0) & (tq - 1)) \
            - lax.broadcasted_iota(jnp.int32, (N_HEADS * tq, win), 1)
        for i in range(3):
            bias_ref[i] = jnp.where(jnp.abs(rel + i * DIL_RADIUS) <= DIL_RADIUS, 0.0, NEG)

    for rr in range(rb):
        for blk in range(nblk):
            q0 = j * ts + blk * tq
            if win == sub_len:
                ws = 0
                bias = bias_ref[blk * tq // DIL_RADIUS]
            else:
                ws = pl.multiple_of(jnp.clip(q0 - DIL_RADIUS, 0, sub_len - win), DIL_RADIUS)
                bias = bias_ref[1] if 0 < blk < nblk - 1 else bias_ref[(q0 - ws) // DIL_RADIUS]
            q = q_ref[0, rr, blk * tq:(blk + 1) * tq, :]
            kw = k_ref[0, rr, pl.ds(ws, win), :]
            vw = v_ref[0, rr, pl.ds(ws, win), :]
            zero = jnp.zeros_like(q)
            qs = jnp.concatenate([jnp.where(lane_head == h, q, zero) for h in range(N_HEADS)], axis=0)
            s = _dot_nt(qs, kw) + bias
            m = jnp.max(s, axis=-1, keepdims=True)
            p = jnp.exp2(s - m)
            den = jnp.sum(p, axis=-1, keepdims=True)
            r = _dot(p.astype(BF16), vw)
            o = jnp.zeros((tq, MIX_W), F32)
            st = jnp.zeros((tq, LANES), F32)
            for h in range(N_HEADS):
                o = jnp.where(lane_head == h, r[h * tq:(h + 1) * tq], o)
                st = jnp.where(stat_slot == 2 * h, m[h * tq:(h + 1) * tq], st)
                st = jnp.where(stat_slot == 2 * h + 1, den[h * tq:(h + 1) * tq], st)
            o_ref[0, rr, blk * tq:(blk + 1) * tq, :] = o.astype(BF16)
            st_ref[0, rr, blk * tq:(blk + 1) * tq, :] = st


def _dilated(dg, group):
    b, dil, sub_len, _ = dg.shape
    tq = min(DIL_TQ, sub_len)
    win = min(tq + 2 * DIL_RADIUS, sub_len)
    ts = min(sub_len, DIL_STEP_ROWS)
    rb = min(dil, DIL_STEP_ROWS // ts)
    part = lambda c: (lambda i, r, j: (i, r, 0, c))
    return pl.pallas_call(
        functools.partial(_dil_kernel, sub_len=sub_len, ts=ts, rb=rb, tq=tq, win=win),
        out_shape=[jax.ShapeDtypeStruct((b, dil, sub_len, MIX_W), BF16),
                   jax.ShapeDtypeStruct((b, dil, sub_len, LANES), F32)],
        grid=(b, dil // rb, sub_len // ts),
        in_specs=[
            pl.BlockSpec((1, rb, ts, MIX_W), lambda i, r, j: (i, r, j, 0)),
            pl.BlockSpec((1, rb, sub_len, MIX_W), part(1)),
            pl.BlockSpec((1, rb, sub_len, MIX_W), part(2)),
        ],
        out_specs=[pl.BlockSpec((1, rb, ts, MIX_W), lambda i, r, j: (i, r, j, 0)),
                   pl.BlockSpec((1, rb, ts, LANES), lambda i, r, j: (i, r, j, 0))],
        scratch_shapes=[pltpu.VMEM((3, N_HEADS * tq, win), F32)],
        compiler_params=_params("arbitrary", "arbitrary", "arbitrary"),
        name=f"dilated{group}",
    )(dg, dg, dg)


def _mla_kernel(q_ref, k_ref, v_ref, o_ref):
    lane_head = lax.broadcasted_iota(jnp.int32, (1, MIX_W), 1) // HEAD_DIM
    v = v_ref[0]
    for b0 in range(0, q_ref.shape[1], MLA_TQ):
        out = jnp.zeros((MLA_TQ, MIX_W), F32)
        for h in range(N_HEADS):
            qh = q_ref[0, b0:b0 + MLA_TQ, h * LANES:(h + 1) * LANES]
            s = _dot_nt(qh, k_ref[0, :, h * LANES:(h + 1) * LANES])
            m = jnp.max(s, axis=-1, keepdims=True)
            p = jnp.exp2(s - m)
            den = jnp.sum(p, axis=-1, keepdims=True)
            o = _dot(p.astype(BF16), v) * (1.0 / den)
            out = jnp.where(lane_head == h, o, out)
        o_ref[0, b0:b0 + MLA_TQ, :] = out.astype(BF16)


def _mla(mq3d, mk3d, mv3d):
    b, seq, _ = mq3d.shape
    ts = max(MLA_TQ, min(seq, MLA_STEP_SCORES // seq))
    return pl.pallas_call(
        _mla_kernel,
        out_shape=jax.ShapeDtypeStruct((b, seq, MIX_W), BF16),
        grid=(b, seq // ts),
        in_specs=[
            pl.BlockSpec((1, ts, N_HEADS * LANES), lambda i, j: (i, j, 0)),
            pl.BlockSpec((1, seq, N_HEADS * LANES), lambda i, j: (i, 0, 0)),
            pl.BlockSpec((1, seq, MIX_W), lambda i, j: (i, 0, 0)),
        ],
        out_specs=pl.BlockSpec((1, ts, MIX_W), lambda i, j: (i, j, 0)),
        compiler_params=_params("arbitrary", "arbitrary"),
        name="mla",
    )(mq3d, mk3d, mv3d)


def _outproj_kernel(x_ref, mod_ref, ro_ref, flo_ref, fhi_ref, d0_ref, d1_ref, d2_ref, l0_ref, l1_ref, l2_ref,
                    mo_ref, w_ref, gpm_ref, gpf_ref, x1_ref, h2_ref, so1_ref, sl1_ref, so2_ref, sl2_ref,
                    *, tiles_per_seq):
    tm = x_ref.shape[0]
    in_lower = (pl.program_id(0) % tiles_per_seq) < tiles_per_seq // 2
    fo = jnp.where(in_lower, flo_ref[0], fhi_ref[0])

    def natural_order(o_ref, l_ref, so_ref, sl_ref):
        dil = o_ref.shape[1]
        if dil == 1:
            return o_ref[0, 0].astype(F32), l_ref[0, 0]
        n = tm // dil
        for r in range(dil):
            o = o_ref[0, r].astype(F32)
            for c in range(MIX_W // LANES):
                so_ref[c, pl.ds(r, n, stride=dil), :] = o[:, c * LANES:(c + 1) * LANES]
            sl_ref[pl.ds(r, n, stride=dil), :] = l_ref[0, r]
        return jnp.concatenate([so_ref[c] for c in range(MIX_W // LANES)], axis=1), sl_ref[...]

    o0, l0 = natural_order(d0_ref, l0_ref, None, None)
    o1, l1 = natural_order(d1_ref, l1_ref, so1_ref, sl1_ref)
    o2, l2 = natural_order(d2_ref, l2_ref, so2_ref, sl2_ref)
    m = jnp.maximum(l0, jnp.maximum(l1, l2))
    e = [jnp.exp2(l - m) for l in (l0, l1, l2)]
    den = sum(eg * pltpu.roll(l, LANES - STAT_W, 1) for eg, l in zip(e, (l0, l1, l2)))
    inv = 1.0 / den
    is_max_lane = (lax.broadcasted_iota(jnp.int32, (1, LANES), 1) & STAT_W) == 0
    src = lax.broadcasted_iota(jnp.int32, (LANES, MIX_W), 0)
    dst = lax.broadcasted_iota(jnp.int32, (LANES, MIX_W), 1)
    spread = jnp.where(src == (dst // HEAD_DIM) * (LANES // N_HEADS), 1.0, 0.0).astype(BF16)
    od = sum(_dot(jnp.where(is_max_lane, eg * inv, 0.0).astype(BF16), spread) * og
             for eg, og in zip(e, (o0, o1, o2))).astype(BF16)
    y = (_dot(ro_ref[...], w_ref[0:256, :]) + _dot(fo, w_ref[256:512, :])
         + _dot(od, w_ref[512:768, :]) + _dot(mo_ref[...], w_ref[768:1024, :]))
    g1 = mod_ref[0, 2:3, :]
    sh2 = mod_ref[0, 3:4, :]
    sc2 = mod_ref[0, 4:5, :]
    x1 = x_ref[...] + g1 * _rms(y, gpm_ref[...])
    x1_ref[...] = x1
    h2_ref[...] = (_rms(x1, gpf_ref[...]) * (1.0 + sc2) + sh2).astype(BF16)


def _outproj(x2d, mod, ro, fo_halves, d_o, d_l, mo, w_out, g_post_mix, g_pre_ffn, seq):
    t = x2d.shape[0]
    tm = ROW_TILE
    tps = seq // tm
    row = lambda i: (i, 0)
    const = lambda i: (0, 0)
    mix = pl.BlockSpec((tm, MIX_W), row)
    res = lambda a: pl.BlockSpec((1, a.shape[1], tm // a.shape[1], a.shape[3]), lambda i: (i // tps, 0, i % tps, 0))
    hps = tps // 2
    flo = pl.BlockSpec((1, tm, MIX_W), lambda i: (i // tps, jnp.minimum(i % tps, hps - 1), 0))
    fhi = pl.BlockSpec((1, tm, MIX_W), lambda i: (i // tps, jnp.maximum(i % tps - hps, 0), 0))
    return pl.pallas_call(
        functools.partial(_outproj_kernel, tiles_per_seq=tps),
        out_shape=[jax.ShapeDtypeStruct((t, D_MODEL), F32), jax.ShapeDtypeStruct((t, D_MODEL), BF16)],
        grid=(t // tm,),
        in_specs=[
            pl.BlockSpec((tm, D_MODEL), row),
            pl.BlockSpec((1, 6, D_MODEL), lambda i: (i // tps, 0, 0)),
            mix, flo, fhi, res(d_o[0]), res(d_o[1]), res(d_o[2]), res(d_l[0]), res(d_l[1]), res(d_l[2]), mix,
            pl.BlockSpec((D_MODEL, D_MODEL), const),
            pl.BlockSpec((1, D_MODEL), const),
            pl.BlockSpec((1, D_MODEL), const),
        ],
        out_specs=[pl.BlockSpec((tm, D_MODEL), row), pl.BlockSpec((tm, D_MODEL), row)],
        scratch_shapes=[pltpu.VMEM((MIX_W // LANES, tm, LANES), F32), pltpu.VMEM((tm, LANES), F32),
                        pltpu.VMEM((MIX_W // LANES, tm, LANES), F32), pltpu.VMEM((tm, LANES), F32)],
        compiler_params=_params("arbitrary"),
        name="outproj",
    )(x2d, mod, ro, fo_halves[0], fo_halves[1], d_o[0], d_o[1], d_o[2], d_l[0], d_l[1], d_l[2], mo, w_out,
      g_post_mix, g_pre_ffn)


def _ffn_kernel(hp_ref, hc_ref, hn_ref, x1_ref, mod_ref, wu_ref, cw_ref, cb_ref, wd_ref, g_ref, o_ref,
                gate_ref, *, tiles_per_seq):
    tm = hc_ref.shape[0]
    t = pl.program_id(0) % tiles_per_seq
    hp = jnp.where(t == 0, jnp.zeros_like(hp_ref[...]), hp_ref[...])
    hn = jnp.where(t == tiles_per_seq - 1, jnp.zeros_like(hn_ref[...]), hn_ref[...])
    he = jnp.concatenate([hp, hc_ref[...], hn], axis=0)

    ext = tm + 2 * HALO

    def conv(c0):
        u = _dot(he, wu_ref[:, c0:c0 + FFN_CHUNK])
        w = cw_ref[:, c0:c0 + FFN_CHUNK]
        prev = pltpu.roll(u, 1, 0)[HALO:HALO + tm]
        nxt = pltpu.roll(u, ext - 1, 0)[HALO:HALO + tm]
        return prev * w[0:1] + u[HALO:HALO + tm] * w[1:2] + nxt * w[2:3] + cb_ref[:, c0:c0 + FFN_CHUNK]

    for c in range(D_FF // FFN_CHUNK):
        a = conv(c * FFN_CHUNK)
        bu = conv(D_FF + c * FFN_CHUNK)
        gate_ref[:, c * FFN_CHUNK:(c + 1) * FFN_CHUNK] = (a * _sigmoid(a) * bu).astype(BF16)
    acc = _dot(gate_ref[...], wd_ref[...])
    g2 = mod_ref[0, 5:6, :]
    o_ref[...] = x1_ref[...] + g2 * _rms(acc, g_ref[...])


def _ffn(h2, x1, mod, w_up, conv_w, conv_b, w_down, g_post_ffn, seq):
    t = x1.shape[0]
    tm = FFN_TILE
    tps = seq // tm
    hb = tm // HALO
    row = lambda i: (i, 0)
    const = lambda i: (0, 0)
    resident = lambda shape: pl.BlockSpec(shape, const, pipeline_mode=pl.Buffered(1))
    return pl.pallas_call(
        functools.partial(_ffn_kernel, tiles_per_seq=tps),
        out_shape=jax.ShapeDtypeStruct((t, D_MODEL), F32),
        grid=(t // tm,),
        in_specs=[
            pl.BlockSpec((HALO, D_MODEL), lambda i: (jnp.maximum(i * hb - 1, 0), 0)),
            pl.BlockSpec((tm, D_MODEL), row),
            pl.BlockSpec((HALO, D_MODEL), lambda i: (jnp.minimum((i + 1) * hb, t // HALO - 1), 0)),
            pl.BlockSpec((tm, D_MODEL), row),
            pl.BlockSpec((1, 6, D_MODEL), lambda i: (i // tps, 0, 0)),
            resident((D_MODEL, 2 * D_FF)),
            pl.BlockSpec((3, 2 * D_FF), const),
            pl.BlockSpec((1, 2 * D_FF), const),
            resident((D_FF, D_MODEL)),
            pl.BlockSpec((1, D_MODEL), const),
        ],
        out_specs=pl.BlockSpec((tm, D_MODEL), row),
        scratch_shapes=[pltpu.VMEM((tm, D_FF), BF16)],
        compiler_params=_params("arbitrary"),
        name="ffn",
    )(h2, h2, h2, x1, mod, w_up, conv_w, conv_b, w_down, g_post_ffn)


def _rope_tables(seq):
    pos = jnp.arange(seq, dtype=F32)[:, None]
    lane = np.arange(LANES)
    cols = []
    for theta, rot in ((RET_THETA, HEAD_DIM), (ROPE_THETA, PARTIAL_ROT), (ROPE_THETA, MLA_ROPE)):
        half = rot // 2
        inv = jnp.power(theta, -jnp.arange(half, dtype=F32) * 2.0 / rot)
        ang = pos * inv[lane % half][None, :]
        cols += [jnp.cos(ang), jnp.sin(ang)]
    return jnp.concatenate(cols, axis=1)


def _dft_tables(seq):
    n2 = 64
    n1 = seq // (2 * n2)
    k = np.arange(seq)[:, None]
    a = 2.0 * np.pi * ((k * np.arange(n1)[None, :] * n2) % seq) / seq
    b = 2.0 * np.pi * ((k * np.arange(n2)[None, :]) % seq) / seq
    ca, sa = jnp.asarray(np.cos(a), F32)[:, :, None], jnp.asarray(np.sin(a), F32)[:, :, None]
    cb, sb = jnp.asarray(np.cos(b), F32)[:, None, :], jnp.asarray(np.sin(b), F32)[:, None, :]
    cs = (ca * cb - sa * sb).reshape(seq, seq // 2).astype(BF16)
    ss = (sa * cb + ca * sb).reshape(seq, seq // 2).astype(BF16)
    return cs, ss


def _block_diag(blocks):
    n = len(blocks)
    rows = [jnp.concatenate([blocks[i] if i == j else jnp.zeros_like(blocks[i]) for j in range(n)], axis=1)
            for i in range(n)]
    return jnp.concatenate(rows, axis=0)


def _perm_w_in(w_in):
    ret = w_in[:, 0:1280]
    dq, dk, dv = w_in[:, 1280:2048], w_in[:, 2048:2816], w_in[:, 2816:3584]
    groups = [jnp.concatenate([m[:, g * MIX_W:(g + 1) * MIX_W] for m in (dq, dk, dv)], axis=1)
              for g in range(N_DIL_GROUPS)]
    pad = jnp.zeros((D_MODEL, D_IN_PAD - w_in.shape[1]), w_in.dtype)
    return jnp.concatenate([ret] + groups + [w_in[:, 3584:], pad], axis=1).astype(BF16)


def _mla_weights(w_qb, w_kvb):
    qh = w_qb.reshape(Q_LORA, N_HEADS, MLA_NOPE + MLA_ROPE)
    wq = jnp.pad(qh, ((0, 0), (0, 0), (0, LANES - MLA_NOPE - MLA_ROPE))).reshape(Q_LORA, N_HEADS * LANES)
    kvh = w_kvb.reshape(KV_LORA, N_HEADS, MLA_NOPE + HEAD_DIM)
    wk = jnp.pad(kvh[:, :, :MLA_NOPE], ((0, 0), (0, 0), (0, LANES - MLA_NOPE))).reshape(KV_LORA, N_HEADS * LANES)
    wv = kvh[:, :, MLA_NOPE:].reshape(KV_LORA, MIX_W)
    place = np.zeros((LANES, N_HEADS * LANES), np.float32)
    for h in range(N_HEADS):
        for r in range(MLA_ROPE):
            place[r, h * LANES + MLA_NOPE + r] = 1.0
    return wq.astype(BF16), wk.astype(BF16), wv.astype(BF16), jnp.asarray(place, BF16)


def _trunk(x, mods, layers, shared):
    b, seq, _ = x.shape
    t = b * seq
    x2d = x.reshape(t, D_MODEL)
    tab = _rope_tables(seq)
    cs, ss = _dft_tables(seq)
    for mod, lw in zip(mods, layers):
        ret, fu, dg0, dg1, dg2, mq, mk, mv = _inproj(x2d, mod, lw["g_pre_mix"], lw["w_in"], tab, lw["q_norm"],
                                                     lw["kv_norm"], lw["wq"], lw["wk"], lw["wv"],
                                                     shared["place"], seq)
        ro = _retention(ret.reshape(b, seq, 1024), lw["lg"]).reshape(t, MIX_W)
        fo = _fourier(fu.reshape(b, seq, MIX_W), cs, ss, shared["cc"], shared["sc"], lw["wf"])
        d_o, d_l = zip(*[_dilated(dg, g) for g, dg in enumerate((dg0, dg1, dg2))])
        mo = _mla(mq.reshape(b, seq, 512), mk.reshape(b, seq, 512), mv.reshape(b, seq, MIX_W)).reshape(t, MIX_W)
        x1, h2 = _outproj(x2d, mod, ro, fo, d_o, d_l, mo, lw["w_out"], lw["g_post_mix"], lw["g_pre_ffn"], seq)
        x2d = _ffn(h2, x1, mod, lw["w_up"], lw["conv_w"], lw["conv_b"], lw["w_down"], lw["g_post_ffn"], seq)
    return x2d.reshape(b, seq, D_MODEL)


def kernel(x_prompt, x_sample, c_prompt, c_sample, w_ada, b_ada, norm_pre_mix, w_in, ret_decay_fwd,
           ret_decay_bwd, w_fmix, mla_q_norm, mla_w_qb, mla_kv_norm, mla_w_kvb, w_out, norm_post_mix,
           norm_pre_ffn, w_up, conv_w, conv_b, w_down, norm_post_ffn):
    depth = w_in.shape[0]
    nb_p, nb_s = c_prompt.shape[0], c_sample.shape[0]
    rows = -(-(nb_p + nb_s) // 8) * 8
    c_all = jnp.concatenate([c_prompt, c_sample, jnp.zeros((rows - nb_p - nb_s, D_MODEL), F32)], axis=0)
    mod_all = _ada(c_all, w_ada, b_ada)
    mods_p = [mod_all[l, :nb_p].reshape(nb_p, 6, D_MODEL) for l in range(depth)]
    mods_s = [mod_all[l, nb_p:nb_p + nb_s].reshape(nb_s, 6, D_MODEL) for l in range(depth)]

    c64 = 2.0 * np.pi * np.outer(np.arange(HEAD_DIM), np.arange(HEAD_DIM)) / HEAD_DIM
    shared = {
        "cc": _block_diag([jnp.asarray(np.cos(c64), BF16)] * N_HEADS),
        "sc": _block_diag([jnp.asarray(np.sin(c64), BF16)] * N_HEADS),
    }
    layers = []
    for l in range(depth):
        wq, wk, wv, place = _mla_weights(mla_w_qb[l], mla_w_kvb[l])
        shared["place"] = place
        layers.append({
            "g_pre_mix": norm_pre_mix[l][None, :],
            "w_in": _perm_w_in(w_in[l]),
            "lg": jnp.stack([jax.nn.log_sigmoid(ret_decay_fwd[l]), jax.nn.log_sigmoid(ret_decay_bwd[l])]),
            "wf": _block_diag([w_fmix[l, g] for g in range(N_HEADS)]).astype(BF16),
            "q_norm": mla_q_norm[l][None, :],
            "kv_norm": mla_kv_norm[l][None, :],
            "wq": wq, "wk": wk, "wv": wv,
            "w_out": w_out[l].astype(BF16),
            "g_post_mix": norm_post_mix[l][None, :],
            "g_pre_ffn": norm_pre_ffn[l][None, :],
            "w_up": w_up[l].astype(BF16),
            "conv_w": conv_w[l],
            "conv_b": conv_b[l][None, :],
            "w_down": w_down[l].astype(BF16),
            "g_post_ffn": norm_post_ffn[l][None, :],
        })
    y_prompt = _trunk(x_prompt, mods_p, layers, shared)
    y_sample = _trunk(x_sample, mods_s, layers, shared)
    return (y_prompt, y_sample)
```

```python
import functools
import math

import numpy as np
import jax
import jax.numpy as jnp
from jax import lax
from jax.experimental import pallas as pl
from jax.experimental.pallas import tpu as pltpu

F32 = jnp.float32
BF16 = jnp.bfloat16

D_MODEL = 1024
HEAD_DIM = 64
N_HEADS = 4
MIX_W = N_HEADS * HEAD_DIM
DIL_PAIRS = ((128, 1), (512, 4), (2048, 16))
N_DIL_GROUPS = 3
DIL_RADIUS = 64
STAT_W = 16
MLA_NOPE = 64
MLA_ROPE = 32
Q_LORA = 256
KV_LORA = 128
D_FF = 2816
ROPE_THETA = 500000.0
RET_THETA = 10000.0
PARTIAL_ROT = HEAD_DIM // 4
EPS = 1e-6
NEG = -1e30
LOG2E = math.log2(math.e)

LANES = 128
D_IN_PAD = 4096
MLA_OFF = 3584
ROW_TILE = 1024
RET_CHUNK = 256
RET_UNROLL = 8
DIL_TQ = 128
DIL_STEP_ROWS = 2048
FOUR_TR = 1024
FLIP_BLOCK = 256
MLA_TQ = 256
MLA_STEP_SCORES = 2 ** 22
FFN_CHUNK = 256
HALO = 16
VMEM_LIMIT = 56 * 1024 * 1024


def _params(*sem):
    return pltpu.CompilerParams(dimension_semantics=sem, vmem_limit_bytes=VMEM_LIMIT)


def _rms(x, g):
    return x * lax.rsqrt(jnp.mean(x * x, axis=-1, keepdims=True) + EPS) * g


def _sigmoid(x):
    return 1.0 / (1.0 + jnp.exp(-x))


def _dot(a, b):
    return jnp.dot(a, b, preferred_element_type=F32)


def _dot_nt(a, b):
    return lax.dot_general(a, b, (((1,), (1,)), ((), ())), preferred_element_type=F32)


def _dot_tn(a, b):
    return lax.dot_general(a, b, (((0,), (0,)), ((), ())), preferred_element_type=F32)


def _ada_kernel(c_ref, w_ref, b_ref, o_ref):
    c = c_ref[...]
    cond = (c * _sigmoid(c)).astype(BF16)
    o_ref[0] = _dot(cond, w_ref[0].astype(BF16)) + b_ref[0]


def _ada(c_all, w_ada, b_ada):
    depth, _, n = w_ada.shape
    rows = c_all.shape[0]
    tn = 1536
    return pl.pallas_call(
        _ada_kernel,
        out_shape=jax.ShapeDtypeStruct((depth, rows, n), F32),
        grid=(depth, n // tn),
        in_specs=[
            pl.BlockSpec((rows, D_MODEL), lambda l, j: (0, 0)),
            pl.BlockSpec((1, D_MODEL, tn), lambda l, j: (l, 0, j)),
            pl.BlockSpec((1, 1, tn), lambda l, j: (l, 0, j)),
        ],
        out_specs=pl.BlockSpec((1, rows, tn), lambda l, j: (l, 0, j)),
        compiler_params=_params("arbitrary", "arbitrary"),
        name="ada",
    )(c_all, w_ada, b_ada.reshape(depth, 1, n))


def _inproj_kernel(x_ref, mod_ref, g_ref, w_ref, tab_ref, qn_ref, kvn_ref, wq_ref, wk_ref, wv_ref,
                   pk_ref, ret_ref, fu_ref, d0_ref, d1_ref, d2_ref, mq_ref, mk_ref, mv_ref, scr_ref, scr2_ref):
    x = x_ref[...]
    sh = mod_ref[0, 0:1, :]
    sc = mod_ref[0, 1:2, :]
    hb = (_rms(x, g_ref[...]) * (1.0 + sc) + sh).astype(BF16)

    def mm(c0, c1):
        return _dot(hb, w_ref[:, c0:c1])

    lane = lax.broadcasted_iota(jnp.int32, (1, LANES), 1)
    j64 = lane & (HEAD_DIM - 1)

    def make_rope(cos, sin, lo_mask, hi_mask, half):
        c = jnp.where(lo_mask | hi_mask, cos, 1.0)
        sa = jnp.where(lo_mask, -sin, 0.0)
        sb = jnp.where(hi_mask, sin, 0.0)

        def apply(z):
            return z * c + pltpu.roll(z, LANES - half, 1) * sa + pltpu.roll(z, half, 1) * sb
        return apply

    rope_ret = make_rope(tab_ref[:, 0:128], tab_ref[:, 128:256], j64 < 32, j64 >= 32, 32)
    rope_dil = make_rope(tab_ref[:, 256:384], tab_ref[:, 384:512], j64 < 8, (j64 >= 8) & (j64 < 16), 8)
    cos_m = tab_ref[:, 512:640]
    sin_m = tab_ref[:, 640:768]
    rope_kr = make_rope(cos_m, sin_m, lane < 16, (lane >= 16) & (lane < 32), 16)
    rope_mq = make_rope(cos_m, sin_m, (lane >= 64) & (lane < 80), (lane >= 80) & (lane < 96), 16)

    z = mm(MLA_OFF, D_IN_PAD)
    cqn = _rms(z[:, 0:Q_LORA], qn_ref[...]).astype(BF16)
    q = _dot(cqn, wq_ref[...])
    scale = (MLA_NOPE + MLA_ROPE) ** -0.5 * LOG2E
    for h in range(N_HEADS):
        r = rope_mq(q[:, h * LANES:(h + 1) * LANES]) * scale
        mq_ref[:, h * LANES:(h + 1) * LANES] = r.astype(BF16)
    ckvn = _rms(z[:, Q_LORA:Q_LORA + KV_LORA], kvn_ref[...]).astype(BF16)
    kr = rope_kr(z[:, 384:512]).astype(BF16)
    mk_ref[...] = (_dot(ckvn, wk_ref[...]) + _dot(kr, pk_ref[...])).astype(BF16)
    mv_ref[...] = _dot(ckvn, wv_ref[...]).astype(BF16)
    z = mm(0, 512)
    for c in range(4):
        r = rope_ret(z[:, c * LANES:(c + 1) * LANES])
        if c >= 2:
            r = r * (HEAD_DIM ** -0.5)
        ret_ref[:, c * LANES:(c + 1) * LANES] = r.astype(BF16)
    ret_ref[:, 512:1024] = mm(512, 1024).astype(BF16)
    fu_ref[...] = mm(1024, 1280).astype(BF16)
    tm = x.shape[0]
    nslab = 3 * MIX_W // LANES
    for g, d_ref in enumerate((d0_ref, d1_ref, d2_ref)):
        dil = DIL_PAIRS[g][1]
        base = g * 3 * MIX_W
        z = mm(1280 + base, 1280 + base + 3 * MIX_W)
        slabs = []
        for c in range(nslab):
            r = z[:, c * LANES:(c + 1) * LANES]
            if c < 4:
                r = rope_dil(r)
            if c < 2:
                r = r * (HEAD_DIM ** -0.5 * LOG2E)
            slabs.append(r)
        if dil == 1:
            d_ref[0, 0] = jnp.concatenate(slabs, axis=1).astype(BF16)
            continue
        for c in range(nslab):
            scr_ref[c] = slabs[c]
        n4 = tm // 4

        def rows4(ref, start, count):
            return jnp.concatenate([ref[c, pl.ds(start, count, stride=4), :] for c in range(nslab)], axis=1)

        if dil == 4:
            for r4 in range(4):
                d_ref[0, r4] = rows4(scr_ref, r4, n4).astype(BF16)
        else:
            for r4 in range(4):
                for c in range(nslab):
                    scr2_ref[c, r4 * n4:(r4 + 1) * n4, :] = scr_ref[c, pl.ds(r4, n4, stride=4), :]
            for r4 in range(4):
                for q4 in range(4):
                    d_ref[0, r4 + 4 * q4] = rows4(scr2_ref, r4 * n4 + q4, tm // 16).astype(BF16)


def _inproj(x2d, mod, g_pre, w_in_p, tab, qn, kvn, wq_p, wk_p, wv_p, pk, seq):
    t = x2d.shape[0]
    b = t // seq
    tm = ROW_TILE
    tps = seq // tm
    const = lambda i: (0, 0)
    row = lambda i: (i, 0)
    flat = [(t, 1024), (t, MIX_W)]
    flat2 = [(t, 512), (t, 512), (t, MIX_W)]
    dils = [d for _, d in DIL_PAIRS]
    assert dils == [1, 4, 16]
    bf = lambda shp: jax.ShapeDtypeStruct(shp, BF16)
    dil_shapes = [bf((b, d, seq // d, 3 * MIX_W)) for d in dils]
    dil_specs = [pl.BlockSpec((1, d, tm // d, 3 * MIX_W), lambda i: (i // tps, 0, i % tps, 0)) for d in dils]
    return pl.pallas_call(
        _inproj_kernel,
        out_shape=[bf(s) for s in flat] + dil_shapes + [bf(s) for s in flat2],
        grid=(t // tm,),
        in_specs=[
            pl.BlockSpec((tm, D_MODEL), row),
            pl.BlockSpec((1, 6, D_MODEL), lambda i: (i // tps, 0, 0)),
            pl.BlockSpec((1, D_MODEL), const),
            pl.BlockSpec((D_MODEL, D_IN_PAD), const, pipeline_mode=pl.Buffered(1)),
            pl.BlockSpec((tm, 6 * LANES), lambda i: (i % tps, 0)),
            pl.BlockSpec((1, Q_LORA), const),
            pl.BlockSpec((1, KV_LORA), const),
            pl.BlockSpec((Q_LORA, 512), const),
            pl.BlockSpec((KV_LORA, 512), const),
            pl.BlockSpec((KV_LORA, MIX_W), const),
            pl.BlockSpec((LANES, 512), const),
        ],
        out_specs=[pl.BlockSpec((tm, s[1]), row) for s in flat] + dil_specs
                  + [pl.BlockSpec((tm, s[1]), row) for s in flat2],
        scratch_shapes=[pltpu.VMEM((3 * MIX_W // LANES, tm, LANES), F32),
                        pltpu.VMEM((3 * MIX_W // LANES, tm, LANES), F32)],
        compiler_params=_params("arbitrary"),
        name="inproj",
    )(x2d, mod, g_pre, w_in_p, tab, qn, kvn, wq_p, wk_p, wv_p, pk)


def _ret_kernel(lg_ref, q_ref, k_ref, v_ref, g_ref, o_ref, acc_ref, st_ref, dmat_ref, vec_ref, rdec_ref):
    c = RET_CHUNK
    seq = q_ref.shape[1]
    n_chunks = seq // c
    lane_head = lax.broadcasted_iota(jnp.int32, (1, MIX_W), 1) // HEAD_DIM
    row_head = lax.broadcasted_iota(jnp.int32, (MIX_W, 1), 0) // HEAD_DIM
    blockdiag = row_head == lane_head

    def per_head(idx, d):
        out = lg_ref[d, 0]
        for h in range(1, N_HEADS):
            out = jnp.where(idx == h, lg_ref[d, h], out)
        return out

    @pl.when(pl.program_id(0) == 0)
    def _tables():
        ri = lax.broadcasted_iota(jnp.int32, (c, c), 0)
        ci = lax.broadcasted_iota(jnp.int32, (c, c), 1)
        diff = (ri - ci).astype(F32)
        for h in range(N_HEADS):
            fwd = jnp.exp(jnp.where(diff >= 0, diff, 0.0) * lg_ref[0, h])
            bwd = jnp.exp(jnp.where(diff < 0, -diff, 0.0) * lg_ref[1, h])
            dmat_ref[h] = jnp.where(diff >= 0, fwd, bwd)
        pos = lax.broadcasted_iota(jnp.int32, (c, MIX_W), 0).astype(F32)
        lf = per_head(lane_head, 0)
        lb = per_head(lane_head, 1)
        vec_ref[0] = jnp.exp((pos + 1.0) * lf)
        vec_ref[1] = jnp.exp((c - 1.0 - pos) * lf)
        vec_ref[2] = jnp.exp((c - pos) * lb)
        vec_ref[3] = jnp.exp(pos * lb)
        rdec_ref[0] = jnp.broadcast_to(jnp.exp(c * per_head(row_head, 0)), (MIX_W, MIX_W))
        rdec_ref[1] = jnp.broadcast_to(jnp.exp(c * per_head(row_head, 1)), (MIX_W, MIX_W))

    ones_bd = jnp.where(blockdiag, 1.0, 0.0).astype(BF16)

    def chunk(ref, n):
        return ref[0, pl.ds(pl.multiple_of(n * c, c), c), :]

    def fwd_body(n, carry):
        qn, kn, vn = chunk(q_ref, n), chunk(k_ref, n), chunk(v_ref, n)
        acc = _dot((qn.astype(F32) * vec_ref[0]).astype(BF16), st_ref[...].astype(BF16))
        for h in range(N_HEADS):
            hm = lane_head == h
            s = _dot_nt(jnp.where(hm, qn, jnp.zeros_like(qn)), kn)
            p = (s * dmat_ref[h]).astype(BF16)
            acc = acc + _dot(p, jnp.where(hm, vn, jnp.zeros_like(vn)))
        acc_ref[pl.ds(pl.multiple_of(n * c, c), c), :] = acc
        kv = _dot_tn((kn.astype(F32) * vec_ref[1]).astype(BF16), vn)
        st_ref[...] = st_ref[...] * rdec_ref[0] + jnp.where(blockdiag, kv, 0.0)
        return carry

    st_ref[...] = jnp.zeros_like(st_ref)
    lax.fori_loop(0, n_chunks, fwd_body, 0, unroll=RET_UNROLL)

    def bwd_body(t, carry):
        n = n_chunks - 1 - t
        qn, kn, vn = chunk(q_ref, n), chunk(k_ref, n), chunk(v_ref, n)
        r0 = pl.multiple_of(n * c, c)
        o = acc_ref[pl.ds(r0, c), :] + _dot((qn.astype(F32) * vec_ref[2]).astype(BF16),
                                           st_ref[...].astype(BF16))
        o2 = o * o
        hi = o2.astype(BF16)
        lo = (o2 - hi.astype(F32)).astype(BF16)
        ms = (_dot(hi, ones_bd) + _dot(lo, ones_bd)) * (1.0 / HEAD_DIM)
        gate = chunk(g_ref, n).astype(F32)
        o_ref[0, pl.ds(r0, c), :] = (gate * _sigmoid(gate) * (o * lax.rsqrt(ms + EPS))).astype(BF16)
        kv = _dot_tn((kn.astype(F32) * vec_ref[3]).astype(BF16), vn)
        st_ref[...] = st_ref[...] * rdec_ref[1] + jnp.where(blockdiag, kv, 0.0)
        return carry

    st_ref[...] = jnp.zeros_like(st_ref)
    lax.fori_loop(0, n_chunks, bwd_body, 0, unroll=RET_UNROLL)


def _retention(ret3d, lg):
    b, seq, _ = ret3d.shape
    spec = lambda col: pl.BlockSpec((1, seq, MIX_W), lambda i, col=col: (i, 0, col))
    return pl.pallas_call(
        _ret_kernel,
        out_shape=jax.ShapeDtypeStruct((b, seq, MIX_W), BF16),
        grid=(b,),
        in_specs=[pl.BlockSpec(memory_space=pltpu.SMEM), spec(0), spec(1), spec(2), spec(3)],
        out_specs=pl.BlockSpec((1, seq, MIX_W), lambda i: (i, 0, 0)),
        scratch_shapes=[
            pltpu.VMEM((seq, MIX_W), F32),
            pltpu.VMEM((MIX_W, MIX_W), F32),
            pltpu.VMEM((N_HEADS, RET_CHUNK, RET_CHUNK), F32),
            pltpu.VMEM((4, RET_CHUNK, MIX_W), F32),
            pltpu.VMEM((2, MIX_W, MIX_W), F32),
        ],
        compiler_params=_params("arbitrary"),
        name="retention",
    )(lg, ret3d, ret3d, ret3d, ret3d)


def _fourier_kernel(cs_ref, csh_ref, ss_ref, ssh_ref, u_ref, cc_ref, sc_ref, wf_ref, lo_ref, hi_ref, *, scale):
    half = u_ref.shape[1] // 2
    tr = cs_ref.shape[0]
    blk = FLIP_BLOCK
    ri = lax.broadcasted_iota(jnp.int32, (blk, blk), 0)
    ci = lax.broadcasted_iota(jnp.int32, (blk, blk), 1)
    flip_shift = jnp.where(ri + ci == blk, 1.0, 0.0).astype(BF16)
    flip = jnp.where(ri + ci == blk - 1, 1.0, 0.0).astype(BF16)
    first = lax.broadcasted_iota(jnp.int32, (blk, 1), 0) == 0
    nb = half // blk
    upper = lambda c: u_ref[0, half + c * blk:half + (c + 1) * blk, :]
    plus, minus = [], []
    for a in range(nb):
        row0 = upper(nb - a)[0:1].astype(F32) if a > 0 else jnp.zeros((1, MIX_W), F32)
        rev = jnp.where(first, row0, _dot(flip_shift, upper(nb - a - 1)))
        lo = u_ref[0, a * blk:(a + 1) * blk, :].astype(F32)
        plus.append((lo + rev).astype(BF16))
        minus.append((lo - rev).astype(BF16))
    u_mid = u_ref[0, half:half + 1, :].astype(F32)
    ext = tr + HALO
    row = lax.broadcasted_iota(jnp.int32, (ext, 1), 0)
    alt = jnp.where((row & 1) == 0, 1.0, -1.0)
    cs = jnp.concatenate([cs_ref[...], csh_ref[...]], axis=0)
    ss = jnp.concatenate([ss_ref[...], ssh_ref[...]], axis=0)
    z1 = (_dot(cs, jnp.concatenate(plus, axis=0)) + alt * u_mid).astype(BF16)
    z2 = _dot(ss, jnp.concatenate(minus, axis=0)).astype(BF16)
    a1 = _dot(z1, cc_ref[...])
    a2 = _dot(z2, sc_ref[...])
    lo_ref[0] = _dot(((a1 - a2)[0:tr] * scale).astype(BF16), wf_ref[...]).astype(BF16)
    mirrored = pltpu.roll(a1 + a2, ext - 1, 0)[0:tr]
    hi = _dot((mirrored * scale).astype(BF16), wf_ref[...]).astype(BF16)
    nbt = tr // blk
    for c in range(nbt):
        src = hi[(nbt - 1 - c) * blk:(nbt - c) * blk]
        hi_ref[0, c * blk:(c + 1) * blk, :] = _dot(flip, src).astype(BF16)


def _fourier(fu3d, cs, ss, cc, sc, wf):
    b, seq, _ = fu3d.shape
    half = seq // 2
    tr = min(FOUR_TR, half)
    steps = half // tr
    const = lambda i, j: (0, 0)
    main = pl.BlockSpec((tr, half), lambda i, j: (i, 0))
    halo = pl.BlockSpec((HALO, half), lambda i, j: ((i + 1) * (tr // HALO), 0))
    return pl.pallas_call(
        functools.partial(_fourier_kernel, scale=1.0 / math.sqrt(seq * HEAD_DIM)),
        out_shape=[jax.ShapeDtypeStruct((b, half, MIX_W), BF16), jax.ShapeDtypeStruct((b, half, MIX_W), BF16)],
        grid=(steps, b),
        in_specs=[
            main, halo, main, halo,
            pl.BlockSpec((1, seq, MIX_W), lambda i, j: (j, 0, 0)),
            pl.BlockSpec((MIX_W, MIX_W), const),
            pl.BlockSpec((MIX_W, MIX_W), const),
            pl.BlockSpec((MIX_W, MIX_W), const),
        ],
        out_specs=[pl.BlockSpec((1, tr, MIX_W), lambda i, j: (j, i, 0)),
                   pl.BlockSpec((1, tr, MIX_W), lambda i, j: (j, steps - 1 - i, 0))],
        compiler_params=_params("arbitrary", "arbitrary"),
        name="fourier",
    )(cs, cs, ss, ss, fu3d, cc, sc, wf)


def _dil_kernel(q_ref, k_ref, v_ref, o_ref, st_ref, bias_ref, *, sub_len, ts, rb, tq, win):
    j = pl.program_id(2)
    nblk = ts // tq
    lane_head = lax.broadcasted_iota(jnp.int32, (1, MIX_W), 1) // HEAD_DIM
    stat_slot = lax.broadcasted_iota(jnp.int32, (1, LANES), 1) // STAT_W

    @pl.when((pl.program_id(0) == 0) & (pl.program_id(1) == 0) & (j == 0))
    def _bias():
        rel = (lax.broadcasted_iota(jnp.int32, (N_HEADS * tq, win), 0) & (tq - 1)) \
            - lax.broadcasted_iota(jnp.int32, (N_HEADS * tq, win), 1)
        for i in range(3):
            bias_ref[i] = jnp.where(jnp.abs(rel + i * DIL_RADIUS) <= DIL_RADIUS, 0.0, NEG)

    for rr in range(rb):
        for blk in range(nblk):
            q0 = j * ts + blk * tq
            if win == sub_len:
                ws = 0
                bias = bias_ref[blk * tq // DIL_RADIUS]
            else:
                ws = pl.multiple_of(jnp.clip(q0 - DIL_RADIUS, 0, sub_len - win), DIL_RADIUS)
                bias = bias_ref[1] if 0 < blk < nblk - 1 else bias_ref[(q0 - ws) // DIL_RADIUS]
            q = q_ref[0, rr, blk * tq:(blk + 1) * tq, :]
            kw = k_ref[0, rr, pl.ds(ws, win), :]
            vw = v_ref[0, rr, pl.ds(ws, win), :]
            zero = jnp.zeros_like(q)
            qs = jnp.concatenate([jnp.where(lane_head == h, q, zero) for h in range(N_HEADS)], axis=0)
            s = _dot_nt(qs, kw) + bias
            m = jnp.max(s, axis=-1, keepdims=True)
            p = jnp.exp2(s - m)
            den = jnp.sum(p, axis=-1, keepdims=True)
            r = _dot(p.astype(BF16), vw)
            o = jnp.zeros((tq, MIX_W), F32)
            st = jnp.zeros((tq, LANES), F32)
            for h in range(N_HEADS):
                o = jnp.where(lane_head == h, r[h * tq:(h + 1) * tq], o)
                st = jnp.where(stat_slot == 2 * h, m[h * tq:(h + 1) * tq], st)
                st = jnp.where(stat_slot == 2 * h + 1, den[h * tq:(h + 1) * tq], st)
            o_ref[0, rr, blk * tq:(blk + 1) * tq, :] = o.astype(BF16)
            st_ref[0, rr, blk * tq:(blk + 1) * tq, :] = st


def _dilated(dg, group):
    b, dil, sub_len, _ = dg.shape
    tq = min(DIL_TQ, sub_len)
    win = min(tq + 2 * DIL_RADIUS, sub_len)
    ts = min(sub_len, DIL_STEP_ROWS)
    rb = min(dil, DIL_STEP_ROWS // ts)
    part = lambda c: (lambda i, r, j: (i, r, 0, c))
    return pl.pallas_call(
        functools.partial(_dil_kernel, sub_len=sub_len, ts=ts, rb=rb, tq=tq, win=win),
        out_shape=[jax.ShapeDtypeStruct((b, dil, sub_len, MIX_W), BF16),
                   jax.ShapeDtypeStruct((b, dil, sub_len, LANES), F32)],
        grid=(b, dil // rb, sub_len // ts),
        in_specs=[
            pl.BlockSpec((1, rb, ts, MIX_W), lambda i, r, j: (i, r, j, 0)),
            pl.BlockSpec((1, rb, sub_len, MIX_W), part(1)),
            pl.BlockSpec((1, rb, sub_len, MIX_W), part(2)),
        ],
        out_specs=[pl.BlockSpec((1, rb, ts, MIX_W), lambda i, r, j: (i, r, j, 0)),
                   pl.BlockSpec((1, rb, ts, LANES), lambda i, r, j: (i, r, j, 0))],
        scratch_shapes=[pltpu.VMEM((3, N_HEADS * tq, win), F32)],
        compiler_params=_params("arbitrary", "arbitrary", "arbitrary"),
        name=f"dilated{group}",
    )(dg, dg, dg)


def _mla_kernel(q_ref, k_ref, v_ref, o_ref):
    lane_head = lax.broadcasted_iota(jnp.int32, (1, MIX_W), 1) // HEAD_DIM
    v = v_ref[0]
    for b0 in range(0, q_ref.shape[1], MLA_TQ):
        out = jnp.zeros((MLA_TQ, MIX_W), F32)
        for h in range(N_HEADS):
            qh = q_ref[0, b0:b0 + MLA_TQ, h * LANES:(h + 1) * LANES]
            s = _dot_nt(qh, k_ref[0, :, h * LANES:(h + 1) * LANES])
            m = jnp.max(s, axis=-1, keepdims=True)
            p = jnp.exp2(s - m)
            den = jnp.sum(p, axis=-1, keepdims=True)
            o = _dot(p.astype(BF16), v) * (1.0 / den)
            out = jnp.where(lane_head == h, o, out)
        o_ref[0, b0:b0 + MLA_TQ, :] = out.astype(BF16)


def _mla(mq3d, mk3d, mv3d):
    b, seq, _ = mq3d.shape
    ts = max(MLA_TQ, min(seq, MLA_STEP_SCORES // seq))
    return pl.pallas_call(
        _mla_kernel,
        out_shape=jax.ShapeDtypeStruct((b, seq, MIX_W), BF16),
        grid=(b, seq // ts),
        in_specs=[
            pl.BlockSpec((1, ts, N_HEADS * LANES), lambda i, j: (i, j, 0)),
            pl.BlockSpec((1, seq, N_HEADS * LANES), lambda i, j: (i, 0, 0)),
            pl.BlockSpec((1, seq, MIX_W), lambda i, j: (i, 0, 0)),
        ],
        out_specs=pl.BlockSpec((1, ts, MIX_W), lambda i, j: (i, j, 0)),
        compiler_params=_params("arbitrary", "arbitrary"),
        name="mla",
    )(mq3d, mk3d, mv3d)


def _outproj_kernel(x_ref, mod_ref, ro_ref, flo_ref, fhi_ref, d0_ref, d1_ref, d2_ref, l0_ref, l1_ref, l2_ref,
                    mo_ref, w_ref, gpm_ref, gpf_ref, x1_ref, h2_ref, so1_ref, sl1_ref, so2_ref, sl2_ref,
                    *, tiles_per_seq):
    tm = x_ref.shape[0]
    in_lower = (pl.program_id(0) % tiles_per_seq) < tiles_per_seq // 2
    fo = jnp.where(in_lower, flo_ref[0], fhi_ref[0])

    def natural_order(o_ref, l_ref, so_ref, sl_ref):
        dil = o_ref.shape[1]
        if dil == 1:
            return o_ref[0, 0].astype(F32), l_ref[0, 0]
        n = tm // dil
        for r in range(dil):
            o = o_ref[0, r].astype(F32)
            for c in range(MIX_W // LANES):
                so_ref[c, pl.ds(r, n, stride=dil), :] = o[:, c * LANES:(c + 1) * LANES]
            sl_ref[pl.ds(r, n, stride=dil), :] = l_ref[0, r]
        return jnp.concatenate([so_ref[c] for c in range(MIX_W // LANES)], axis=1), sl_ref[...]

    o0, l0 = natural_order(d0_ref, l0_ref, None, None)
    o1, l1 = natural_order(d1_ref, l1_ref, so1_ref, sl1_ref)
    o2, l2 = natural_order(d2_ref, l2_ref, so2_ref, sl2_ref)
    m = jnp.maximum(l0, jnp.maximum(l1, l2))
    e = [jnp.exp2(l - m) for l in (l0, l1, l2)]
    den = sum(eg * pltpu.roll(l, LANES - STAT_W, 1) for eg, l in zip(e, (l0, l1, l2)))
    inv = 1.0 / den
    is_max_lane = (lax.broadcasted_iota(jnp.int32, (1, LANES), 1) & STAT_W) == 0
    src = lax.broadcasted_iota(jnp.int32, (LANES, MIX_W), 0)
    dst = lax.broadcasted_iota(jnp.int32, (LANES, MIX_W), 1)
    spread = jnp.where(src == (dst // HEAD_DIM) * (LANES // N_HEADS), 1.0, 0.0).astype(BF16)
    od = sum(_dot(jnp.where(is_max_lane, eg * inv, 0.0).astype(BF16), spread) * og
             for eg, og in zip(e, (o0, o1, o2))).astype(BF16)
    y = (_dot(ro_ref[...], w_ref[0:256, :]) + _dot(fo, w_ref[256:512, :])
         + _dot(od, w_ref[512:768, :]) + _dot(mo_ref[...], w_ref[768:1024, :]))
    g1 = mod_ref[0, 2:3, :]
    sh2 = mod_ref[0, 3:4, :]
    sc2 = mod_ref[0, 4:5, :]
    x1 = x_ref[...] + g1 * _rms(y, gpm_ref[...])
    x1_ref[...] = x1
    h2_ref[...] = (_rms(x1, gpf_ref[...]) * (1.0 + sc2) + sh2).astype(BF16)


def _outproj(x2d, mod, ro, fo_halves, d_o, d_l, mo, w_out, g_post_mix, g_pre_ffn, seq):
    t = x2d.shape[0]
    tm = ROW_TILE
    tps = seq // tm
    row = lambda i: (i, 0)
    const = lambda i: (0, 0)
    mix = pl.BlockSpec((tm, MIX_W), row)
    res = lambda a: pl.BlockSpec((1, a.shape[1], tm // a.shape[1], a.shape[3]), lambda i: (i // tps, 0, i % tps, 0))
    hps = tps // 2
    flo = pl.BlockSpec((1, tm, MIX_W), lambda i: (i // tps, jnp.minimum(i % tps, hps - 1), 0))
    fhi = pl.BlockSpec((1, tm, MIX_W), lambda i: (i // tps, jnp.maximum(i % tps - hps, 0), 0))
    return pl.pallas_call(
        functools.partial(_outproj_kernel, tiles_per_seq=tps),
        out_shape=[jax.ShapeDtypeStruct((t, D_MODEL), F32), jax.ShapeDtypeStruct((t, D_MODEL), BF16)],
        grid=(t // tm,),
        in_specs=[
            pl.BlockSpec((tm, D_MODEL), row),
            pl.BlockSpec((1, 6, D_MODEL), lambda i: (i // tps, 0, 0)),
            mix, flo, fhi, res(d_o[0]), res(d_o[1]), res(d_o[2]), res(d_l[0]), res(d_l[1]), res(d_l[2]), mix,
            pl.BlockSpec((D_MODEL, D_MODEL), const),
            pl.BlockSpec((1, D_MODEL), const),
            pl.BlockSpec((1, D_MODEL), const),
        ],
        out_specs=[pl.BlockSpec((tm, D_MODEL), row), pl.BlockSpec((tm, D_MODEL), row)],
        scratch_shapes=[pltpu.VMEM((MIX_W // LANES, tm, LANES), F32), pltpu.VMEM((tm, LANES), F32),
                        pltpu.VMEM((MIX_W // LANES, tm, LANES), F32), pltpu.VMEM((tm, LANES), F32)],
        compiler_params=_params("arbitrary"),
        name="outproj",
    )(x2d, mod, ro, fo_halves[0], fo_halves[1], d_o[0], d_o[1], d_o[2], d_l[0], d_l[1], d_l[2], mo, w_out,
      g_post_mix, g_pre_ffn)


def _ffn_kernel(hp_ref, hc_ref, hn_ref, x1_ref, mod_ref, wu_ref, cw_ref, cb_ref, wd_ref, g_ref, o_ref,
                gate_ref, *, tiles_per_seq):
    tm = hc_ref.shape[0]
    t = pl.program_id(0) % tiles_per_seq
    hp = jnp.where(t == 0, jnp.zeros_like(hp_ref[...]), hp_ref[...])
    hn = jnp.where(t == tiles_per_seq - 1, jnp.zeros_like(hn_ref[...]), hn_ref[...])
    he = jnp.concatenate([hp, hc_ref[...], hn], axis=0)

    ext = tm + 2 * HALO

    def conv(c0):
        u = _dot(he, wu_ref[:, c0:c0 + FFN_CHUNK])
        w = cw_ref[:, c0:c0 + FFN_CHUNK]
        prev = pltpu.roll(u, 1, 0)[HALO:HALO + tm]
        nxt = pltpu.roll(u, ext - 1, 0)[HALO:HALO + tm]
        return prev * w[0:1] + u[HALO:HALO + tm] * w[1:2] + nxt * w[2:3] + cb_ref[:, c0:c0 + FFN_CHUNK]

    for c in range(D_FF // FFN_CHUNK):
        a = conv(c * FFN_CHUNK)
        bu = conv(D_FF + c * FFN_CHUNK)
        gate_ref[:, c * FFN_CHUNK:(c + 1) * FFN_CHUNK] = (a * _sigmoid(a) * bu).astype(BF16)
    acc = _dot(gate_ref[...], wd_ref[...])
    g2 = mod_ref[0, 5:6, :]
    o_ref[...] = x1_ref[...] + g2 * _rms(acc, g_ref[...])


def _ffn(h2, x1, mod, w_up, conv_w, conv_b, w_down, g_post_ffn, seq):
    t = x1.shape[0]
    tm = ROW_TILE
    tps = seq // tm
    hb = tm // HALO
    row = lambda i: (i, 0)
    const = lambda i: (0, 0)
    resident = lambda shape: pl.BlockSpec(shape, const, pipeline_mode=pl.Buffered(1))
    return pl.pallas_call(
        functools.partial(_ffn_kernel, tiles_per_seq=tps),
        out_shape=jax.ShapeDtypeStruct((t, D_MODEL), F32),
        grid=(t // tm,),
        in_specs=[
            pl.BlockSpec((HALO, D_MODEL), lambda i: (jnp.maximum(i * hb - 1, 0), 0)),
            pl.BlockSpec((tm, D_MODEL), row),
            pl.BlockSpec((HALO, D_MODEL), lambda i: (jnp.minimum((i + 1) * hb, t // HALO - 1), 0)),
            pl.BlockSpec((tm, D_MODEL), row),
            pl.BlockSpec((1, 6, D_MODEL), lambda i: (i // tps, 0, 0)),
            resident((D_MODEL, 2 * D_FF)),
            pl.BlockSpec((3, 2 * D_FF), const),
            pl.BlockSpec((1, 2 * D_FF), const),
            resident((D_FF, D_MODEL)),
            pl.BlockSpec((1, D_MODEL), const),
        ],
        out_specs=pl.BlockSpec((tm, D_MODEL), row),
        scratch_shapes=[pltpu.VMEM((tm, D_FF), BF16)],
        compiler_params=_params("arbitrary"),
        name="ffn",
    )(h2, h2, h2, x1, mod, w_up, conv_w, conv_b, w_down, g_post_ffn)


def _rope_tables(seq):
    pos = jnp.arange(seq, dtype=F32)[:, None]
    lane = np.arange(LANES)
    cols = []
    for theta, rot in ((RET_THETA, HEAD_DIM), (ROPE_THETA, PARTIAL_ROT), (ROPE_THETA, MLA_ROPE)):
        half = rot // 2
        inv = jnp.power(theta, -jnp.arange(half, dtype=F32) * 2.0 / rot)
        ang = pos * inv[lane % half][None, :]
        cols += [jnp.cos(ang), jnp.sin(ang)]
    return jnp.concatenate(cols, axis=1)


def _dft_tables(seq):
    n2 = 64
    n1 = seq // (2 * n2)
    k = np.arange(seq)[:, None]
    a = 2.0 * np.pi * ((k * np.arange(n1)[None, :] * n2) % seq) / seq
    b = 2.0 * np.pi * ((k * np.arange(n2)[None, :]) % seq) / seq
    ca, sa = jnp.asarray(np.cos(a), F32)[:, :, None], jnp.asarray(np.sin(a), F32)[:, :, None]
    cb, sb = jnp.asarray(np.cos(b), F32)[:, None, :], jnp.asarray(np.sin(b), F32)[:, None, :]
    cs = (ca * cb - sa * sb).reshape(seq, seq // 2).astype(BF16)
    ss = (sa * cb + ca * sb).reshape(seq, seq // 2).astype(BF16)
    return cs, ss


def _block_diag(blocks):
    n = len(blocks)
    rows = [jnp.concatenate([blocks[i] if i == j else jnp.zeros_like(blocks[i]) for j in range(n)], axis=1)
            for i in range(n)]
    return jnp.concatenate(rows, axis=0)


def _perm_w_in(w_in):
    ret = w_in[:, 0:1280]
    dq, dk, dv = w_in[:, 1280:2048], w_in[:, 2048:2816], w_in[:, 2816:3584]
    groups = [jnp.concatenate([m[:, g * MIX_W:(g + 1) * MIX_W] for m in (dq, dk, dv)], axis=1)
              for g in range(N_DIL_GROUPS)]
    pad = jnp.zeros((D_MODEL, D_IN_PAD - w_in.shape[1]), w_in.dtype)
    return jnp.concatenate([ret] + groups + [w_in[:, 3584:], pad], axis=1).astype(BF16)


def _mla_weights(w_qb, w_kvb):
    qh = w_qb.reshape(Q_LORA, N_HEADS, MLA_NOPE + MLA_ROPE)
    wq = jnp.pad(qh, ((0, 0), (0, 0), (0, LANES - MLA_NOPE - MLA_ROPE))).reshape(Q_LORA, N_HEADS * LANES)
    kvh = w_kvb.reshape(KV_LORA, N_HEADS, MLA_NOPE + HEAD_DIM)
    wk = jnp.pad(kvh[:, :, :MLA_NOPE], ((0, 0), (0, 0), (0, LANES - MLA_NOPE))).reshape(KV_LORA, N_HEADS * LANES)
    wv = kvh[:, :, MLA_NOPE:].reshape(KV_LORA, MIX_W)
    place = np.zeros((LANES, N_HEADS * LANES), np.float32)
    for h in range(N_HEADS):
        for r in range(MLA_ROPE):
            place[r, h * LANES + MLA_NOPE + r] = 1.0
    return wq.astype(BF16), wk.astype(BF16), wv.astype(BF16), jnp.asarray(place, BF16)


def _trunk(x, mods, layers, shared):
    b, seq, _ = x.shape
    t = b * seq
    x2d = x.reshape(t, D_MODEL)
    tab = _rope_tables(seq)
    cs, ss = _dft_tables(seq)
    for mod, lw in zip(mods, layers):
        ret, fu, dg0, dg1, dg2, mq, mk, mv = _inproj(x2d, mod, lw["g_pre_mix"], lw["w_in"], tab, lw["q_norm"],
                                                     lw["kv_norm"], lw["wq"], lw["wk"], lw["wv"],
                                                     shared["place"], seq)
        ro = _retention(ret.reshape(b, seq, 1024), lw["lg"]).reshape(t, MIX_W)
        fo = _fourier(fu.reshape(b, seq, MIX_W), cs, ss, shared["cc"], shared["sc"], lw["wf"])
        d_o, d_l = zip(*[_dilated(dg, g) for g, dg in enumerate((dg0, dg1, dg2))])
        mo = _mla(mq.reshape(b, seq, 512), mk.reshape(b, seq, 512), mv.reshape(b, seq, MIX_W)).reshape(t, MIX_W)
        x1, h2 = _outproj(x2d, mod, ro, fo, d_o, d_l, mo, lw["w_out"], lw["g_post_mix"], lw["g_pre_ffn"], seq)
        x2d = _ffn(h2, x1, mod, lw["w_up"], lw["conv_w"], lw["conv_b"], lw["w_down"], lw["g_post_ffn"], seq)
    return x2d.reshape(b, seq, D_MODEL)


def kernel(x_prompt, x_sample, c_prompt, c_sample, w_ada, b_ada, norm_pre_mix, w_in, ret_decay_fwd,
           ret_decay_bwd, w_fmix, mla_q_norm, mla_w_qb, mla_kv_norm, mla_w_kvb, w_out, norm_post_mix,
           norm_pre_ffn, w_up, conv_w, conv_b, w_down, norm_post_ffn):
    depth = w_in.shape[0]
    nb_p, nb_s = c_prompt.shape[0], c_sample.shape[0]
    rows = -(-(nb_p + nb_s) // 8) * 8
    c_all = jnp.concatenate([c_prompt, c_sample, jnp.zeros((rows - nb_p - nb_s, D_MODEL), F32)], axis=0)
    mod_all = _ada(c_all, w_ada, b_ada)
    mods_p = [mod_all[l, :nb_p].reshape(nb_p, 6, D_MODEL) for l in range(depth)]
    mods_s = [mod_all[l, nb_p:nb_p + nb_s].reshape(nb_s, 6, D_MODEL) for l in range(depth)]

    c64 = 2.0 * np.pi * np.outer(np.arange(HEAD_DIM), np.arange(HEAD_DIM)) / HEAD_DIM
    shared = {
        "cc": _block_diag([jnp.asarray(np.cos(c64), BF16)] * N_HEADS),
        "sc": _block_diag([jnp.asarray(np.sin(c64), BF16)] * N_HEADS),
    }
    layers = []
    for l in range(depth):
        wq, wk, wv, place = _mla_weights(mla_w_qb[l], mla_w_kvb[l])
        shared["place"] = place
        layers.append({
            "g_pre_mix": norm_pre_mix[l][None, :],
            "w_in": _perm_w_in(w_in[l]),
            "lg": jnp.stack([jax.nn.log_sigmoid(ret_decay_fwd[l]), jax.nn.log_sigmoid(ret_decay_bwd[l])]),
            "wf": _block_diag([w_fmix[l, g] for g in range(N_HEADS)]).astype(BF16),
            "q_norm": mla_q_norm[l][None, :],
            "kv_norm": mla_kv_norm[l][None, :],
            "wq": wq, "wk": wk, "wv": wv,
            "w_out": w_out[l].astype(BF16),
            "g_post_mix": norm_post_mix[l][None, :],
            "g_pre_ffn": norm_pre_ffn[l][None, :],
            "w_up": w_up[l].astype(BF16),
            "conv_w": conv_w[l],
            "conv_b": conv_b[l][None, :],
            "w_down": w_down[l].astype(BF16),
            "g_post_ffn": norm_post_ffn[l][None, :],
        })
    y_prompt = _trunk(x_prompt, mods_p, layers, shared)
    y_sample = _trunk(x_sample, mods_s, layers, shared)
    return (y_prompt, y_sample)
```

```python
import functools
import math

import numpy as np
import jax
import jax.numpy as jnp
from jax import lax
from jax.experimental import pallas as pl
from jax.experimental.pallas import tpu as pltpu

F32 = jnp.float32
BF16 = jnp.bfloat16

D_MODEL = 1024
HEAD_DIM = 64
N_HEADS = 4
MIX_W = N_HEADS * HEAD_DIM
DIL_PAIRS = ((128, 1), (512, 4), (2048, 16))
N_DIL_GROUPS = 3
DIL_RADIUS = 64
STAT_W = 16
MLA_NOPE = 64
MLA_ROPE = 32
Q_LORA = 256
KV_LORA = 128
D_FF = 2816
ROPE_THETA = 500000.0
RET_THETA = 10000.0
PARTIAL_ROT = HEAD_DIM // 4
EPS = 1e-6
NEG = -1e30
LOG2E = math.log2(math.e)

LANES = 128
D_IN_PAD = 4096
MLA_OFF = 3584
ROW_TILE = 1024
RET_CHUNK = 256
RET_UNROLL = 8
DIL_TQ = 128
DIL_STEP_ROWS = 2048
FOUR_TR = 1024
FLIP_BLOCK = 256
MLA_TQ = 256
MLA_STEP_SCORES = 2 ** 22
FFN_CHUNK = 256
HALO = 16
VMEM_LIMIT = 56 * 1024 * 1024


def _params(*sem):
    return pltpu.CompilerParams(dimension_semantics=sem, vmem_limit_bytes=VMEM_LIMIT)


def _rms(x, g):
    return x * lax.rsqrt(jnp.mean(x * x, axis=-1, keepdims=True) + EPS) * g


def _sigmoid(x):
    return 1.0 / (1.0 + jnp.exp(-x))


def _dot(a, b):
    return jnp.dot(a, b, preferred_element_type=F32)


def _dot_nt(a, b):
    return lax.dot_general(a, b, (((1,), (1,)), ((), ())), preferred_element_type=F32)


def _dot_tn(a, b):
    return lax.dot_general(a, b, (((0,), (0,)), ((), ())), preferred_element_type=F32)


def _ada_kernel(c_ref, w_ref, b_ref, o_ref):
    c = c_ref[...]
    cond = (c * _sigmoid(c)).astype(BF16)
    o_ref[0] = _dot(cond, w_ref[0].astype(BF16)) + b_ref[0]


def _ada(c_all, w_ada, b_ada):
    depth, _, n = w_ada.shape
    rows = c_all.shape[0]
    tn = 1536
    return pl.pallas_call(
        _ada_kernel,
        out_shape=jax.ShapeDtypeStruct((depth, rows, n), F32),
        grid=(depth, n // tn),
        in_specs=[
            pl.BlockSpec((rows, D_MODEL), lambda l, j: (0, 0)),
            pl.BlockSpec((1, D_MODEL, tn), lambda l, j: (l, 0, j)),
            pl.BlockSpec((1, 1, tn), lambda l, j: (l, 0, j)),
        ],
        out_specs=pl.BlockSpec((1, rows, tn), lambda l, j: (l, 0, j)),
        compiler_params=_params("arbitrary", "arbitrary"),
        name="ada",
    )(c_all, w_ada, b_ada.reshape(depth, 1, n))


def _inproj_kernel(x_ref, mod_ref, g_ref, w_ref, tab_ref, qn_ref, kvn_ref, wq_ref, wk_ref, wv_ref,
                   pk_ref, ret_ref, fu_ref, d0_ref, d1_ref, d2_ref, mq_ref, mk_ref, mv_ref, scr_ref, scr2_ref):
    x = x_ref[...]
    sh = mod_ref[0, 0:1, :]
    sc = mod_ref[0, 1:2, :]
    hb = (_rms(x, g_ref[...]) * (1.0 + sc) + sh).astype(BF16)

    def mm(c0, c1):
        return _dot(hb, w_ref[:, c0:c1])

    lane = lax.broadcasted_iota(jnp.int32, (1, LANES), 1)
    j64 = lane & (HEAD_DIM - 1)

    def make_rope(cos, sin, lo_mask, hi_mask, half):
        c = jnp.where(lo_mask | hi_mask, cos, 1.0)
        sa = jnp.where(lo_mask, -sin, 0.0)
        sb = jnp.where(hi_mask, sin, 0.0)

        def apply(z):
            return z * c + pltpu.roll(z, LANES - half, 1) * sa + pltpu.roll(z, half, 1) * sb
        return apply

    rope_ret = make_rope(tab_ref[:, 0:128], tab_ref[:, 128:256], j64 < 32, j64 >= 32, 32)
    rope_dil = make_rope(tab_ref[:, 256:384], tab_ref[:, 384:512], j64 < 8, (j64 >= 8) & (j64 < 16), 8)
    cos_m = tab_ref[:, 512:640]
    sin_m = tab_ref[:, 640:768]
    rope_kr = make_rope(cos_m, sin_m, lane < 16, (lane >= 16) & (lane < 32), 16)
    rope_mq = make_rope(cos_m, sin_m, (lane >= 64) & (lane < 80), (lane >= 80) & (lane < 96), 16)

    z = mm(MLA_OFF, D_IN_PAD)
    cqn = _rms(z[:, 0:Q_LORA], qn_ref[...]).astype(BF16)
    q = _dot(cqn, wq_ref[...])
    scale = (MLA_NOPE + MLA_ROPE) ** -0.5 * LOG2E
    for h in range(N_HEADS):
        r = rope_mq(q[:, h * LANES:(h + 1) * LANES]) * scale
        mq_ref[:, h * LANES:(h + 1) * LANES] = r.astype(BF16)
    ckvn = _rms(z[:, Q_LORA:Q_LORA + KV_LORA], kvn_ref[...]).astype(BF16)
    kr = rope_kr(z[:, 384:512]).astype(BF16)
    mk_ref[...] = (_dot(ckvn, wk_ref[...]) + _dot(kr, pk_ref[...])).astype(BF16)
    mv_ref[...] = _dot(ckvn, wv_ref[...]).astype(BF16)
    tm = x.shape[0]
    nslab = 3 * MIX_W // LANES
    for g, d_ref in reversed(list(enumerate((d0_ref, d1_ref, d2_ref)))):
        dil = DIL_PAIRS[g][1]
        base = g * 3 * MIX_W
        z = mm(1280 + base, 1280 + base + 3 * MIX_W)
        slabs = []
        for c in range(nslab):
            r = z[:, c * LANES:(c + 1) * LANES]
            if c < 4:
                r = rope_dil(r)
            if c < 2:
                r = r * (HEAD_DIM ** -0.5 * LOG2E)
            slabs.append(r)
        if dil == 1:
            d_ref[0, 0] = jnp.concatenate(slabs, axis=1).astype(BF16)
            continue
        for c in range(nslab):
            scr_ref[c] = slabs[c]
        n4 = tm // 4

        def rows4(ref, start, count):
            return jnp.concatenate([ref[c, pl.ds(start, count, stride=4), :] for c in range(nslab)], axis=1)

        if dil == 4:
            for r4 in range(4):
                d_ref[0, r4] = rows4(scr_ref, r4, n4).astype(BF16)
        else:
            for r4 in range(4):
                for c in range(nslab):
                    scr2_ref[c, r4 * n4:(r4 + 1) * n4, :] = scr_ref[c, pl.ds(r4, n4, stride=4), :]
            for r4 in range(4):
                for q4 in range(4):
                    d_ref[0, r4 + 4 * q4] = rows4(scr2_ref, r4 * n4 + q4, tm // 16).astype(BF16)
    z = mm(0, 512)
    for c in range(4):
        r = rope_ret(z[:, c * LANES:(c + 1) * LANES])
        if c >= 2:
            r = r * (HEAD_DIM ** -0.5)
        ret_ref[:, c * LANES:(c + 1) * LANES] = r.astype(BF16)
    ret_ref[:, 512:1024] = mm(512, 1024).astype(BF16)
    fu_ref[...] = mm(1024, 1280).astype(BF16)


def _inproj(x2d, mod, g_pre, w_in_p, tab, qn, kvn, wq_p, wk_p, wv_p, pk, seq):
    t = x2d.shape[0]
    b = t // seq
    tm = ROW_TILE
    tps = seq // tm
    const = lambda i: (0, 0)
    row = lambda i: (i, 0)
    flat = [(t, 1024), (t, MIX_W)]
    flat2 = [(t, 512), (t, 512), (t, MIX_W)]
    dils = [d for _, d in DIL_PAIRS]
    assert dils == [1, 4, 16]
    bf = lambda shp: jax.ShapeDtypeStruct(shp, BF16)
    dil_shapes = [bf((b, d, seq // d, 3 * MIX_W)) for d in dils]
    dil_specs = [pl.BlockSpec((1, d, tm // d, 3 * MIX_W), lambda i: (i // tps, 0, i % tps, 0)) for d in dils]
    return pl.pallas_call(
        _inproj_kernel,
        out_shape=[bf(s) for s in flat] + dil_shapes + [bf(s) for s in flat2],
        grid=(t // tm,),
        in_specs=[
            pl.BlockSpec((tm, D_MODEL), row),
            pl.BlockSpec((1, 6, D_MODEL), lambda i: (i // tps, 0, 0)),
            pl.BlockSpec((1, D_MODEL), const),
            pl.BlockSpec((D_MODEL, D_IN_PAD), const, pipeline_mode=pl.Buffered(1)),
            pl.BlockSpec((tm, 6 * LANES), lambda i: (i % tps, 0)),
            pl.BlockSpec((1, Q_LORA), const),
            pl.BlockSpec((1, KV_LORA), const),
            pl.BlockSpec((Q_LORA, 512), const),
            pl.BlockSpec((KV_LORA, 512), const),
            pl.BlockSpec((KV_LORA, MIX_W), const),
            pl.BlockSpec((LANES, 512), const),
        ],
        out_specs=[pl.BlockSpec((tm, s[1]), row) for s in flat] + dil_specs
                  + [pl.BlockSpec((tm, s[1]), row) for s in flat2],
        scratch_shapes=[pltpu.VMEM((3 * MIX_W // LANES, tm, LANES), F32),
                        pltpu.VMEM((3 * MIX_W // LANES, tm, LANES), F32)],
        compiler_params=_params("arbitrary"),
        name="inproj",
    )(x2d, mod, g_pre, w_in_p, tab, qn, kvn, wq_p, wk_p, wv_p, pk)


def _ret_kernel(lg_ref, q_ref, k_ref, v_ref, g_ref, o_ref, acc_ref, st_ref, dmat_ref, vec_ref, rdec_ref):
    c = RET_CHUNK
    seq = q_ref.shape[1]
    n_chunks = seq // c
    lane_head = lax.broadcasted_iota(jnp.int32, (1, MIX_W), 1) // HEAD_DIM
    row_head = lax.broadcasted_iota(jnp.int32, (MIX_W, 1), 0) // HEAD_DIM
    blockdiag = row_head == lane_head

    def per_head(idx, d):
        out = lg_ref[d, 0]
        for h in range(1, N_HEADS):
            out = jnp.where(idx == h, lg_ref[d, h], out)
        return out

    @pl.when(pl.program_id(0) == 0)
    def _tables():
        ri = lax.broadcasted_iota(jnp.int32, (c, c), 0)
        ci = lax.broadcasted_iota(jnp.int32, (c, c), 1)
        diff = (ri - ci).astype(F32)
        for h in range(N_HEADS):
            fwd = jnp.exp(jnp.where(diff >= 0, diff, 0.0) * lg_ref[0, h])
            bwd = jnp.exp(jnp.where(diff < 0, -diff, 0.0) * lg_ref[1, h])
            dmat_ref[h] = jnp.where(diff >= 0, fwd, bwd)
        pos = lax.broadcasted_iota(jnp.int32, (c, MIX_W), 0).astype(F32)
        lf = per_head(lane_head, 0)
        lb = per_head(lane_head, 1)
        vec_ref[0] = jnp.exp((pos + 1.0) * lf)
        vec_ref[1] = jnp.exp((c - 1.0 - pos) * lf)
        vec_ref[2] = jnp.exp((c - pos) * lb)
        vec_ref[3] = jnp.exp(pos * lb)
        rdec_ref[0] = jnp.broadcast_to(jnp.exp(c * per_head(row_head, 0)), (MIX_W, MIX_W))
        rdec_ref[1] = jnp.broadcast_to(jnp.exp(c * per_head(row_head, 1)), (MIX_W, MIX_W))

    ones_bd = jnp.where(blockdiag, 1.0, 0.0).astype(BF16)

    def chunk(ref, n):
        return ref[0, pl.ds(pl.multiple_of(n * c, c), c), :]

    def fwd_body(n, carry):
        qn, kn, vn = chunk(q_ref, n), chunk(k_ref, n), chunk(v_ref, n)
        acc = _dot((qn.astype(F32) * vec_ref[0]).astype(BF16), st_ref[...].astype(BF16))
        for h in range(N_HEADS):
            hm = lane_head == h
            s = _dot_nt(jnp.where(hm, qn, jnp.zeros_like(qn)), kn)
            p = (s * dmat_ref[h]).astype(BF16)
            acc = acc + _dot(p, jnp.where(hm, vn, jnp.zeros_like(vn)))
        acc_ref[pl.ds(pl.multiple_of(n * c, c), c), :] = acc
        kv = _dot_tn((kn.astype(F32) * vec_ref[1]).astype(BF16), vn)
        st_ref[...] = st_ref[...] * rdec_ref[0] + jnp.where(blockdiag, kv, 0.0)
        return carry

    st_ref[...] = jnp.zeros_like(st_ref)
    lax.fori_loop(0, n_chunks, fwd_body, 0, unroll=RET_UNROLL)

    def bwd_body(t, carry):
        n = n_chunks - 1 - t
        qn, kn, vn = chunk(q_ref, n), chunk(k_ref, n), chunk(v_ref, n)
        r0 = pl.multiple_of(n * c, c)
        o = acc_ref[pl.ds(r0, c), :] + _dot((qn.astype(F32) * vec_ref[2]).astype(BF16),
                                           st_ref[...].astype(BF16))
        o2 = o * o
        hi = o2.astype(BF16)
        lo = (o2 - hi.astype(F32)).astype(BF16)
        ms = (_dot(hi, ones_bd) + _dot(lo, ones_bd)) * (1.0 / HEAD_DIM)
        gate = chunk(g_ref, n).astype(F32)
        o_ref[0, pl.ds(r0, c), :] = (gate * _sigmoid(gate) * (o * lax.rsqrt(ms + EPS))).astype(BF16)
        kv = _dot_tn((kn.astype(F32) * vec_ref[3]).astype(BF16), vn)
        st_ref[...] = st_ref[...] * rdec_ref[1] + jnp.where(blockdiag, kv, 0.0)
        return carry

    st_ref[...] = jnp.zeros_like(st_ref)
    lax.fori_loop(0, n_chunks, bwd_body, 0, unroll=RET_UNROLL)


def _retention(ret3d, lg):
    b, seq, _ = ret3d.shape
    spec = lambda col: pl.BlockSpec((1, seq, MIX_W), lambda i, col=col: (i, 0, col))
    return pl.pallas_call(
        _ret_kernel,
        out_shape=jax.ShapeDtypeStruct((b, seq, MIX_W), BF16),
        grid=(b,),
        in_specs=[pl.BlockSpec(memory_space=pltpu.SMEM), spec(0), spec(1), spec(2), spec(3)],
        out_specs=pl.BlockSpec((1, seq, MIX_W), lambda i: (i, 0, 0)),
        scratch_shapes=[
            pltpu.VMEM((seq, MIX_W), F32),
            pltpu.VMEM((MIX_W, MIX_W), F32),
            pltpu.VMEM((N_HEADS, RET_CHUNK, RET_CHUNK), F32),
            pltpu.VMEM((4, RET_CHUNK, MIX_W), F32),
            pltpu.VMEM((2, MIX_W, MIX_W), F32),
        ],
        compiler_params=_params("arbitrary"),
        name="retention",
    )(lg, ret3d, ret3d, ret3d, ret3d)


def _fourier_kernel(cs_ref, csh_ref, ss_ref, ssh_ref, u_ref, cc_ref, sc_ref, wf_ref, lo_ref, hi_ref, *, scale):
    half = u_ref.shape[1] // 2
    tr = cs_ref.shape[0]
    blk = FLIP_BLOCK
    ri = lax.broadcasted_iota(jnp.int32, (blk, blk), 0)
    ci = lax.broadcasted_iota(jnp.int32, (blk, blk), 1)
    flip_shift = jnp.where(ri + ci == blk, 1.0, 0.0).astype(BF16)
    flip = jnp.where(ri + ci == blk - 1, 1.0, 0.0).astype(BF16)
    first = lax.broadcasted_iota(jnp.int32, (blk, 1), 0) == 0
    nb = half // blk
    upper = lambda c: u_ref[0, half + c * blk:half + (c + 1) * blk, :]
    plus, minus = [], []
    for a in range(nb):
        row0 = upper(nb - a)[0:1].astype(F32) if a > 0 else jnp.zeros((1, MIX_W), F32)
        rev = jnp.where(first, row0, _dot(flip_shift, upper(nb - a - 1)))
        lo = u_ref[0, a * blk:(a + 1) * blk, :].astype(F32)
        plus.append((lo + rev).astype(BF16))
        minus.append((lo - rev).astype(BF16))
    u_mid = u_ref[0, half:half + 1, :].astype(F32)
    ext = tr + HALO
    row = lax.broadcasted_iota(jnp.int32, (ext, 1), 0)
    alt = jnp.where((row & 1) == 0, 1.0, -1.0)
    cs = jnp.concatenate([cs_ref[...], csh_ref[...]], axis=0)
    ss = jnp.concatenate([ss_ref[...], ssh_ref[...]], axis=0)
    z1 = (_dot(cs, jnp.concatenate(plus, axis=0)) + alt * u_mid).astype(BF16)
    z2 = _dot(ss, jnp.concatenate(minus, axis=0)).astype(BF16)
    a1 = _dot(z1, cc_ref[...])
    a2 = _dot(z2, sc_ref[...])
    lo_ref[0] = _dot(((a1 - a2)[0:tr] * scale).astype(BF16), wf_ref[...]).astype(BF16)
    mirrored = pltpu.roll(a1 + a2, ext - 1, 0)[0:tr]
    hi = _dot((mirrored * scale).astype(BF16), wf_ref[...]).astype(BF16)
    nbt = tr // blk
    for c in range(nbt):
        src = hi[(nbt - 1 - c) * blk:(nbt - c) * blk]
        hi_ref[0, c * blk:(c + 1) * blk, :] = _dot(flip, src).astype(BF16)


def _fourier(fu3d, cs, ss, cc, sc, wf):
    b, seq, _ = fu3d.shape
    half = seq // 2
    tr = min(FOUR_TR, half)
    steps = half // tr
    const = lambda i, j: (0, 0)
    main = pl.BlockSpec((tr, half), lambda i, j: (i, 0))
    halo = pl.BlockSpec((HALO, half), lambda i, j: ((i + 1) * (tr // HALO), 0))
    return pl.pallas_call(
        functools.partial(_fourier_kernel, scale=1.0 / math.sqrt(seq * HEAD_DIM)),
        out_shape=[jax.ShapeDtypeStruct((b, half, MIX_W), BF16), jax.ShapeDtypeStruct((b, half, MIX_W), BF16)],
        grid=(steps, b),
        in_specs=[
            main, halo, main, halo,
            pl.BlockSpec((1, seq, MIX_W), lambda i, j: (j, 0, 0)),
            pl.BlockSpec((MIX_W, MIX_W), const),
            pl.BlockSpec((MIX_W, MIX_W), const),
            pl.BlockSpec((MIX_W, MIX_W), const),
        ],
        out_specs=[pl.BlockSpec((1, tr, MIX_W), lambda i, j: (j, i, 0)),
                   pl.BlockSpec((1, tr, MIX_W), lambda i, j: (j, steps - 1 - i, 0))],
        compiler_params=_params("arbitrary", "arbitrary"),
        name="fourier",
    )(cs, cs, ss, ss, fu3d, cc, sc, wf)


def _dil_kernel(q_ref, k_ref, v_ref, o_ref, st_ref, bias_ref, *, sub_len, ts, rb, tq, win):
    j = pl.program_id(2)
    nblk = ts // tq
    lane_head = lax.broadcasted_iota(jnp.int32, (1, MIX_W), 1) // HEAD_DIM
    stat_slot = lax.broadcasted_iota(jnp.int32, (1, LANES), 1) // STAT_W

    @pl.when((pl.program_id(0) == 0) & (pl.program_id(1) == 0) & (j == 0))
    def _bias():
        rel = (lax.broadcasted_iota(jnp.int32, (N_HEADS * tq, win), 0) & (tq - 1)) \
            - lax.broadcasted_iota(jnp.int32, (N_HEADS * tq, win), 1)
        for i in range(3):
            bias_ref[i] = jnp.where(jnp.abs(rel + i * DIL_RADIUS) <= DIL_RADIUS, 0.0, NEG)

    for rr in range(rb):
        for blk in range(nblk):
            q0 = j * ts + blk * tq
            if win == sub_len:
                ws = 0
                bias = bias_ref[blk * tq // DIL_RADIUS]
            else:
                ws = pl.multiple_of(jnp.clip(q0 - DIL_RADIUS, 0, sub_len - win), DIL_RADIUS)
                bias = bias_ref[1] if 0 < blk < nblk - 1 else bias_ref[(q0 - ws) // DIL_RADIUS]
            q = q_ref[0, rr, blk * tq:(blk + 1) * tq, :]
            kw = k_ref[0, rr, pl.ds(ws, win), :]
            vw = v_ref[0, rr, pl.ds(ws, win), :]
            zero = jnp.zeros_like(q)
            qs = jnp.concatenate([jnp.where(lane_head == h, q, zero) for h in range(N_HEADS)], axis=0)
            s = _dot_nt(qs, kw) + bias
            m = jnp.max(s, axis=-1, keepdims=True)
            p = jnp.exp2(s - m)
            den = jnp.sum(p, axis=-1, keepdims=True)
            r = _dot(p.astype(BF16), vw)
            o = jnp.zeros((tq, MIX_W), F32)
            st = jnp.zeros((tq, LANES), F32)
            for h in range(N_HEADS):
                o = jnp.where(lane_head == h, r[h * tq:(h + 1) * tq], o)
                st = jnp.where(stat_slot == 2 * h, m[h * tq:(h + 1) * tq], st)
                st = jnp.where(stat_slot == 2 * h + 1, den[h * tq:(h + 1) * tq], st)
            o_ref[0, rr, blk * tq:(blk + 1) * tq, :] = o.astype(BF16)
            st_ref[0, rr, blk * tq:(blk + 1) * tq, :] = st


def _dilated(dg, group):
    b, dil, sub_len, _ = dg.shape
    tq = min(DIL_TQ, sub_len)
    win = min(tq + 2 * DIL_RADIUS, sub_len)
    ts = min(sub_len, DIL_STEP_ROWS)
    rb = min(dil, DIL_STEP_ROWS // ts)
    part = lambda c: (lambda i, r, j: (i, r, 0, c))
    return pl.pallas_call(
        functools.partial(_dil_kernel, sub_len=sub_len, ts=ts, rb=rb, tq=tq, win=win),
        out_shape=[jax.ShapeDtypeStruct((b, dil, sub_len, MIX_W), BF16),
                   jax.ShapeDtypeStruct((b, dil, sub_len, LANES), F32)],
        grid=(b, dil // rb, sub_len // ts),
        in_specs=[
            pl.BlockSpec((1, rb, ts, MIX_W), lambda i, r, j: (i, r, j, 0)),
            pl.BlockSpec((1, rb, sub_len, MIX_W), part(1)),
            pl.BlockSpec((1, rb, sub_len, MIX_W), part(2)),
        ],
        out_specs=[pl.BlockSpec((1, rb, ts, MIX_W), lambda i, r, j: (i, r, j, 0)),
                   pl.BlockSpec((1, rb, ts, LANES), lambda i, r, j: (i, r, j, 0))],
        scratch_shapes=[pltpu.VMEM((3, N_HEADS * tq, win), F32)],
        compiler_params=_params("arbitrary", "arbitrary", "arbitrary"),
        name=f"dilated{group}",
    )(dg, dg, dg)


def _mla_kernel(q_ref, k_ref, v_ref, o_ref):
    lane_head = lax.broadcasted_iota(jnp.int32, (1, MIX_W), 1) // HEAD_DIM
    v = v_ref[0]
    for b0 in range(0, q_ref.shape[1], MLA_TQ):
        out = jnp.zeros((MLA_TQ, MIX_W), F32)
        for h in range(N_HEADS):
            qh = q_ref[0, b0:b0 + MLA_TQ, h * LANES:(h + 1) * LANES]
            s = _dot_nt(qh, k_ref[0, :, h * LANES:(h + 1) * LANES])
            m = jnp.max(s, axis=-1, keepdims=True)
            p = jnp.exp2(s - m)
            den = jnp.sum(p, axis=-1, keepdims=True)
            o = _dot(p.astype(BF16), v) * (1.0 / den)
            out = jnp.where(lane_head == h, o, out)
        o_ref[0, b0:b0 + MLA_TQ, :] = out.astype(BF16)


def _mla(mq3d, mk3d, mv3d):
    b, seq, _ = mq3d.shape
    ts = max(MLA_TQ, min(seq, MLA_STEP_SCORES // seq))
    return pl.pallas_call(
        _mla_kernel,
        out_shape=jax.ShapeDtypeStruct((b, seq, MIX_W), BF16),
        grid=(b, seq // ts),
        in_specs=[
            pl.BlockSpec((1, ts, N_HEADS * LANES), lambda i, j: (i, j, 0)),
            pl.BlockSpec((1, seq, N_HEADS * LANES), lambda i, j: (i, 0, 0)),
            pl.BlockSpec((1, seq, MIX_W), lambda i, j: (i, 0, 0)),
        ],
        out_specs=pl.BlockSpec((1, ts, MIX_W), lambda i, j: (i, j, 0)),
        compiler_params=_params("arbitrary", "arbitrary"),
        name="mla",
    )(mq3d, mk3d, mv3d)


def _outproj_kernel(x_ref, mod_ref, ro_ref, flo_ref, fhi_ref, d0_ref, d1_ref, d2_ref, l0_ref, l1_ref, l2_ref,
                    mo_ref, w_ref, gpm_ref, gpf_ref, x1_ref, h2_ref, so1_ref, sl1_ref, so2_ref, sl2_ref,
                    *, tiles_per_seq):
    tm = x_ref.shape[0]
    in_lower = (pl.program_id(0) % tiles_per_seq) < tiles_per_seq // 2
    fo = jnp.where(in_lower, flo_ref[0], fhi_ref[0])

    def natural_order(o_ref, l_ref, so_ref, sl_ref):
        dil = o_ref.shape[1]
        if dil == 1:
            return o_ref[0, 0].astype(F32), l_ref[0, 0]
        n = tm // dil
        for r in range(dil):
            o = o_ref[0, r].astype(F32)
            for c in range(MIX_W // LANES):
                so_ref[c, pl.ds(r, n, stride=dil), :] = o[:, c * LANES:(c + 1) * LANES]
            sl_ref[pl.ds(r, n, stride=dil), :] = l_ref[0, r]
        return jnp.concatenate([so_ref[c] for c in range(MIX_W // LANES)], axis=1), sl_ref[...]

    o0, l0 = natural_order(d0_ref, l0_ref, None, None)
    o1, l1 = natural_order(d1_ref, l1_ref, so1_ref, sl1_ref)
    o2, l2 = natural_order(d2_ref, l2_ref, so2_ref, sl2_ref)
    m = jnp.maximum(l0, jnp.maximum(l1, l2))
    e = [jnp.exp2(l - m) for l in (l0, l1, l2)]
    den = sum(eg * pltpu.roll(l, LANES - STAT_W, 1) for eg, l in zip(e, (l0, l1, l2)))
    inv = 1.0 / den
    is_max_lane = (lax.broadcasted_iota(jnp.int32, (1, LANES), 1) & STAT_W) == 0
    src = lax.broadcasted_iota(jnp.int32, (LANES, MIX_W), 0)
    dst = lax.broadcasted_iota(jnp.int32, (LANES, MIX_W), 1)
    spread = jnp.where(src == (dst // HEAD_DIM) * (LANES // N_HEADS), 1.0, 0.0).astype(BF16)
    od = sum(_dot(jnp.where(is_max_lane, eg * inv, 0.0).astype(BF16), spread) * og
             for eg, og in zip(e, (o0, o1, o2))).astype(BF16)
    y = (_dot(ro_ref[...], w_ref[0:256, :]) + _dot(fo, w_ref[256:512, :])
         + _dot(od, w_ref[512:768, :]) + _dot(mo_ref[...], w_ref[768:1024, :]))
    g1 = mod_ref[0, 2:3, :]
    sh2 = mod_ref[0, 3:4, :]
    sc2 = mod_ref[0, 4:5, :]
    x1 = x_ref[...] + g1 * _rms(y, gpm_ref[...])
    x1_ref[...] = x1
    h2_ref[...] = (_rms(x1, gpf_ref[...]) * (1.0 + sc2) + sh2).astype(BF16)


def _outproj(x2d, mod, ro, fo_halves, d_o, d_l, mo, w_out, g_post_mix, g_pre_ffn, seq):
    t = x2d.shape[0]
    tm = ROW_TILE
    tps = seq // tm
    row = lambda i: (i, 0)
    const = lambda i: (0, 0)
    mix = pl.BlockSpec((tm, MIX_W), row)
    res = lambda a: pl.BlockSpec((1, a.shape[1], tm // a.shape[1], a.shape[3]), lambda i: (i // tps, 0, i % tps, 0))
    hps = tps // 2
    flo = pl.BlockSpec((1, tm, MIX_W), lambda i: (i // tps, jnp.minimum(i % tps, hps - 1), 0))
    fhi = pl.BlockSpec((1, tm, MIX_W), lambda i: (i // tps, jnp.maximum(i % tps - hps, 0), 0))
    return pl.pallas_call(
        functools.partial(_outproj_kernel, tiles_per_seq=tps),
        out_shape=[jax.ShapeDtypeStruct((t, D_MODEL), F32), jax.ShapeDtypeStruct((t, D_MODEL), BF16)],
        grid=(t // tm,),
        in_specs=[
            pl.BlockSpec((tm, D_MODEL), row),
            pl.BlockSpec((1, 6, D_MODEL), lambda i: (i // tps, 0, 0)),
            mix, flo, fhi, res(d_o[0]), res(d_o[1]), res(d_o[2]), res(d_l[0]), res(d_l[1]), res(d_l[2]), mix,
            pl.BlockSpec((D_MODEL, D_MODEL), const),
            pl.BlockSpec((1, D_MODEL), const),
            pl.BlockSpec((1, D_MODEL), const),
        ],
        out_specs=[pl.BlockSpec((tm, D_MODEL), row), pl.BlockSpec((tm, D_MODEL), row)],
        scratch_shapes=[pltpu.VMEM((MIX_W // LANES, tm, LANES), F32), pltpu.VMEM((tm, LANES), F32),
                        pltpu.VMEM((MIX_W // LANES, tm, LANES), F32), pltpu.VMEM((tm, LANES), F32)],
        compiler_params=_params("arbitrary"),
        name="outproj",
    )(x2d, mod, ro, fo_halves[0], fo_halves[1], d_o[0], d_o[1], d_o[2], d_l[0], d_l[1], d_l[2], mo, w_out,
      g_post_mix, g_pre_ffn)


def _ffn_kernel(hp_ref, hc_ref, hn_ref, x1_ref, mod_ref, wu_ref, cw_ref, cb_ref, wd_ref, g_ref, o_ref,
                gate_ref, *, tiles_per_seq):
    tm = hc_ref.shape[0]
    t = pl.program_id(0) % tiles_per_seq
    hp = jnp.where(t == 0, jnp.zeros_like(hp_ref[...]), hp_ref[...])
    hn = jnp.where(t == tiles_per_seq - 1, jnp.zeros_like(hn_ref[...]), hn_ref[...])
    he = jnp.concatenate([hp, hc_ref[...], hn], axis=0)

    ext = tm + 2 * HALO

    def conv(c0):
        u = _dot(he, wu_ref[:, c0:c0 + FFN_CHUNK])
        w = cw_ref[:, c0:c0 + FFN_CHUNK]
        prev = pltpu.roll(u, 1, 0)[HALO:HALO + tm]
        nxt = pltpu.roll(u, ext - 1, 0)[HALO:HALO + tm]
        return prev * w[0:1] + u[HALO:HALO + tm] * w[1:2] + nxt * w[2:3] + cb_ref[:, c0:c0 + FFN_CHUNK]

    for c in range(D_FF // FFN_CHUNK):
        a = conv(c * FFN_CHUNK)
        bu = conv(D_FF + c * FFN_CHUNK)
        gate_ref[:, c * FFN_CHUNK:(c + 1) * FFN_CHUNK] = (a * _sigmoid(a) * bu).astype(BF16)
    acc = _dot(gate_ref[...], wd_ref[...])
    g2 = mod_ref[0, 5:6, :]
    o_ref[...] = x1_ref[...] + g2 * _rms(acc, g_ref[...])


def _ffn(h2, x1, mod, w_up, conv_w, conv_b, w_down, g_post_ffn, seq):
    t = x1.shape[0]
    tm = ROW_TILE
    tps = seq // tm
    hb = tm // HALO
    row = lambda i: (i, 0)
    const = lambda i: (0, 0)
    resident = lambda shape: pl.BlockSpec(shape, const, pipeline_mode=pl.Buffered(1))
    return pl.pallas_call(
        functools.partial(_ffn_kernel, tiles_per_seq=tps),
        out_shape=jax.ShapeDtypeStruct((t, D_MODEL), F32),
        grid=(t // tm,),
        in_specs=[
            pl.BlockSpec((HALO, D_MODEL), lambda i: (jnp.maximum(i * hb - 1, 0), 0)),
            pl.BlockSpec((tm, D_MODEL), row),
            pl.BlockSpec((HALO, D_MODEL), lambda i: (jnp.minimum((i + 1) * hb, t // HALO - 1), 0)),
            pl.BlockSpec((tm, D_MODEL), row),
            pl.BlockSpec((1, 6, D_MODEL), lambda i: (i // tps, 0, 0)),
            resident((D_MODEL, 2 * D_FF)),
            pl.BlockSpec((3, 2 * D_FF), const),
            pl.BlockSpec((1, 2 * D_FF), const),
            resident((D_FF, D_MODEL)),
            pl.BlockSpec((1, D_MODEL), const),
        ],
        out_specs=pl.BlockSpec((tm, D_MODEL), row),
        scratch_shapes=[pltpu.VMEM((tm, D_FF), BF16)],
        compiler_params=_params("arbitrary"),
        name="ffn",
    )(h2, h2, h2, x1, mod, w_up, conv_w, conv_b, w_down, g_post_ffn)


def _rope_tables(seq):
    pos = jnp.arange(seq, dtype=F32)[:, None]
    lane = np.arange(LANES)
    cols = []
    for theta, rot in ((RET_THETA, HEAD_DIM), (ROPE_THETA, PARTIAL_ROT), (ROPE_THETA, MLA_ROPE)):
        half = rot // 2
        inv = jnp.power(theta, -jnp.arange(half, dtype=F32) * 2.0 / rot)
        ang = pos * inv[lane % half][None, :]
        cols += [jnp.cos(ang), jnp.sin(ang)]
    return jnp.concatenate(cols, axis=1)


def _dft_tables(seq):
    n2 = 64
    n1 = seq // (2 * n2)
    k = np.arange(seq)[:, None]
    a = 2.0 * np.pi * ((k * np.arange(n1)[None, :] * n2) % seq) / seq
    b = 2.0 * np.pi * ((k * np.arange(n2)[None, :]) % seq) / seq
    ca, sa = jnp.asarray(np.cos(a), F32)[:, :, None], jnp.asarray(np.sin(a), F32)[:, :, None]
    cb, sb = jnp.asarray(np.cos(b), F32)[:, None, :], jnp.asarray(np.sin(b), F32)[:, None, :]
    cs = (ca * cb - sa * sb).reshape(seq, seq // 2).astype(BF16)
    ss = (sa * cb + ca * sb).reshape(seq, seq // 2).astype(BF16)
    return cs, ss


def _block_diag(blocks):
    n = len(blocks)
    rows = [jnp.concatenate([blocks[i] if i == j else jnp.zeros_like(blocks[i]) for j in range(n)], axis=1)
            for i in range(n)]
    return jnp.concatenate(rows, axis=0)


def _perm_w_in(w_in):
    ret = w_in[:, 0:1280]
    dq, dk, dv = w_in[:, 1280:2048], w_in[:, 2048:2816], w_in[:, 2816:3584]
    groups = [jnp.concatenate([m[:, g * MIX_W:(g + 1) * MIX_W] for m in (dq, dk, dv)], axis=1)
              for g in range(N_DIL_GROUPS)]
    pad = jnp.zeros((D_MODEL, D_IN_PAD - w_in.shape[1]), w_in.dtype)
    return jnp.concatenate([ret] + groups + [w_in[:, 3584:], pad], axis=1).astype(BF16)


def _mla_weights(w_qb, w_kvb):
    qh = w_qb.reshape(Q_LORA, N_HEADS, MLA_NOPE + MLA_ROPE)
    wq = jnp.pad(qh, ((0, 0), (0, 0), (0, LANES - MLA_NOPE - MLA_ROPE))).reshape(Q_LORA, N_HEADS * LANES)
    kvh = w_kvb.reshape(KV_LORA, N_HEADS, MLA_NOPE + HEAD_DIM)
    wk = jnp.pad(kvh[:, :, :MLA_NOPE], ((0, 0), (0, 0), (0, LANES - MLA_NOPE))).reshape(KV_LORA, N_HEADS * LANES)
    wv = kvh[:, :, MLA_NOPE:].reshape(KV_LORA, MIX_W)
    place = np.zeros((LANES, N_HEADS * LANES), np.float32)
    for h in range(N_HEADS):
        for r in range(MLA_ROPE):
            place[r, h * LANES + MLA_NOPE + r] = 1.0
    return wq.astype(BF16), wk.astype(BF16), wv.astype(BF16), jnp.asarray(place, BF16)


def _trunk(x, mods, layers, shared):
    b, seq, _ = x.shape
    t = b * seq
    x2d = x.reshape(t, D_MODEL)
    tab = _rope_tables(seq)
    cs, ss = _dft_tables(seq)
    for mod, lw in zip(mods, layers):
        ret, fu, dg0, dg1, dg2, mq, mk, mv = _inproj(x2d, mod, lw["g_pre_mix"], lw["w_in"], tab, lw["q_norm"],
                                                     lw["kv_norm"], lw["wq"], lw["wk"], lw["wv"],
                                                     shared["place"], seq)
        ro = _retention(ret.reshape(b, seq, 1024), lw["lg"]).reshape(t, MIX_W)
        fo = _fourier(fu.reshape(b, seq, MIX_W), cs, ss, shared["cc"], shared["sc"], lw["wf"])
        d_o, d_l = zip(*[_dilated(dg, g) for g, dg in enumerate((dg0, dg1, dg2))])
        mo = _mla(mq.reshape(b, seq, 512), mk.reshape(b, seq, 512), mv.reshape(b, seq, MIX_W)).reshape(t, MIX_W)
        x1, h2 = _outproj(x2d, mod, ro, fo, d_o, d_l, mo, lw["w_out"], lw["g_post_mix"], lw["g_pre_ffn"], seq)
        x2d = _ffn(h2, x1, mod, lw["w_up"], lw["conv_w"], lw["conv_b"], lw["w_down"], lw["g_post_ffn"], seq)
    return x2d.reshape(b, seq, D_MODEL)


def kernel(x_prompt, x_sample, c_prompt, c_sample, w_ada, b_ada, norm_pre_mix, w_in, ret_decay_fwd,
           ret_decay_bwd, w_fmix, mla_q_norm, mla_w_qb, mla_kv_norm, mla_w_kvb, w_out, norm_post_mix,
           norm_pre_ffn, w_up, conv_w, conv_b, w_down, norm_post_ffn):
    depth = w_in.shape[0]
    nb_p, nb_s = c_prompt.shape[0], c_sample.shape[0]
    rows = -(-(nb_p + nb_s) // 8) * 8
    c_all = jnp.concatenate([c_prompt, c_sample, jnp.zeros((rows - nb_p - nb_s, D_MODEL), F32)], axis=0)
    mod_all = _ada(c_all, w_ada, b_ada)
    mods_p = [mod_all[l, :nb_p].reshape(nb_p, 6, D_MODEL) for l in range(depth)]
    mods_s = [mod_all[l, nb_p:nb_p + nb_s].reshape(nb_s, 6, D_MODEL) for l in range(depth)]

    c64 = 2.0 * np.pi * np.outer(np.arange(HEAD_DIM), np.arange(HEAD_DIM)) / HEAD_DIM
    shared = {
        "cc": _block_diag([jnp.asarray(np.cos(c64), BF16)] * N_HEADS),
        "sc": _block_diag([jnp.asarray(np.sin(c64), BF16)] * N_HEADS),
    }
    layers = []
    for l in range(depth):
        wq, wk, wv, place = _mla_weights(mla_w_qb[l], mla_w_kvb[l])
        shared["place"] = place
        layers.append({
            "g_pre_mix": norm_pre_mix[l][None, :],
            "w_in": _perm_w_in(w_in[l]),
            "lg": jnp.stack([jax.nn.log_sigmoid(ret_decay_fwd[l]), jax.nn.log_sigmoid(ret_decay_bwd[l])]),
            "wf": _block_diag([w_fmix[l, g] for g in range(N_HEADS)]).astype(BF16),
            "q_norm": mla_q_norm[l][None, :],
            "kv_norm": mla_kv_norm[l][None, :],
            "wq": wq, "wk": wk, "wv": wv,
            "w_out": w_out[l].astype(BF16),
            "g_post_mix": norm_post_mix[l][None, :],
            "g_pre_ffn": norm_pre_ffn[l][None, :],
            "w_up": w_up[l].astype(BF16),
            "conv_w": conv_w[l],
            "conv_b": conv_b[l][None, :],
            "w_down": w_down[l].astype(BF16),
            "g_post_ffn": norm_post_ffn[l][None, :],
        })
    y_prompt = _trunk(x_prompt, mods_p, layers, shared)
    y_sample = _trunk(x_sample, mods_s, layers, shared)
    return (y_prompt, y_sample)
```
